```python
import math
import jax
import jax.numpy as jnp
from jax import lax
import numpy as np

D_MODEL = 2048
BATCH = 4
SEQ = 4096
DEPTH = 4

HEAD_DIM = 128
GRID_W = 64
ROPE_THETA = 10000.0
EPS = 1e-6
NEG_INF = -1e30

A_HEADS = 6
A_PATTERNS = ((128, 1), (512, 4), (2048, 16))
A_BAND_BLOCK = 64
B_Q_HEADS = 8
B_KV_HEADS = 2
B_Q_BLOCK = 128
C_HEADS = 8
C_QK_DIM = 64
C_V_DIM = 128
C_CHUNK = 64
C_CONV = 5
D_FF = 4 * D_MODEL
N_BRANCHES = 3

A_WIDTH = A_HEADS * HEAD_DIM
B_Q_WIDTH = B_Q_HEADS * HEAD_DIM
B_KV_WIDTH = B_KV_HEADS * HEAD_DIM
C_QK_WIDTH = C_HEADS * C_QK_DIM
C_V_WIDTH = C_HEADS * C_V_DIM
C_GATES = 4 * C_HEADS
IN_SPLITS = (A_WIDTH, A_WIDTH, A_WIDTH, B_Q_WIDTH, B_KV_WIDTH, B_KV_WIDTH, 2 * C_QK_WIDTH, C_V_WIDTH, C_V_WIDTH, C_GATES, N_BRANCHES * D_MODEL)
IN_COLS = sum(IN_SPLITS)

kernel_name = 'hybrid_dilated_gqa_mlstm_encoder'


def rms_norm(x, g):
    xf = x.astype(jnp.float32)
    y = xf * lax.rsqrt(jnp.mean(xf * xf, axis=-1, keepdims=True) + EPS)
    return (y * g.astype(jnp.float32)).astype(x.dtype)


def rope_tables(pos, dim, dtype):
    inv_freq = ROPE_THETA ** (-jnp.arange(0, dim, 2, dtype=jnp.float32) / dim)
    ang = pos.astype(jnp.float32)[:, None] * inv_freq[None, :]
    return jnp.cos(ang).astype(dtype), jnp.sin(ang).astype(dtype)


def apply_rope(x, cos, sin):
    half = x.shape[-1] // 2
    x1, x2 = x[..., :half], x[..., half:]
    c, s = cos[None, :, None, :], sin[None, :, None, :]
    return jnp.concatenate([x1 * c - x2 * s, x2 * c + x1 * s], axis=-1)


def dilated_band_attention(q, k, v, dilation, reach):
    bsz, seq, heads, hd = q.shape
    L = seq // dilation
    blk = A_BAND_BLOCK
    nb = -(-L // blk)
    Lp = nb * blk

    def by_residue(t):
        return t.reshape(bsz, L, dilation, heads, hd).transpose(0, 2, 3, 1, 4)

    qr, kr, vr = by_residue(q), by_residue(k), by_residue(v)
    qb = jnp.pad(qr, ((0, 0), (0, 0), (0, 0), (0, Lp - L), (0, 0))).reshape(bsz, dilation, heads, nb, blk, hd)
    pad_k = ((0, 0), (0, 0), (0, 0), (blk, Lp - L + blk), (0, 0))

    def band(t):
        tb = jnp.pad(t, pad_k).reshape(bsz, dilation, heads, nb + 2, blk, hd)
        return jnp.concatenate([tb[:, :, :, :-2], tb[:, :, :, 1:-1], tb[:, :, :, 2:]], axis=4)

    kw, vw = band(kr), band(vr)
    qpos = jnp.arange(nb)[:, None, None] * blk + jnp.arange(blk)[None, :, None]
    kpos = jnp.arange(nb)[:, None, None] * blk + jnp.arange(3 * blk)[None, None, :] - blk
    valid = (jnp.abs(kpos - qpos) <= reach) & (kpos >= 0) & (kpos < L)
    s = jnp.einsum('bdhiqc,bdhikc->bdhiqk', qb, kw).astype(jnp.float32) * (hd ** -0.5)
    s = jnp.where(valid, s, NEG_INF)
    m = jnp.max(s, axis=-1, keepdims=True)
    p = jnp.exp(s - m)
    z = jnp.sum(p, axis=-1, keepdims=True)
    o = jnp.einsum('bdhiqk,bdhikc->bdhiqc', (p / z).astype(v.dtype), vw)
    lse = (m + jnp.log(z))[..., 0]
    o = o.reshape(bsz, dilation, heads, Lp, hd)[:, :, :, :L].transpose(0, 3, 1, 2, 4).reshape(bsz, seq, heads, hd)
    lse = lse.reshape(bsz, dilation, heads, Lp)[:, :, :, :L].transpose(0, 3, 1, 2).reshape(bsz, seq, heads)
    return o, lse


def dilated_mixer(q, k, v, q_gain, k_gain, rope_1d):
    bsz, seq, _ = q.shape
    shp = (bsz, seq, A_HEADS, HEAD_DIM)
    cos, sin = rope_1d
    q = apply_rope(rms_norm(q.reshape(shp), q_gain), cos, sin)
    k = apply_rope(rms_norm(k.reshape(shp), k_gain), cos, sin)
    v = v.reshape(shp)
    outs, lses = [], []
    for window, dilation in A_PATTERNS:
        o, lse = dilated_band_attention(q, k, v, dilation, window // (2 * dilation))
        outs.append(o)
        lses.append(lse)
    w = jax.nn.softmax(jnp.stack(lses), axis=0)
    o = jnp.einsum('gbsh,gbshc->bshc', w, jnp.stack(outs).astype(jnp.float32))
    return o.astype(v.dtype).reshape(bsz, seq, A_WIDTH)


def gqa_axial_mixer(q, k, v, q_gain, k_gain, rope_row, rope_col):
    bsz, seq, _ = q.shape
    half = HEAD_DIM // 2
    group = B_Q_HEADS // B_KV_HEADS

    def axial(t):
        return jnp.concatenate([apply_rope(t[..., :half], *rope_row), apply_rope(t[..., half:], *rope_col)], axis=-1)

    q = axial(rms_norm(q.reshape(bsz, seq, B_Q_HEADS, HEAD_DIM), q_gain))
    k = axial(rms_norm(k.reshape(bsz, seq, B_KV_HEADS, HEAD_DIM), k_gain))
    v = v.reshape(bsz, seq, B_KV_HEADS, HEAD_DIM)
    nb = seq // B_Q_BLOCK
    qb = q.reshape(bsz, nb, B_Q_BLOCK, B_KV_HEADS, group, HEAD_DIM).transpose(1, 0, 3, 4, 2, 5)
    scale = HEAD_DIM ** -0.5

    def attend(q_blk):
        s = jnp.einsum('bhgqc,bkhc->bhgqk', q_blk, k).astype(jnp.float32) * scale
        p = jax.nn.softmax(s, axis=-1).astype(v.dtype)
        return jnp.einsum('bhgqk,bkhc->bhgqc', p, v)

    o = lax.map(attend, qb)
    return o.transpose(1, 0, 4, 2, 3, 5).reshape(bsz, seq, B_Q_WIDTH)


def centred_depthwise_conv(x, w, b):
    chans = x.shape[-1]
    pad = C_CONV // 2
    y = lax.conv_general_dilated(x, w[:, None, :], window_strides=(1,), padding=[(pad, pad)],
                                 dimension_numbers=('NWC', 'WIO', 'NWC'), feature_group_count=chans)
    return y + b


def mlstm_chunkwise(q, k, v, log_f, log_i):
    bsz, heads, seq, dqk = q.shape
    dv = v.shape[-1]
    L = C_CHUNK
    nc = seq // L

    def chunks(t):
        return jnp.moveaxis(t.reshape(bsz, heads, nc, L, *t.shape[3:]), 2, 0)

    tril = jnp.tril(jnp.ones((L, L), dtype=bool))

    def step(carry, inp):
        C, n, m = carry
        qc, kc, vc, lf, ic = inp
        b = jnp.cumsum(lf, axis=-1)
        D = jnp.where(tril, b[..., :, None] - b[..., None, :] + ic[..., None, :], NEG_INF)
        m_inter = b + m[..., None]
        m_t = jnp.maximum(m_inter, jnp.max(D, axis=-1))
        A = jnp.exp(D - m_t[..., None]) * jnp.einsum('bhtd,bhsd->bhts', qc, kc)
        inter = jnp.exp(m_inter - m_t)
        num = jnp.einsum('bhts,bhsv->bhtv', A, vc) + inter[..., None] * jnp.einsum('bhvd,bhtd->bhtv', C, qc)
        den = jnp.sum(A, axis=-1) + inter * jnp.einsum('bhd,bhtd->bht', n, qc)
        h = num / jnp.maximum(jnp.abs(den), jnp.exp(-m_t))[..., None]
        b_last = b[..., -1]
        g = b_last[..., None] - b + ic
        m_new = jnp.maximum(b_last + m, jnp.max(g, axis=-1))
        decay = jnp.exp(b_last + m - m_new)
        wk = jnp.exp(g - m_new[..., None])
        C_new = decay[..., None, None] * C + jnp.einsum('bhsv,bhsd->bhvd', vc * wk[..., None], kc)
        n_new = decay[..., None] * n + jnp.einsum('bhs,bhsd->bhd', wk, kc)
        return (C_new, n_new, m_new), h

    init = (jnp.zeros((bsz, heads, dv, dqk), jnp.float32), jnp.zeros((bsz, heads, dqk), jnp.float32),
            jnp.zeros((bsz, heads), jnp.float32))
    _, h = lax.scan(step, init, (chunks(q), chunks(k), chunks(v), chunks(log_f), chunks(log_i)))
    return jnp.moveaxis(h, 0, 2).reshape(bsz, heads, seq, dv)


def mlstm_mixer(qk_pre, v, o_pre, gate_pre, conv_w, conv_b, gate_b, out_g):
    bsz, seq, _ = qk_pre.shape
    qk = jax.nn.silu(centred_depthwise_conv(qk_pre, conv_w, conv_b))
    to_heads = lambda t, d: t.reshape(bsz, seq, C_HEADS, d).transpose(0, 2, 1, 3).astype(jnp.float32)
    q = to_heads(qk[..., :C_QK_WIDTH], C_QK_DIM)
    k = to_heads(qk[..., C_QK_WIDTH:], C_QK_DIM) * (C_QK_DIM ** -0.5)
    vh = to_heads(v, C_V_DIM)
    gates = (gate_pre + gate_b).astype(jnp.float32).reshape(bsz, seq, 4, C_HEADS).transpose(2, 0, 3, 1)
    i_fw, f_fw, i_bw, f_bw = gates[0], gates[1], gates[2], gates[3]
    h_fw = mlstm_chunkwise(q, k, vh, jax.nn.log_sigmoid(f_fw), i_fw)
    flip = lambda t: jnp.flip(t, axis=2)
    h_bw = flip(mlstm_chunkwise(flip(q), flip(k), flip(vh), jax.nn.log_sigmoid(flip(f_bw)), flip(i_bw)))
    h = (h_fw + h_bw).transpose(0, 2, 1, 3)
    h = h * lax.rsqrt(jnp.mean(h * h, axis=-1, keepdims=True) + EPS)
    h = h.reshape(bsz, seq, C_V_WIDTH) * out_g.astype(jnp.float32)
    return (h * jax.nn.sigmoid(o_pre.astype(jnp.float32))).astype(qk_pre.dtype)


def hybrid_layer(x, rope_1d, rope_row, rope_col, norm1_g, w_in, a_q_gain, a_k_gain, b_q_gain, b_k_gain,
                 c_conv_w, c_conv_b, c_gate_b, c_out_g, w_branch_a, w_branch_b, w_branch_c, w_out,
                 norm2_g, w_up, w_down):
    bsz, seq, _ = x.shape
    xn = rms_norm(x, norm1_g)
    split_points = [int(p) for p in np.cumsum(IN_SPLITS)[:-1]]
    (a_q, a_k, a_v, b_q, b_k, b_v, c_qk, c_v, c_o, c_gate, g) = jnp.split(xn @ w_in, split_points, axis=-1)
    ya = dilated_mixer(a_q, a_k, a_v, a_q_gain, a_k_gain, rope_1d)
    yb = gqa_axial_mixer(b_q, b_k, b_v, b_q_gain, b_k_gain, rope_row, rope_col)
    yc = mlstm_mixer(c_qk, c_v, c_o, c_gate, c_conv_w, c_conv_b, c_gate_b, c_out_g)
    gates = jax.nn.sigmoid(g.astype(jnp.float32)).astype(x.dtype).reshape(bsz, seq, N_BRANCHES, D_MODEL)
    merged = (gates[:, :, 0] * (ya @ w_branch_a) + gates[:, :, 1] * (yb @ w_branch_b)
              + gates[:, :, 2] * (yc @ w_branch_c))
    h = x + merged @ w_out
    u = jnp.maximum(rms_norm(h, norm2_g) @ w_up, 0)
    return h + (u * u) @ w_down


def setup_inputs(seed: int = 0) -> dict:
    key = jax.random.key(seed)
    ks = jax.random.split(key, 20)
    f32 = jnp.float32
    nrm = lambda k, shape, fan_in: jax.random.normal(k, shape, f32) * (fan_in ** -0.5)
    gain = lambda k, shape: 1.0 + 0.02 * jax.random.normal(k, shape, f32)
    f_base = jnp.linspace(3.0, 6.0, C_HEADS, dtype=f32)
    i_base = jnp.zeros((C_HEADS,), f32)
    gate_base = jnp.concatenate([i_base, f_base, i_base, f_base])
    return {
        'x': jax.random.normal(ks[0], (BATCH, SEQ, D_MODEL), f32),
        'norm1_g': gain(ks[1], (DEPTH, D_MODEL)),
        'w_in': nrm(ks[2], (DEPTH, D_MODEL, IN_COLS), D_MODEL),
        'a_q_gain': gain(ks[3], (DEPTH, HEAD_DIM)),
        'a_k_gain': gain(ks[4], (DEPTH, HEAD_DIM)),
        'b_q_gain': gain(ks[5], (DEPTH, HEAD_DIM)),
        'b_k_gain': gain(ks[6], (DEPTH, HEAD_DIM)),
        'c_conv_w': nrm(ks[7], (DEPTH, C_CONV, 2 * C_QK_WIDTH), C_CONV),
        'c_conv_b': 0.02 * jax.random.normal(ks[8], (DEPTH, 2 * C_QK_WIDTH), f32),
        'c_gate_b': gate_base[None, :] + 0.1 * jax.random.normal(ks[9], (DEPTH, C_GATES), f32),
        'c_out_g': gain(ks[10], (DEPTH, C_V_WIDTH)),
        'w_branch_a': nrm(ks[11], (DEPTH, A_WIDTH, D_MODEL), A_WIDTH),
        'w_branch_b': nrm(ks[12], (DEPTH, B_Q_WIDTH, D_MODEL), B_Q_WIDTH),
        'w_branch_c': nrm(ks[13], (DEPTH, C_V_WIDTH, D_MODEL), C_V_WIDTH),
        'w_out': nrm(ks[14], (DEPTH, D_MODEL, D_MODEL), D_MODEL),
        'norm2_g': gain(ks[15], (DEPTH, D_MODEL)),
        'w_up': nrm(ks[16], (DEPTH, D_MODEL, D_FF), D_MODEL),
        'w_down': nrm(ks[17], (DEPTH, D_FF, D_MODEL), D_FF),
    }


def reference(x, norm1_g, w_in, a_q_gain, a_k_gain, b_q_gain, b_k_gain, c_conv_w, c_conv_b, c_gate_b,
              c_out_g, w_branch_a, w_branch_b, w_branch_c, w_out, norm2_g, w_up, w_down):
    seq = x.shape[1]
    rows = seq // GRID_W
    pos = jnp.arange(seq)
    row = jnp.repeat(jnp.arange(rows), GRID_W)
    col = jnp.tile(jnp.arange(GRID_W), rows)
    rope_1d = rope_tables(pos, HEAD_DIM, x.dtype)
    rope_row = rope_tables(row, HEAD_DIM // 2, x.dtype)
    rope_col = rope_tables(col, HEAD_DIM // 2, x.dtype)
    for l in range(DEPTH):
        x = hybrid_layer(x, rope_1d, rope_row, rope_col, norm1_g[l], w_in[l], a_q_gain[l], a_k_gain[l],
                         b_q_gain[l], b_k_gain[l], c_conv_w[l], c_conv_b[l], c_gate_b[l], c_out_g[l],
                         w_branch_a[l], w_branch_b[l], w_branch_c[l], w_out[l], norm2_g[l], w_up[l], w_down[l])
    return x
```

```python
import functools

import jax
import jax.numpy as jnp
from jax import lax
from jax.experimental import pallas as pl
from jax.experimental.pallas import tpu as pltpu

F32 = jnp.float32
BF16 = jnp.bfloat16

HEAD_DIM = 128
GRID_W = 64
ROPE_THETA = 10000.0
EPS = 1e-6
NEG_INF = -1e30

A_HEADS = 6
A_PATTERNS = ((128, 1), (512, 4), (2048, 16))
A_REACH_BLOCK = 64
B_Q_HEADS = 8
B_KV_HEADS = 2
C_HEADS = 8
C_QK_DIM = 64
C_V_DIM = 128
C_CONV = 5
N_BRANCHES = 3

A_WIDTH = A_HEADS * HEAD_DIM
B_Q_WIDTH = B_Q_HEADS * HEAD_DIM
B_KV_WIDTH = B_KV_HEADS * HEAD_DIM
C_QK_WIDTH = C_HEADS * C_QK_DIM
C_V_WIDTH = C_HEADS * C_V_DIM

LANES = 128
MLSTM_CHUNK = 128
C_PAIRS = C_HEADS // 2
C_STREAMS = 2 * C_HEADS
VMEM_LIMIT = 56 * 1024 * 1024

Y_G = 0
Y_CQK = 6144
Y_CO = 7168
Y_BQ = 8192
Y_AQ = 9216
Y_AK = 9984
Y_BK = 10752
Y_GI = 11008
Y_GF = 11136
Y_COLS = 11264
V_BV = 0
V_CV = 256
V_COLS = 1280


def _cparams(sem):
    return pltpu.CompilerParams(dimension_semantics=sem, vmem_limit_bytes=VMEM_LIMIT)


def _rmsnorm_kernel(x_ref, g_ref, o_ref):
    x = x_ref[...]
    r = lax.rsqrt(jnp.mean(x * x, axis=-1, keepdims=True) + EPS)
    o_ref[...] = ((x * r) * g_ref[...]).astype(o_ref.dtype)


def rmsnorm(x, g, tm=512):
    m, d = x.shape
    return pl.pallas_call(
        _rmsnorm_kernel,
        grid=(m // tm,),
        in_specs=[pl.BlockSpec((tm, d), lambda i: (i, 0)), pl.BlockSpec((1, d), lambda i: (0, 0))],
        out_specs=pl.BlockSpec((tm, d), lambda i: (i, 0)),
        out_shape=jax.ShapeDtypeStruct((m, d), BF16),
        compiler_params=_cparams(("parallel",)),
        name="rmsnorm",
    )(x, g.reshape(1, d))


def _matmul_kernel(*refs, nk, relu2, has_res):
    if has_res:
        a_ref, w_ref, r_ref, o_ref, acc_ref = refs
    else:
        a_ref, w_ref, o_ref, acc_ref = refs
        r_ref = None

    def finish(acc):
        if relu2:
            u = jnp.maximum(acc, 0.0)
            acc = u * u
        if has_res:
            acc = r_ref[...] + acc
        o_ref[...] = acc.astype(o_ref.dtype)

    prod = jnp.dot(a_ref[...], w_ref[...], preferred_element_type=F32)
    if nk == 1:
        finish(prod)
    else:
        k = pl.program_id(2)

        @pl.when(k == 0)
        def _():
            acc_ref[...] = prod

        @pl.when(k > 0)
        def _():
            acc_ref[...] += prod

        @pl.when(k == nk - 1)
        def _():
            finish(acc_ref[...])


def matmul(a, w, *, out_dtype, res=None, relu2=False, tm=1024, tn=1024, tk=2048):
    m, kdim = a.shape
    n = w.shape[1]
    tn = min(tn, n)
    tk = min(tk, kdim)
    nk = kdim // tk
    in_specs = [pl.BlockSpec((tm, tk), lambda i, j, k: (i, k)),
                pl.BlockSpec((tk, tn), lambda i, j, k: (k, j))]
    args = [a, w]
    if res is not None:
        in_specs.append(pl.BlockSpec((tm, tn), lambda i, j, k: (i, j)))
        args.append(res)
    return pl.pallas_call(
        functools.partial(_matmul_kernel, nk=nk, relu2=relu2, has_res=res is not None),
        grid=(m // tm, n // tn, nk),
        in_specs=in_specs,
        out_specs=pl.BlockSpec((tm, tn), lambda i, j, k: (i, j)),
        out_shape=jax.ShapeDtypeStruct((m, n), out_dtype),
        scratch_shapes=[pltpu.VMEM((tm, tn) if nk > 1 else (8, LANES), F32)],
        compiler_params=_cparams(("parallel", "parallel", "arbitrary")),
        name="matmul",
    )(*args)


def _qk_prep_kernel(x_ref, g_ref, c_ref, s_ref, o_ref, *, half):
    x = x_ref[0]
    r = lax.rsqrt(jnp.mean(x * x, axis=-1, keepdims=True) + EPS)
    y = (x * r) * g_ref[0]
    if 2 * half == LANES:
        partner = pltpu.roll(y, half, 1)
    else:
        lane = lax.broadcasted_iota(jnp.int32, y.shape, 1)
        partner = jnp.where(lane % (2 * half) < half, pltpu.roll(y, LANES - half, 1), pltpu.roll(y, half, 1))
    o_ref[0] = (y * c_ref[...] + partner * s_ref[...]).astype(o_ref.dtype)


def qk_prep(y3, col_of_head, gains, cos_full, sin_signed, *, half, tile=512):
    b, s, _ = y3.shape
    nh = gains.shape[0]
    return pl.pallas_call(
        functools.partial(_qk_prep_kernel, half=half),
        grid=(b, s // tile, nh),
        in_specs=[pl.BlockSpec((1, tile, HEAD_DIM), lambda bi, ti, h: (bi, ti, col_of_head(h))),
                  pl.BlockSpec((1, 1, HEAD_DIM), lambda bi, ti, h: (h, 0, 0)),
                  pl.BlockSpec((tile, HEAD_DIM), lambda bi, ti, h: (ti, 0)),
                  pl.BlockSpec((tile, HEAD_DIM), lambda bi, ti, h: (ti, 0))],
        out_specs=pl.BlockSpec((1, tile, HEAD_DIM), lambda bi, ti, h: (bi, ti, h)),
        out_shape=jax.ShapeDtypeStruct((b, s, nh * HEAD_DIM), BF16),
        compiler_params=_cparams(("parallel", "parallel", "arbitrary")),
        name="qk_prep",
    )(y3, gains.reshape(nh, 1, HEAD_DIM), cos_full, sin_signed)


def _band_kernel(q_ref, kp_ref, km_ref, kn_ref, vp_ref, vm_ref, vn_ref, o_ref, lse_ref, kbuf, vbuf, *, tq, sub_len):
    blk = A_REACH_BLOCK
    ti = pl.program_id(2)
    kbuf[0:blk] = kp_ref[0]
    kbuf[blk:blk + tq] = km_ref[0]
    kbuf[blk + tq:] = kn_ref[0]
    vbuf[0:blk] = vp_ref[0]
    vbuf[blk:blk + tq] = vm_ref[0]
    vbuf[blk + tq:] = vn_ref[0]
    scale = HEAD_DIM ** -0.5
    lane = lax.broadcasted_iota(jnp.int32, (blk, LANES), 1)
    for bi in range(tq // blk):
        base = ti * tq + bi * blk
        qpos = base + lax.broadcasted_iota(jnp.int32, (blk, 3 * blk), 0)
        kpos = base - blk + lax.broadcasted_iota(jnp.int32, (blk, 3 * blk), 1)
        valid = (jnp.abs(kpos - qpos) <= blk) & (kpos >= 0) & (kpos < sub_len)
        lse_tile = jnp.zeros((blk, LANES), F32)
        for h in range(A_HEADS):
            cols = slice(h * HEAD_DIM, (h + 1) * HEAD_DIM)
            q = q_ref[0, bi * blk:(bi + 1) * blk, cols]
            kw = kbuf[bi * blk:(bi + 3) * blk, cols]
            vw = vbuf[bi * blk:(bi + 3) * blk, cols]
            s = lax.dot_general(q, kw, (((1,), (1,)), ((), ())), preferred_element_type=F32) * scale
            s = jnp.where(valid, s, NEG_INF)
            m = jnp.max(s, axis=-1, keepdims=True)
            p = jnp.exp(s - m)
            z = jnp.sum(p, axis=-1, keepdims=True)
            o = jnp.dot(p.astype(BF16), vw, preferred_element_type=F32) / z
            o_ref[0, bi * blk:(bi + 1) * blk, cols] = o
            lse_tile = jnp.where(lane == h, m + jnp.log(z), lse_tile)
        lse_ref[0, bi * blk:(bi + 1) * blk, :] = lse_tile


def band_attention(qk, v, dilation):
    b, s, _ = qk.shape
    d = dilation
    sub_len = s // d
    blk = A_REACH_BLOCK
    tq = min(sub_len, 512)
    nt = sub_len // tq
    nblk = sub_len // blk
    per = tq // blk
    qk_v = qk.reshape(b, sub_len, d * 2 * A_WIDTH)
    v_v = v.reshape(b, sub_len, d * A_WIDTH)

    def main(off, width):
        return pl.BlockSpec((1, tq, A_WIDTH), lambda bi, r, t: (bi, t, r * width + off))

    def prev(off, width):
        return pl.BlockSpec((1, blk, A_WIDTH), lambda bi, r, t: (bi, jnp.maximum(t * per - 1, 0), r * width + off))

    def nxt(off, width):
        return pl.BlockSpec((1, blk, A_WIDTH), lambda bi, r, t: (bi, jnp.minimum((t + 1) * per, nblk - 1), r * width + off))

    o, lse = pl.pallas_call(
        functools.partial(_band_kernel, tq=tq, sub_len=sub_len),
        grid=(b, d, nt),
        in_specs=[main(0, 2), prev(1, 2), main(1, 2), nxt(1, 2), prev(0, 1), main(0, 1), nxt(0, 1)],
        out_specs=[pl.BlockSpec((1, tq, A_WIDTH), lambda bi, r, t: (bi, t, r)),
                   pl.BlockSpec((1, tq, LANES), lambda bi, r, t: (bi, t, r))],
        out_shape=[jax.ShapeDtypeStruct((b, sub_len, d * A_WIDTH), F32),
                   jax.ShapeDtypeStruct((b, sub_len, d * LANES), F32)],
        scratch_shapes=[pltpu.VMEM((tq + 2 * blk, A_WIDTH), BF16), pltpu.VMEM((tq + 2 * blk, A_WIDTH), BF16)],
        compiler_params=_cparams(("parallel", "parallel", "parallel")),
        name="band_attention",
    )(qk_v, qk_v, qk_v, qk_v, v_v, v_v, v_v)
    return o.reshape(b, s, A_WIDTH), lse.reshape(b, s, LANES)


def _combine_kernel(o1_ref, o2_ref, o3_ref, l1_ref, l2_ref, l3_ref, y_ref):
    ls = [l1_ref[...], l2_ref[...], l3_ref[...]]
    for h in range(A_HEADS):
        cols = slice(h * HEAD_DIM, (h + 1) * HEAD_DIM)
        l = [x[:, h:h + 1] for x in ls]
        mx = jnp.maximum(jnp.maximum(l[0], l[1]), l[2])
        e = [jnp.exp(x - mx) for x in l]
        den = e[0] + e[1] + e[2]
        y = (e[0] / den) * o1_ref[:, cols] + (e[1] / den) * o2_ref[:, cols] + (e[2] / den) * o3_ref[:, cols]
        y_ref[:, cols] = y.astype(y_ref.dtype)


def combine_patterns(outs, lses, tm=512):
    m = outs[0].shape[0]
    o_spec = pl.BlockSpec((tm, A_WIDTH), lambda i: (i, 0))
    l_spec = pl.BlockSpec((tm, LANES), lambda i: (i, 0))
    return pl.pallas_call(
        _combine_kernel,
        grid=(m // tm,),
        in_specs=[o_spec] * 3 + [l_spec] * 3,
        out_specs=pl.BlockSpec((tm, A_WIDTH), lambda i: (i, 0)),
        out_shape=jax.ShapeDtypeStruct((m, A_WIDTH), BF16),
        compiler_params=_cparams(("parallel",)),
        name="combine_patterns",
    )(*outs, *lses)


def _gqa_kernel(q_ref, k_ref, v_ref, o_ref, *, group):
    k = k_ref[0]
    v = v_ref[0]
    scale = HEAD_DIM ** -0.5
    for g in range(group):
        cols = slice(g * HEAD_DIM, (g + 1) * HEAD_DIM)
        s = lax.dot_general(q_ref[0, :, cols], k, (((1,), (1,)), ((), ())), preferred_element_type=F32) * scale
        m = jnp.max(s, axis=-1, keepdims=True)
        p = jnp.exp(s - m)
        l = jnp.sum(p, axis=-1, keepdims=True)
        o = jnp.dot(p.astype(BF16), v, preferred_element_type=F32) / l
        o_ref[0, :, cols] = o.astype(o_ref.dtype)


def gqa_attention(qk, v3, tq=256):
    b, s, _ = qk.shape
    group = B_Q_HEADS // B_KV_HEADS
    gw = group * HEAD_DIM
    return pl.pallas_call(
        functools.partial(_gqa_kernel, group=group),
        grid=(b, B_KV_HEADS, s // tq),
        in_specs=[pl.BlockSpec((1, tq, gw), lambda bi, h, t: (bi, t, h)),
                  pl.BlockSpec((1, s, HEAD_DIM), lambda bi, h, t: (bi, 0, B_Q_HEADS + h)),
                  pl.BlockSpec((1, s, HEAD_DIM), lambda bi, h, t: (bi, 0, V_BV // HEAD_DIM + h))],
        out_specs=pl.BlockSpec((1, tq, gw), lambda bi, h, t: (bi, t, h)),
        out_shape=jax.ShapeDtypeStruct((b, s, B_Q_WIDTH), BF16),
        compiler_params=_cparams(("parallel", "parallel", "arbitrary")),
        name="gqa_attention",
    )(qk, qk, v3)


def _conv_kernel(xp_ref, xm_ref, xn_ref, w_ref, b_ref, qk_ref, kt_ref, buf, *, tile, ntiles):
    ti = pl.program_id(1)
    halo = 8
    buf[0:halo] = jnp.where(ti > 0, xp_ref[0], 0.0)
    buf[halo:halo + tile] = xm_ref[0]
    buf[halo + tile:] = jnp.where(ti < ntiles - 1, xn_ref[0], 0.0)
    pad = C_CONV // 2
    acc = w_ref[0:1, :] * buf[halo - pad:halo - pad + tile, :]
    for j in range(1, C_CONV):
        acc = acc + w_ref[j:j + 1, :] * buf[halo - pad + j:halo - pad + j + tile, :]
    acc = acc + b_ref[...]
    act = acc * jax.nn.sigmoid(acc)
    q = act[:, :C_QK_WIDTH]
    k = act[:, C_QK_WIDTH:] * (C_QK_DIM ** -0.5)
    qk_ref[0, :, :C_QK_WIDTH] = q.astype(qk_ref.dtype)
    qk_ref[0, :, C_QK_WIDTH:] = k.astype(qk_ref.dtype)
    for c in range(tile // MLSTM_CHUNK):
        kt_ref[0, c] = k[c * MLSTM_CHUNK:(c + 1) * MLSTM_CHUNK, :].T.astype(kt_ref.dtype)


def mlstm_conv(y3, conv_w, conv_b, tile=512):
    b, s, _ = y3.shape
    ntiles = s // tile
    w2 = 2 * C_QK_WIDTH
    cb = Y_CQK // w2
    per = tile // 8
    return pl.pallas_call(
        functools.partial(_conv_kernel, tile=tile, ntiles=ntiles),
        grid=(b, ntiles),
        in_specs=[pl.BlockSpec((1, 8, w2), lambda bi, t: (bi, jnp.maximum(t * per - 1, 0), cb)),
                  pl.BlockSpec((1, tile, w2), lambda bi, t: (bi, t, cb)),
                  pl.BlockSpec((1, 8, w2), lambda bi, t: (bi, jnp.minimum((t + 1) * per, s // 8 - 1), cb)),
                  pl.BlockSpec((C_CONV, w2), lambda bi, t: (0, 0)),
                  pl.BlockSpec((1, w2), lambda bi, t: (0, 0))],
        out_specs=[pl.BlockSpec((1, tile, w2), lambda bi, t: (bi, t, 0)),
                   pl.BlockSpec((1, tile // MLSTM_CHUNK, C_QK_WIDTH, MLSTM_CHUNK), lambda bi, t: (bi, t, 0, 0))],
        out_shape=[jax.ShapeDtypeStruct((b, s, w2), BF16),
                   jax.ShapeDtypeStruct((b, s // MLSTM_CHUNK, C_QK_WIDTH, MLSTM_CHUNK), BF16)],
        scratch_shapes=[pltpu.VMEM((tile + 16, w2), F32)],
        compiler_params=_cparams(("parallel", "parallel")),
        name="mlstm_conv",
    )(y3, y3, y3, conv_w, conv_b.reshape(1, w2))


def _gate_kernel(gi_ref, gf_ref, bi_ref, bf_ref, cb_ref, inter_ref, e_ref, rt_ref, wkt_ref, dec_ref,
                 b_s, r_s, dmax_s, btot_s, gmax_s, mpf_s, mpb_s, *, nc):
    lc = MLSTM_CHUNK
    t_in = lax.broadcasted_iota(jnp.int32, (lc, LANES), 0)
    fwd = lax.broadcasted_iota(jnp.int32, (lc, LANES), 1) % 2 == 0
    fwd_row = fwd[0:1]
    steps = [1 << i for i in range(lc.bit_length() - 1)]

    def chunk_rows(c):
        return pl.ds(pl.multiple_of(c * lc, lc), lc)

    def local(c, carry):
        rows = chunk_rows(c)
        ii = gi_ref[0, rows, :] + bi_ref[...]
        ff = gf_ref[0, rows, :] + bf_ref[...]
        lf = jnp.minimum(ff, 0.0) - jnp.log1p(jnp.exp(-jnp.abs(ff)))
        cs = lf
        rs = lf
        for k in steps:
            cs = cs + jnp.where(t_in >= k, pltpu.roll(cs, k, 0), 0.0)
            rs = rs + jnp.where(t_in + k < lc, pltpu.roll(rs, lc - k, 0), 0.0)
        b = jnp.where(fwd, cs, rs)
        btot = cs + rs - lf
        r = ii - b
        pm = r
        sm = r
        for k in steps:
            pm = jnp.where(t_in >= k, jnp.maximum(pm, pltpu.roll(pm, k, 0)), pm)
            sm = jnp.where(t_in + k < lc, jnp.maximum(sm, pltpu.roll(sm, lc - k, 0)), sm)
        b_s[rows, :] = b
        r_s[rows, :] = r
        dmax_s[rows, :] = b + jnp.where(fwd, pm, sm)
        btot_s[pl.ds(c, 1), :] = btot[0:1]
        gmax_s[pl.ds(c, 1), :] = (btot + jnp.maximum(pm, sm))[0:1]
        return carry

    lax.fori_loop(0, nc, local, 0)

    def scan(k, carry):
        mf, mb = carry
        mpf_s[pl.ds(k, 1), :] = mf
        mf = jnp.maximum(btot_s[pl.ds(k, 1), :] + mf, gmax_s[pl.ds(k, 1), :])
        kb = nc - 1 - k
        mpb_s[pl.ds(kb, 1), :] = mb
        mb = jnp.maximum(btot_s[pl.ds(kb, 1), :] + mb, gmax_s[pl.ds(kb, 1), :])
        return mf, mb

    zero = jnp.zeros((1, LANES), F32)
    lax.fori_loop(0, nc, scan, (zero, zero))

    def emit(c, carry):
        rows = chunk_rows(c)
        b = b_s[rows, :]
        r = r_s[rows, :]
        btot = btot_s[pl.ds(c, 1), :]
        mprev = jnp.where(fwd_row, mpf_s[pl.ds(c, 1), :], mpb_s[pl.ds(c, 1), :])
        mnew = jnp.maximum(btot + mprev, gmax_s[pl.ds(c, 1), :])
        dec_ref[0, pl.ds(c, 1), :] = jnp.exp(btot + mprev - mnew)
        m_inter = b + mprev
        m_t = jnp.maximum(m_inter, dmax_s[rows, :])
        cb_ref[0, rows, :] = b - m_t
        inter_ref[0, rows, :] = jnp.exp(m_inter - m_t)
        e_ref[0, rows, :] = jnp.exp(-m_t)
        wk = jnp.exp(btot + r - mnew)
        rt_ref[0, c] = r.T[:C_STREAMS, :]
        wkt_ref[0, c] = wk.T[:C_STREAMS, :]
        return carry

    lax.fori_loop(0, nc, emit, 0)


def mlstm_gates(y3, bias_i, bias_f):
    b, s, _ = y3.shape
    nc = s // MLSTM_CHUNK
    col = pl.BlockSpec((1, s, LANES), lambda bi: (bi, 0, 0))
    row = pl.BlockSpec((1, nc, C_STREAMS, MLSTM_CHUNK), lambda bi: (bi, 0, 0, 0))
    vec = pl.BlockSpec((1, LANES), lambda bi: (0, 0))
    return pl.pallas_call(
        functools.partial(_gate_kernel, nc=nc),
        grid=(b,),
        in_specs=[pl.BlockSpec((1, s, LANES), lambda bi: (bi, 0, Y_GI // LANES)),
                  pl.BlockSpec((1, s, LANES), lambda bi: (bi, 0, Y_GF // LANES)), vec, vec],
        out_specs=[col, col, col, row, row, pl.BlockSpec((1, nc, LANES), lambda bi: (bi, 0, 0))],
        out_shape=[jax.ShapeDtypeStruct((b, s, LANES), F32)] * 3
        + [jax.ShapeDtypeStruct((b, nc, C_STREAMS, MLSTM_CHUNK), F32)] * 2
        + [jax.ShapeDtypeStruct((b, nc, LANES), F32)],
        scratch_shapes=[pltpu.VMEM((s, LANES), F32)] * 3 + [pltpu.VMEM((nc, LANES), F32)] * 4,
        compiler_params=_cparams(("parallel",)),
        name="mlstm_gates",
    )(y3, y3, bias_i.reshape(1, LANES), bias_f.reshape(1, LANES))


def _mlstm_kernel(dec_ref, q_ref, kt_ref, v_ref, o_ref, g_ref, cb_ref, inter_ref, e_ref, rt_ref, wkt_ref,
                  y_ref, cbs, inters, es, hf, hb, ct, *, seq, nc):
    lc = MLSTM_CHUNK
    bi = pl.program_id(0)
    pair = pl.program_id(1)
    shift = (LANES - pair * 4) % LANES

    def align(c, carry):
        rows = pl.ds(pl.multiple_of(c * lc, lc), lc)
        cbs[rows, :] = pltpu.roll(cb_ref[0, rows, :], shift, 1)
        inters[rows, :] = pltpu.roll(inter_ref[0, rows, :], shift, 1)
        es[rows, :] = pltpu.roll(e_ref[0, rows, :], shift, 1)
        return carry

    lax.fori_loop(0, nc, align, 0)
    ct[...] = jnp.zeros_like(ct)
    lane = lax.broadcasted_iota(jnp.int32, (lc, LANES), 1)
    trow = lax.broadcasted_iota(jnp.int32, (lc, lc), 0)
    scol = lax.broadcasted_iota(jnp.int32, (lc, lc), 1)
    ones = jnp.ones((lc, C_V_DIM), BF16)

    def step(it, carry):
        for direction in range(2):
            c = it if direction == 0 else nc - 1 - it
            rows = pl.ds(pl.multiple_of(c * lc, lc), lc)
            mask = (scol <= trow) if direction == 0 else (scol >= trow)
            q2 = q_ref[0, rows, :]
            kt2 = kt_ref[0, c]
            for hh in range(2):
                sl = hh * 2 + direction
                stream = pair * 4 + sl
                dec = dec_ref[(bi * nc + c) * C_STREAMS + stream]
                qh = jnp.where((lane >= hh * C_QK_DIM) & (lane < (hh + 1) * C_QK_DIM), q2, jnp.zeros_like(q2))
                vaug = jnp.concatenate([v_ref[0, rows, hh * C_V_DIM:(hh + 1) * C_V_DIM], ones], axis=1)
                sc = jnp.dot(qh, kt2, preferred_element_type=F32)
                arg = cbs[rows, sl:sl + 1] + rt_ref[0, c, pl.ds(stream, 1), :]
                a = jnp.exp(jnp.where(mask, arg, NEG_INF)) * sc
                num_a = jnp.dot(a.astype(BF16), vaug, preferred_element_type=F32)
                ct_pair = ct[direction]
                num_c = jnp.dot(qh, ct_pair.astype(BF16), preferred_element_type=F32)
                num = num_a + inters[rows, sl:sl + 1] * num_c
                den = jnp.maximum(jnp.abs(num[:, C_V_DIM:]), es[rows, sl:sl + 1])
                h = num[:, :C_V_DIM] / den
                dst = hf if direction == 0 else hb
                dst[rows, hh * C_V_DIM:(hh + 1) * C_V_DIM] = h
                kw = (kt2[hh * C_QK_DIM:(hh + 1) * C_QK_DIM, :].astype(F32)
                      * wkt_ref[0, c, pl.ds(stream, 1), :]).astype(BF16)
                u = jnp.dot(kw, vaug, preferred_element_type=F32)
                d_rows = slice(hh * C_QK_DIM, (hh + 1) * C_QK_DIM)
                ct[direction, d_rows, :] = dec * ct[direction, d_rows, :] + u
        return carry

    lax.fori_loop(0, nc, step, 0)

    tile = 512

    def finish(i, carry):
        rows = pl.ds(pl.multiple_of(i * tile, tile), tile)
        for hh in range(2):
            cols = slice(hh * C_V_DIM, (hh + 1) * C_V_DIM)
            h = hf[rows, cols] + hb[rows, cols]
            h = h * lax.rsqrt(jnp.mean(h * h, axis=-1, keepdims=True) + EPS)
            h = h * g_ref[:, cols]
            y_ref[0, rows, cols] = (h * jax.nn.sigmoid(o_ref[0, rows, cols])).astype(y_ref.dtype)
        return carry

    lax.fori_loop(0, seq // tile, finish, 0)


def mlstm_scan(dec, qk, kt, v3, y3, out_g, cb, inter, e, rt, wkt):
    b, s, _ = qk.shape
    nc = s // MLSTM_CHUNK
    pw = 2 * C_V_DIM
    col = pl.BlockSpec((1, s, LANES), lambda bi, p, d: (bi, 0, 0))
    row = pl.BlockSpec((1, nc, C_STREAMS, MLSTM_CHUNK), lambda bi, p, d: (bi, 0, 0, 0))
    grid_spec = pltpu.PrefetchScalarGridSpec(
        num_scalar_prefetch=1,
        grid=(b, C_PAIRS),
        in_specs=[pl.BlockSpec((1, s, LANES), lambda bi, p, d: (bi, 0, p)),
                  pl.BlockSpec((1, nc, LANES, MLSTM_CHUNK), lambda bi, p, d: (bi, 0, p, 0)),
                  pl.BlockSpec((1, s, pw), lambda bi, p, d: (bi, 0, V_CV // pw + p)),
                  pl.BlockSpec((1, s, pw), lambda bi, p, d: (bi, 0, Y_CO // pw + p)),
                  pl.BlockSpec((1, pw), lambda bi, p, d: (0, p)),
                  col, col, col, row, row],
        out_specs=pl.BlockSpec((1, s, pw), lambda bi, p, d: (bi, 0, p)),
        scratch_shapes=[pltpu.VMEM((s, LANES), F32)] * 3 + [pltpu.VMEM((s, pw), F32)] * 2
        + [pltpu.VMEM((2, LANES, 2 * C_V_DIM), F32)],
    )
    return pl.pallas_call(
        functools.partial(_mlstm_kernel, seq=s, nc=nc),
        grid_spec=grid_spec,
        out_shape=jax.ShapeDtypeStruct((b, s, C_HEADS * C_V_DIM), BF16),
        compiler_params=_cparams(("parallel", "arbitrary")),
        name="mlstm_scan",
    )(dec, qk, kt, v3, y3, out_g.reshape(1, C_V_WIDTH), cb, inter, e, rt, wkt)


def _merge_kernel(ya_ref, yb_ref, yc_ref, ga_ref, gb_ref, gc_ref, wa_ref, wb_ref, wc_ref, o_ref):
    acc = jax.nn.sigmoid(ga_ref[...]) * jnp.dot(ya_ref[...], wa_ref[...], preferred_element_type=F32)
    acc = acc + jax.nn.sigmoid(gb_ref[...]) * jnp.dot(yb_ref[...], wb_ref[...], preferred_element_type=F32)
    acc = acc + jax.nn.sigmoid(gc_ref[...]) * jnp.dot(yc_ref[...], wc_ref[...], preferred_element_type=F32)
    o_ref[...] = acc.astype(o_ref.dtype)


def branch_merge(ya, yb, yc, y2, wa, wb, wc, tm=512, tn=1024):
    m = ya.shape[0]
    n = wa.shape[1]
    nj = n // tn

    def act(width):
        return pl.BlockSpec((tm, width), lambda i, j: (i, 0))

    def gate(branch):
        return pl.BlockSpec((tm, tn), lambda i, j: (i, Y_G // tn + branch * nj + j))

    def weight(width):
        return pl.BlockSpec((width, tn), lambda i, j: (0, j))

    return pl.pallas_call(
        _merge_kernel,
        grid=(m // tm, nj),
        in_specs=[act(A_WIDTH), act(B_Q_WIDTH), act(C_V_WIDTH), gate(0), gate(1), gate(2),
                  weight(A_WIDTH), weight(B_Q_WIDTH), weight(C_V_WIDTH)],
        out_specs=pl.BlockSpec((tm, tn), lambda i, j: (i, j)),
        out_shape=jax.ShapeDtypeStruct((m, n), BF16),
        compiler_params=_cparams(("parallel", "parallel")),
        name="branch_merge",
    )(ya, yb, yc, y2, y2, y2, wa, wb, wc)


def _rope_tables(pos, dim):
    inv_freq = ROPE_THETA ** (-jnp.arange(0, dim, 2, dtype=F32) / dim)
    ang = pos.astype(F32)[:, None] * inv_freq[None, :]
    return jnp.cos(ang), jnp.sin(ang)


def _stream_gate_columns(w_gate, b_gate):
    lanes = jnp.arange(C_STREAMS)
    head = (lanes // 4) * 2 + (lanes % 4) // 2
    direction = lanes % 2
    i_cols = direction * 2 * C_HEADS + head
    f_cols = i_cols + C_HEADS
    pad = LANES - C_STREAMS
    wi = jnp.pad(w_gate[:, i_cols], ((0, 0), (0, pad)))
    wf = jnp.pad(w_gate[:, f_cols], ((0, 0), (0, pad)))
    return wi, wf, jnp.pad(b_gate[i_cols], (0, pad)), jnp.pad(b_gate[f_cols], (0, pad))


def _layer(x2, shape, tables, norm1_g, w_in, a_q_gain, a_k_gain, b_q_gain, b_k_gain, c_conv_w, c_conv_b,
           c_gate_b, c_out_g, w_branch_a, w_branch_b, w_branch_c, w_out, norm2_g, w_up, w_down):
    bsz, seq = shape
    cos_a, sin_a, cos_b, sin_b = tables
    sizes = (A_WIDTH, A_WIDTH, A_WIDTH, B_Q_WIDTH, B_KV_WIDTH, B_KV_WIDTH, 2 * C_QK_WIDTH, C_V_WIDTH, C_V_WIDTH,
             4 * C_HEADS, N_BRANCHES * x2.shape[1])
    offs = [0]
    for sz in sizes:
        offs.append(offs[-1] + sz)
    seg = lambda i: w_in[:, offs[i]:offs[i + 1]]
    wi, wf, bias_i, bias_f = _stream_gate_columns(seg(9), c_gate_b)
    w_y = jnp.concatenate([seg(10), seg(6), seg(8), seg(3), seg(0), seg(1), seg(4), wi, wf], axis=1).astype(BF16)
    w_av = seg(2).astype(BF16)
    w_v = jnp.concatenate([seg(5), seg(7)], axis=1).astype(BF16)

    xn = rmsnorm(x2, norm1_g)
    y2 = matmul(xn, w_y, out_dtype=F32)
    av = matmul(xn, w_av, out_dtype=BF16, tn=A_WIDTH)
    v2 = matmul(xn, w_v, out_dtype=BF16, tn=V_COLS)
    y3 = y2.reshape(bsz, seq, Y_COLS)
    v3 = v2.reshape(bsz, seq, V_COLS)

    gains_a = jnp.concatenate([jnp.tile(a_q_gain[None], (A_HEADS, 1)), jnp.tile(a_k_gain[None], (A_HEADS, 1))])
    qk_a = qk_prep(y3, lambda h: Y_AQ // HEAD_DIM + h, gains_a, cos_a, sin_a, half=HEAD_DIM // 2)
    av3 = av.reshape(bsz, seq, A_WIDTH)
    outs, lses = [], []
    for window, dilation in A_PATTERNS:
        assert window // (2 * dilation) == A_REACH_BLOCK
        o, lse = band_attention(qk_a, av3, dilation)
        outs.append(o.reshape(bsz * seq, A_WIDTH))
        lses.append(lse.reshape(bsz * seq, LANES))
    ya = combine_patterns(outs, lses)

    gains_b = jnp.concatenate([jnp.tile(b_q_gain[None], (B_Q_HEADS, 1)), jnp.tile(b_k_gain[None], (B_KV_HEADS, 1))])
    col_b = lambda h: jnp.where(h < B_Q_HEADS, Y_BQ // HEAD_DIM + h, Y_BK // HEAD_DIM + h - B_Q_HEADS)
    qk_b = qk_prep(y3, col_b, gains_b, cos_b, sin_b, half=HEAD_DIM // 4)
    yb = gqa_attention(qk_b, v3).reshape(bsz * seq, B_Q_WIDTH)

    qk_c, kt_c = mlstm_conv(y3, c_conv_w, c_conv_b)
    cb, inter, e, rt, wkt, dec = mlstm_gates(y3, bias_i, bias_f)
    dec_flat = dec[:, :, :C_STREAMS].reshape(-1)
    yc = mlstm_scan(dec_flat, qk_c, kt_c, v3, y3, c_out_g, cb, inter, e, rt, wkt).reshape(bsz * seq, C_V_WIDTH)

    merged = branch_merge(ya, yb, yc, y2, w_branch_a.astype(BF16), w_branch_b.astype(BF16), w_branch_c.astype(BF16))
    h = matmul(merged, w_out.astype(BF16), out_dtype=F32, res=x2)
    hn = rmsnorm(h, norm2_g)
    u2 = matmul(hn, w_up.astype(BF16), out_dtype=BF16, relu2=True)
    return matmul(u2, w_down.astype(BF16), out_dtype=F32, res=h)


def kernel(x, norm1_g, w_in, a_q_gain, a_k_gain, b_q_gain, b_k_gain, c_conv_w, c_conv_b, c_gate_b, c_out_g, w_branch_a, w_branch_b, w_branch_c, w_out, norm2_g, w_up, w_down):
    bsz, seq, d_model = x.shape
    pos = jnp.arange(seq)
    cos1, sin1 = _rope_tables(pos, HEAD_DIM)
    cos_r, sin_r = _rope_tables(pos // GRID_W, HEAD_DIM // 2)
    cos_c, sin_c = _rope_tables(pos % GRID_W, HEAD_DIM // 2)
    tables = (jnp.concatenate([cos1, cos1], axis=1), jnp.concatenate([-sin1, sin1], axis=1),
              jnp.concatenate([cos_r, cos_r, cos_c, cos_c], axis=1),
              jnp.concatenate([-sin_r, sin_r, -sin_c, sin_c], axis=1))
    x2 = x.reshape(bsz * seq, d_model)
    for l in range(norm1_g.shape[0]):
        x2 = _layer(x2, (bsz, seq), tables, norm1_g[l], w_in[l], a_q_gain[l], a_k_gain[l], b_q_gain[l], b_k_gain[l],
                    c_conv_w[l], c_conv_b[l], c_gate_b[l], c_out_g[l], w_branch_a[l], w_branch_b[l], w_branch_c[l],
                    w_out[l], norm2_g[l], w_up[l], w_down[l])
    return x2.reshape(bsz, seq, d_model)
```

```python
import functools
import math

import jax
import jax.numpy as jnp
from jax import lax
from jax.experimental import pallas as pl
from jax.experimental.pallas import tpu as pltpu

F32 = jnp.float32
BF16 = jnp.bfloat16

HEAD_DIM = 128
GRID_W = 64
ROPE_THETA = 10000.0
EPS = 1e-6
NEG_INF = -1e30

A_HEADS = 6
A_PATTERNS = ((128, 1), (512, 4), (2048, 16))
A_REACH_BLOCK = 64
A_GROUP = 4
B_Q_HEADS = 8
B_KV_HEADS = 2
C_HEADS = 8
C_QK_DIM = 64
C_V_DIM = 128
C_CONV = 5
N_BRANCHES = 3

A_WIDTH = A_HEADS * HEAD_DIM
B_Q_WIDTH = B_Q_HEADS * HEAD_DIM
B_KV_WIDTH = B_KV_HEADS * HEAD_DIM
C_QK_WIDTH = C_HEADS * C_QK_DIM
C_V_WIDTH = C_HEADS * C_V_DIM

LANES = 128
MLSTM_CHUNK = 128
C_PAIRS = C_HEADS // 2
C_STREAMS = 2 * C_HEADS
VMEM_LIMIT = 56 * 1024 * 1024

Y_G = 0
Y_CQK = 6144
Y_CO = 7168
Y_BQ = 8192
Y_AQ = 9216
Y_AK = 9984
Y_BK = 10752
Y_GI = 11008
Y_GF = 11136
Y_COLS = 11264
V_AV = 0
V_BV = 768
V_CV = 1024
V_COLS = 2048


def _cparams(sem):
    return pltpu.CompilerParams(dimension_semantics=sem, vmem_limit_bytes=VMEM_LIMIT)


def _rmsnorm_kernel(x_ref, g_ref, o_ref):
    x = x_ref[...]
    r = lax.rsqrt(jnp.mean(x * x, axis=-1, keepdims=True) + EPS)
    o_ref[...] = ((x * r) * g_ref[...]).astype(o_ref.dtype)


def rmsnorm(x, g, tm=512):
    m, d = x.shape
    return pl.pallas_call(
        _rmsnorm_kernel,
        grid=(m // tm,),
        in_specs=[pl.BlockSpec((tm, d), lambda i: (i, 0)), pl.BlockSpec((1, d), lambda i: (0, 0))],
        out_specs=pl.BlockSpec((tm, d), lambda i: (i, 0)),
        out_shape=jax.ShapeDtypeStruct((m, d), BF16),
        compiler_params=_cparams(("parallel",)),
        name="rmsnorm",
    )(x, g.reshape(1, d))


def _matmul_kernel(*refs, nk, relu2, has_res):
    if has_res:
        a_ref, w_ref, r_ref, o_ref, acc_ref = refs
    else:
        a_ref, w_ref, o_ref, acc_ref = refs
        r_ref = None

    def finish(acc):
        if relu2:
            u = jnp.maximum(acc, 0.0)
            acc = u * u
        if has_res:
            acc = r_ref[...] + acc
        o_ref[...] = acc.astype(o_ref.dtype)

    prod = jnp.dot(a_ref[...], w_ref[...], preferred_element_type=F32)
    if nk == 1:
        finish(prod)
    else:
        k = pl.program_id(2)

        @pl.when(k == 0)
        def _():
            acc_ref[...] = prod

        @pl.when(k > 0)
        def _():
            acc_ref[...] += prod

        @pl.when(k == nk - 1)
        def _():
            finish(acc_ref[...])


def matmul(a, w, *, out_dtype, res=None, relu2=False, tm=1024, tn=1024, tk=2048):
    m, kdim = a.shape
    n = w.shape[1]
    tn = min(tn, n)
    tk = min(tk, kdim)
    nk = kdim // tk
    in_specs = [pl.BlockSpec((tm, tk), lambda i, j, k: (i, k)),
                pl.BlockSpec((tk, tn), lambda i, j, k: (k, j))]
    args = [a, w]
    if res is not None:
        in_specs.append(pl.BlockSpec((tm, tn), lambda i, j, k: (i, j)))
        args.append(res)
    return pl.pallas_call(
        functools.partial(_matmul_kernel, nk=nk, relu2=relu2, has_res=res is not None),
        grid=(m // tm, n // tn, nk),
        in_specs=in_specs,
        out_specs=pl.BlockSpec((tm, tn), lambda i, j, k: (i, j)),
        out_shape=jax.ShapeDtypeStruct((m, n), out_dtype),
        scratch_shapes=[pltpu.VMEM((tm, tn) if nk > 1 else (8, LANES), F32)],
        compiler_params=_cparams(("parallel", "parallel", "arbitrary")),
        name="matmul",
    )(*args)


def _norm_rope(x, gain, cos_full, sin_signed, half):
    y = (x * lax.rsqrt(jnp.mean(x * x, axis=-1, keepdims=True) + EPS)) * gain
    if 2 * half == LANES:
        partner = pltpu.roll(y, half, 1)
    else:
        lane = lax.broadcasted_iota(jnp.int32, y.shape, 1)
        partner = jnp.where(lane % (2 * half) < half, pltpu.roll(y, LANES - half, 1), pltpu.roll(y, half, 1))
    return y * cos_full + partner * sin_signed


def _dilated_kernel(q_ref, k_ref, v_ref, cos_ref, sin_ref, qg_ref, kg_ref, y_ref,
                    qn, kn, vn, kd, vd, acc, mrun, zrun, *, seq):
    blk = A_REACH_BLOCK
    grp = A_GROUP
    tile = 512
    scale = HEAD_DIM ** -0.5

    def prep(i, carry):
        rows = pl.ds(pl.multiple_of(i * tile, tile), tile)
        cos = cos_ref[rows, :]
        sin = sin_ref[rows, :]
        qn[rows, :] = _norm_rope(q_ref[0, rows, :], qg_ref[...], cos, sin, HEAD_DIM // 2)
        kn[rows, :] = _norm_rope(k_ref[0, rows, :], kg_ref[...], cos, sin, HEAD_DIM // 2)
        vn[rows, :] = v_ref[0, rows, :].astype(F32)
        return carry

    lax.fori_loop(0, seq // tile, prep, 0)
    zpad = jnp.zeros((blk, HEAD_DIM), BF16)
    kd[0:blk, :] = zpad
    kd[blk + seq:, :] = zpad
    vd[0:blk, :] = zpad
    vd[blk + seq:, :] = zpad

    t_io = lax.broadcasted_iota(jnp.int32, (grp, blk, 3 * blk), 1)
    j_io = lax.broadcasted_iota(jnp.int32, (grp, blk, 3 * blk), 2)
    g_io = lax.broadcasted_iota(jnp.int32, (grp, blk, 3 * blk), 0)
    in_reach = jnp.abs(j_io - blk - t_io) <= blk

    for pi, (_, d) in enumerate(A_PATTERNS):
        sub_len = seq // d
        chunk = 256
        per_res = sub_len // chunk

        def gather(idx, carry, d=d, sub_len=sub_len, per_res=per_res):
            r = idx // per_res
            j = idx % per_res
            if d == 1:
                src = pl.ds(pl.multiple_of(j * chunk, chunk), chunk)
            else:
                src = pl.ds(r + j * chunk * d, chunk, stride=d)
            dst = pl.ds(pl.multiple_of(blk + r * sub_len + j * chunk, blk), chunk)
            kd[dst, :] = kn[src, :].astype(BF16)
            vd[dst, :] = vn[src, :].astype(BF16)
            return carry

        lax.fori_loop(0, seq // chunk, gather, 0)

        its_per_res = sub_len // (blk * grp)

        def band(idx, carry, d=d, sub_len=sub_len, its_per_res=its_per_res, first=(pi == 0)):
            r = idx // its_per_res
            i0 = (idx % its_per_res) * grp
            rows, qs, kws, vws = [], [], [], []
            for g in range(grp):
                i = i0 + g
                if d == 1:
                    rw = pl.ds(pl.multiple_of(i * blk, blk), blk)
                else:
                    rw = pl.ds(r + i * blk * d, blk, stride=d)
                rows.append(rw)
                qs.append(qn[rw, :].astype(BF16))
                win = pl.ds(pl.multiple_of(r * sub_len + i * blk, blk), 3 * blk)
                kws.append(kd[win, :])
                vws.append(vd[win, :])
            q4 = jnp.stack(qs)
            k4 = jnp.stack(kws)
            v4 = jnp.stack(vws)
            s = lax.dot_general(q4, k4, (((2,), (2,)), ((0,), (0,))), preferred_element_type=F32) * scale
            kpos = (i0 + g_io) * blk - blk + j_io
            valid = in_reach & (kpos >= 0) & (kpos < sub_len)
            s = jnp.where(valid, s, NEG_INF)
            m = jnp.max(s, axis=-1, keepdims=True)
            p = jnp.exp(s - m)
            z = jnp.sum(p, axis=-1, keepdims=True)
            o = lax.dot_general(p.astype(BF16), v4, (((2,), (1,)), ((0,), (0,))), preferred_element_type=F32)
            for g in range(grp):
                rw = rows[g]
                mb = jnp.broadcast_to(m[g], (blk, HEAD_DIM))
                zb = jnp.broadcast_to(z[g], (blk, HEAD_DIM))
                if first:
                    acc[rw, :] = o[g]
                    mrun[rw, :] = mb
                    zrun[rw, :] = zb
                else:
                    m_old = mrun[rw, :]
                    m_new = jnp.maximum(m_old, mb)
                    w_old = jnp.exp(m_old - m_new)
                    w_blk = jnp.exp(mb - m_new)
                    acc[rw, :] = w_old * acc[rw, :] + w_blk * o[g]
                    zrun[rw, :] = w_old * zrun[rw, :] + w_blk * zb
                    mrun[rw, :] = m_new
            return carry

        lax.fori_loop(0, d * its_per_res, band, 0)

    def finish(i, carry):
        rows = pl.ds(pl.multiple_of(i * tile, tile), tile)
        y_ref[0, rows, :] = (acc[rows, :] / zrun[rows, :]).astype(y_ref.dtype)
        return carry

    lax.fori_loop(0, seq // tile, finish, 0)


def dilated_mixer(y3, v3, cos_full, sin_signed, q_gain, k_gain):
    b, s, _ = y3.shape
    assert all(w // (2 * d) == A_REACH_BLOCK for w, d in A_PATTERNS)
    assert all((s // d) % (A_REACH_BLOCK * A_GROUP) == 0 for _, d in A_PATTERNS)
    head = lambda off: pl.BlockSpec((1, s, HEAD_DIM), lambda bi, h: (bi, 0, off // HEAD_DIM + h))
    table = pl.BlockSpec((s, HEAD_DIM), lambda bi, h: (0, 0))
    vec = pl.BlockSpec((1, HEAD_DIM), lambda bi, h: (0, 0))
    pad_rows = s + 2 * A_REACH_BLOCK
    return pl.pallas_call(
        functools.partial(_dilated_kernel, seq=s),
        grid=(b, A_HEADS),
        in_specs=[head(Y_AQ), head(Y_AK), head(V_AV), table, table, vec, vec],
        out_specs=pl.BlockSpec((1, s, HEAD_DIM), lambda bi, h: (bi, 0, h)),
        out_shape=jax.ShapeDtypeStruct((b, s, A_WIDTH), BF16),
        scratch_shapes=[pltpu.VMEM((s, HEAD_DIM), F32)] * 3 + [pltpu.VMEM((pad_rows, HEAD_DIM), BF16)] * 2
        + [pltpu.VMEM((s, HEAD_DIM), F32)] * 3,
        compiler_params=_cparams(("parallel", "parallel")),
        name="dilated_mixer",
    )(y3, y3, v3, cos_full, sin_signed, q_gain.reshape(1, HEAD_DIM), k_gain.reshape(1, HEAD_DIM))


def _gqa_kernel(q_ref, k_ref, v_ref, cosq_ref, sinq_ref, cosk_ref, sink_ref, qg_ref, kg_ref, o_ref, kn, *, group, seq):
    half = HEAD_DIM // 4
    tile = 512

    @pl.when(pl.program_id(2) == 0)
    def _():
        def prep(i, carry):
            rows = pl.ds(pl.multiple_of(i * tile, tile), tile)
            kn[rows, :] = _norm_rope(k_ref[0, rows, :], kg_ref[...], cosk_ref[rows, :], sink_ref[rows, :],
                                     half).astype(kn.dtype)
            return carry

        lax.fori_loop(0, seq // tile, prep, 0)

    k = kn[...]
    v = v_ref[0]
    c = (HEAD_DIM ** -0.5) * math.log2(math.e)
    for g in range(group):
        cols = slice(g * HEAD_DIM, (g + 1) * HEAD_DIM)
        q = _norm_rope(q_ref[0, :, cols], qg_ref[...], cosq_ref[...], sinq_ref[...], half).astype(BF16)
        s = lax.dot_general(q, k, (((1,), (1,)), ((), ())), preferred_element_type=F32)
        m = jnp.max(s, axis=-1, keepdims=True)
        p = jnp.exp2((s - m) * c)
        l = jnp.sum(p, axis=-1, keepdims=True)
        o = jnp.dot(p.astype(BF16), v, preferred_element_type=F32) / l
        o_ref[0, :, cols] = o.astype(o_ref.dtype)


def gqa_attention(y3, v3, cos_full, sin_signed, q_gain, k_gain, tq=256):
    b, s, _ = y3.shape
    group = B_Q_HEADS // B_KV_HEADS
    gw = group * HEAD_DIM
    vec = pl.BlockSpec((1, HEAD_DIM), lambda bi, h, t: (0, 0))
    qtab = pl.BlockSpec((tq, HEAD_DIM), lambda bi, h, t: (t, 0))
    ktab = pl.BlockSpec((s, HEAD_DIM), lambda bi, h, t: (0, 0))
    return pl.pallas_call(
        functools.partial(_gqa_kernel, group=group, seq=s),
        grid=(b, B_KV_HEADS, s // tq),
        in_specs=[pl.BlockSpec((1, tq, gw), lambda bi, h, t: (bi, t, Y_BQ // gw + h)),
                  pl.BlockSpec((1, s, HEAD_DIM), lambda bi, h, t: (bi, 0, Y_BK // HEAD_DIM + h)),
                  pl.BlockSpec((1, s, HEAD_DIM), lambda bi, h, t: (bi, 0, V_BV // HEAD_DIM + h)),
                  qtab, qtab, ktab, ktab, vec, vec],
        out_specs=pl.BlockSpec((1, tq, gw), lambda bi, h, t: (bi, t, h)),
        out_shape=jax.ShapeDtypeStruct((b, s, B_Q_WIDTH), BF16),
        scratch_shapes=[pltpu.VMEM((s, HEAD_DIM), BF16)],
        compiler_params=_cparams(("parallel", "parallel", "arbitrary")),
        name="gqa_attention",
    )(y3, y3, v3, cos_full, sin_signed, cos_full, sin_signed, q_gain.reshape(1, HEAD_DIM), k_gain.reshape(1, HEAD_DIM))


def _conv_kernel(xp_ref, xm_ref, xn_ref, w_ref, b_ref, qk_ref, kt_ref, buf, *, tile, ntiles):
    ti = pl.program_id(1)
    halo = 8
    buf[0:halo] = jnp.where(ti > 0, xp_ref[0], 0.0)
    buf[halo:halo + tile] = xm_ref[0]
    buf[halo + tile:] = jnp.where(ti < ntiles - 1, xn_ref[0], 0.0)
    pad = C_CONV // 2
    acc = w_ref[0:1, :] * buf[halo - pad:halo - pad + tile, :]
    for j in range(1, C_CONV):
        acc = acc + w_ref[j:j + 1, :] * buf[halo - pad + j:halo - pad + j + tile, :]
    acc = acc + b_ref[...]
    act = acc * jax.nn.sigmoid(acc)
    q = act[:, :C_QK_WIDTH]
    k = act[:, C_QK_WIDTH:] * (C_QK_DIM ** -0.5)
    qk_ref[0, :, :C_QK_WIDTH] = q.astype(qk_ref.dtype)
    qk_ref[0, :, C_QK_WIDTH:] = k.astype(qk_ref.dtype)
    for c in range(tile // MLSTM_CHUNK):
        kt_ref[0, c] = k[c * MLSTM_CHUNK:(c + 1) * MLSTM_CHUNK, :].T.astype(kt_ref.dtype)


def mlstm_conv(y3, conv_w, conv_b, tile=512):
    b, s, _ = y3.shape
    ntiles = s // tile
    w2 = 2 * C_QK_WIDTH
    cb = Y_CQK // w2
    per = tile // 8
    return pl.pallas_call(
        functools.partial(_conv_kernel, tile=tile, ntiles=ntiles),
        grid=(b, ntiles),
        in_specs=[pl.BlockSpec((1, 8, w2), lambda bi, t: (bi, jnp.maximum(t * per - 1, 0), cb)),
                  pl.BlockSpec((1, tile, w2), lambda bi, t: (bi, t, cb)),
                  pl.BlockSpec((1, 8, w2), lambda bi, t: (bi, jnp.minimum((t + 1) * per, s // 8 - 1), cb)),
                  pl.BlockSpec((C_CONV, w2), lambda bi, t: (0, 0)),
                  pl.BlockSpec((1, w2), lambda bi, t: (0, 0))],
        out_specs=[pl.BlockSpec((1, tile, w2), lambda bi, t: (bi, t, 0)),
                   pl.BlockSpec((1, tile // MLSTM_CHUNK, C_QK_WIDTH, MLSTM_CHUNK), lambda bi, t: (bi, t, 0, 0))],
        out_shape=[jax.ShapeDtypeStruct((b, s, w2), BF16),
                   jax.ShapeDtypeStruct((b, s // MLSTM_CHUNK, C_QK_WIDTH, MLSTM_CHUNK), BF16)],
        scratch_shapes=[pltpu.VMEM((tile + 16, w2), F32)],
        compiler_params=_cparams(("parallel", "parallel")),
        name="mlstm_conv",
    )(y3, y3, y3, conv_w, conv_b.reshape(1, w2))


def _gate_kernel(gi_ref, gf_ref, bi_ref, bf_ref, cb_ref, inter_ref, e_ref, rt_ref, wkt_ref, dec_ref,
                 b_s, r_s, dmax_s, btot_s, gmax_s, mpf_s, mpb_s, *, nc):
    lc = MLSTM_CHUNK
    t_in = lax.broadcasted_iota(jnp.int32, (lc, LANES), 0)
    fwd = lax.broadcasted_iota(jnp.int32, (lc, LANES), 1) % 2 == 0
    fwd_row = fwd[0:1]
    steps = [1 << i for i in range(lc.bit_length() - 1)]

    def chunk_rows(c):
        return pl.ds(pl.multiple_of(c * lc, lc), lc)

    def local(c, carry):
        rows = chunk_rows(c)
        ii = gi_ref[0, rows, :] + bi_ref[...]
        ff = gf_ref[0, rows, :] + bf_ref[...]
        lf = jnp.minimum(ff, 0.0) - jnp.log1p(jnp.exp(-jnp.abs(ff)))
        cs = lf
        rs = lf
        for k in steps:
            cs = cs + jnp.where(t_in >= k, pltpu.roll(cs, k, 0), 0.0)
            rs = rs + jnp.where(t_in + k < lc, pltpu.roll(rs, lc - k, 0), 0.0)
        b = jnp.where(fwd, cs, rs)
        btot = cs + rs - lf
        r = ii - b
        pm = r
        sm = r
        for k in steps:
            pm = jnp.where(t_in >= k, jnp.maximum(pm, pltpu.roll(pm, k, 0)), pm)
            sm = jnp.where(t_in + k < lc, jnp.maximum(sm, pltpu.roll(sm, lc - k, 0)), sm)
        b_s[rows, :] = b
        r_s[rows, :] = r
        dmax_s[rows, :] = b + jnp.where(fwd, pm, sm)
        btot_s[pl.ds(c, 1), :] = btot[0:1]
        gmax_s[pl.ds(c, 1), :] = (btot + jnp.maximum(pm, sm))[0:1]
        return carry

    lax.fori_loop(0, nc, local, 0)

    def scan(k, carry):
        mf, mb = carry
        mpf_s[pl.ds(k, 1), :] = mf
        mf = jnp.maximum(btot_s[pl.ds(k, 1), :] + mf, gmax_s[pl.ds(k, 1), :])
        kb = nc - 1 - k
        mpb_s[pl.ds(kb, 1), :] = mb
        mb = jnp.maximum(btot_s[pl.ds(kb, 1), :] + mb, gmax_s[pl.ds(kb, 1), :])
        return mf, mb

    zero = jnp.zeros((1, LANES), F32)
    lax.fori_loop(0, nc, scan, (zero, zero))

    def emit(c, carry):
        rows = chunk_rows(c)
        b = b_s[rows, :]
        r = r_s[rows, :]
        btot = btot_s[pl.ds(c, 1), :]
        mprev = jnp.where(fwd_row, mpf_s[pl.ds(c, 1), :], mpb_s[pl.ds(c, 1), :])
        mnew = jnp.maximum(btot + mprev, gmax_s[pl.ds(c, 1), :])
        dec_ref[0, pl.ds(c, 1), :] = jnp.exp(btot + mprev - mnew)
        m_inter = b + mprev
        m_t = jnp.maximum(m_inter, dmax_s[rows, :])
        cb_ref[0, rows, :] = b - m_t
        inter_ref[0, rows, :] = jnp.exp(m_inter - m_t)
        e_ref[0, rows, :] = jnp.exp(-m_t)
        wk = jnp.exp(btot + r - mnew)
        rt_ref[0, c] = r.T[:C_STREAMS, :]
        wkt_ref[0, c] = wk.T[:C_STREAMS, :]
        return carry

    lax.fori_loop(0, nc, emit, 0)


def mlstm_gates(y3, bias_i, bias_f):
    b, s, _ = y3.shape
    nc = s // MLSTM_CHUNK
    col = pl.BlockSpec((1, s, LANES), lambda bi: (bi, 0, 0))
    row = pl.BlockSpec((1, nc, C_STREAMS, MLSTM_CHUNK), lambda bi: (bi, 0, 0, 0))
    vec = pl.BlockSpec((1, LANES), lambda bi: (0, 0))
    return pl.pallas_call(
        functools.partial(_gate_kernel, nc=nc),
        grid=(b,),
        in_specs=[pl.BlockSpec((1, s, LANES), lambda bi: (bi, 0, Y_GI // LANES)),
                  pl.BlockSpec((1, s, LANES), lambda bi: (bi, 0, Y_GF // LANES)), vec, vec],
        out_specs=[col, col, col, row, row, pl.BlockSpec((1, nc, LANES), lambda bi: (bi, 0, 0))],
        out_shape=[jax.ShapeDtypeStruct((b, s, LANES), F32)] * 3
        + [jax.ShapeDtypeStruct((b, nc, C_STREAMS, MLSTM_CHUNK), F32)] * 2
        + [jax.ShapeDtypeStruct((b, nc, LANES), F32)],
        scratch_shapes=[pltpu.VMEM((s, LANES), F32)] * 3 + [pltpu.VMEM((nc, LANES), F32)] * 4,
        compiler_params=_cparams(("parallel",)),
        name="mlstm_gates",
    )(y3, y3, bias_i.reshape(1, LANES), bias_f.reshape(1, LANES))


def _mlstm_kernel(dec_ref, q_ref, kt_ref, v_ref, o_ref, g_ref, cb_ref, inter_ref, e_ref, rt_ref, wkt_ref,
                  y_ref, cbs, inters, es, hf, hb, ct, *, seq, nc):
    lc = MLSTM_CHUNK
    bi = pl.program_id(0)
    pair = pl.program_id(1)
    shift = (LANES - pair * 4) % LANES

    def align(c, carry):
        rows = pl.ds(pl.multiple_of(c * lc, lc), lc)
        cbs[rows, :] = pltpu.roll(cb_ref[0, rows, :], shift, 1)
        inters[rows, :] = pltpu.roll(inter_ref[0, rows, :], shift, 1)
        es[rows, :] = pltpu.roll(e_ref[0, rows, :], shift, 1)
        return carry

    lax.fori_loop(0, nc, align, 0)
    ct[...] = jnp.zeros_like(ct)
    lane = lax.broadcasted_iota(jnp.int32, (lc, LANES), 1)
    trow = lax.broadcasted_iota(jnp.int32, (lc, lc), 0)
    scol = lax.broadcasted_iota(jnp.int32, (lc, lc), 1)
    ones = jnp.ones((lc, C_V_DIM), BF16)

    def step(it, carry):
        for direction in range(2):
            c = it if direction == 0 else nc - 1 - it
            rows = pl.ds(pl.multiple_of(c * lc, lc), lc)
            mask = (scol <= trow) if direction == 0 else (scol >= trow)
            q2 = q_ref[0, rows, :]
            kt2 = kt_ref[0, c]
            for hh in range(2):
                sl = hh * 2 + direction
                stream = pair * 4 + sl
                dec = dec_ref[(bi * nc + c) * C_STREAMS + stream]
                qh = jnp.where((lane >= hh * C_QK_DIM) & (lane < (hh + 1) * C_QK_DIM), q2, jnp.zeros_like(q2))
                vaug = jnp.concatenate([v_ref[0, rows, hh * C_V_DIM:(hh + 1) * C_V_DIM], ones], axis=1)
                sc = jnp.dot(qh, kt2, preferred_element_type=F32)
                arg = cbs[rows, sl:sl + 1] + rt_ref[0, c, pl.ds(stream, 1), :]
                a = jnp.exp(jnp.where(mask, arg, NEG_INF)) * sc
                num_a = jnp.dot(a.astype(BF16), vaug, preferred_element_type=F32)
                ct_pair = ct[direction]
                num_c = jnp.dot(qh, ct_pair.astype(BF16), preferred_element_type=F32)
                num = num_a + inters[rows, sl:sl + 1] * num_c
                den = jnp.maximum(jnp.abs(num[:, C_V_DIM:]), es[rows, sl:sl + 1])
                h = num[:, :C_V_DIM] / den
                dst = hf if direction == 0 else hb
                dst[rows, hh * C_V_DIM:(hh + 1) * C_V_DIM] = h
                kw = (kt2[hh * C_QK_DIM:(hh + 1) * C_QK_DIM, :].astype(F32)
                      * wkt_ref[0, c, pl.ds(stream, 1), :]).astype(BF16)
                u = jnp.dot(kw, vaug, preferred_element_type=F32)
                d_rows = slice(hh * C_QK_DIM, (hh + 1) * C_QK_DIM)
                ct[direction, d_rows, :] = dec * ct[direction, d_rows, :] + u
        return carry

    lax.fori_loop(0, nc, step, 0)

    tile = 512

    def finish(i, carry):
        rows = pl.ds(pl.multiple_of(i * tile, tile), tile)
        for hh in range(2):
            cols = slice(hh * C_V_DIM, (hh + 1) * C_V_DIM)
            h = hf[rows, cols] + hb[rows, cols]
            h = h * lax.rsqrt(jnp.mean(h * h, axis=-1, keepdims=True) + EPS)
            h = h * g_ref[:, cols]
            y_ref[0, rows, cols] = (h * jax.nn.sigmoid(o_ref[0, rows, cols])).astype(y_ref.dtype)
        return carry

    lax.fori_loop(0, seq // tile, finish, 0)


def mlstm_scan(dec, qk, kt, v3, y3, out_g, cb, inter, e, rt, wkt):
    b, s, _ = qk.shape
    nc = s // MLSTM_CHUNK
    pw = 2 * C_V_DIM
    col = pl.BlockSpec((1, s, LANES), lambda bi, p, d: (bi, 0, 0))
    row = pl.BlockSpec((1, nc, C_STREAMS, MLSTM_CHUNK), lambda bi, p, d: (bi, 0, 0, 0))
    grid_spec = pltpu.PrefetchScalarGridSpec(
        num_scalar_prefetch=1,
        grid=(b, C_PAIRS),
        in_specs=[pl.BlockSpec((1, s, LANES), lambda bi, p, d: (bi, 0, p)),
                  pl.BlockSpec((1, nc, LANES, MLSTM_CHUNK), lambda bi, p, d: (bi, 0, p, 0)),
                  pl.BlockSpec((1, s, pw), lambda bi, p, d: (bi, 0, V_CV // pw + p)),
                  pl.BlockSpec((1, s, pw), lambda bi, p, d: (bi, 0, Y_CO // pw + p)),
                  pl.BlockSpec((1, pw), lambda bi, p, d: (0, p)),
                  col, col, col, row, row],
        out_specs=pl.BlockSpec((1, s, pw), lambda bi, p, d: (bi, 0, p)),
        scratch_shapes=[pltpu.VMEM((s, LANES), F32)] * 3 + [pltpu.VMEM((s, pw), F32)] * 2
        + [pltpu.VMEM((2, LANES, 2 * C_V_DIM), F32)],
    )
    return pl.pallas_call(
        functools.partial(_mlstm_kernel, seq=s, nc=nc),
        grid_spec=grid_spec,
        out_shape=jax.ShapeDtypeStruct((b, s, C_HEADS * C_V_DIM), BF16),
        compiler_params=_cparams(("parallel", "arbitrary")),
        name="mlstm_scan",
    )(dec, qk, kt, v3, y3, out_g.reshape(1, C_V_WIDTH), cb, inter, e, rt, wkt)


def _merge_kernel(ya_ref, yb_ref, yc_ref, ga_ref, gb_ref, gc_ref, wa_ref, wb_ref, wc_ref, o_ref):
    acc = jax.nn.sigmoid(ga_ref[...]) * jnp.dot(ya_ref[...], wa_ref[...], preferred_element_type=F32)
    acc = acc + jax.nn.sigmoid(gb_ref[...]) * jnp.dot(yb_ref[...], wb_ref[...], preferred_element_type=F32)
    acc = acc + jax.nn.sigmoid(gc_ref[...]) * jnp.dot(yc_ref[...], wc_ref[...], preferred_element_type=F32)
    o_ref[...] = acc.astype(o_ref.dtype)


def branch_merge(ya, yb, yc, y2, wa, wb, wc, tm=512, tn=1024):
    m = ya.shape[0]
    n = wa.shape[1]
    nj = n // tn

    def act(width):
        return pl.BlockSpec((tm, width), lambda i, j: (i, 0))

    def gate(branch):
        return pl.BlockSpec((tm, tn), lambda i, j: (i, Y_G // tn + branch * nj + j))

    def weight(width):
        return pl.BlockSpec((width, tn), lambda i, j: (0, j))

    return pl.pallas_call(
        _merge_kernel,
        grid=(m // tm, nj),
        in_specs=[act(A_WIDTH), act(B_Q_WIDTH), act(C_V_WIDTH), gate(0), gate(1), gate(2),
                  weight(A_WIDTH), weight(B_Q_WIDTH), weight(C_V_WIDTH)],
        out_specs=pl.BlockSpec((tm, tn), lambda i, j: (i, j)),
        out_shape=jax.ShapeDtypeStruct((m, n), BF16),
        compiler_params=_cparams(("parallel", "parallel")),
        name="branch_merge",
    )(ya, yb, yc, y2, y2, y2, wa, wb, wc)


def _rope_tables(pos, dim):
    inv_freq = ROPE_THETA ** (-jnp.arange(0, dim, 2, dtype=F32) / dim)
    ang = pos.astype(F32)[:, None] * inv_freq[None, :]
    return jnp.cos(ang), jnp.sin(ang)


def _stream_gate_columns(w_gate, b_gate):
    lanes = jnp.arange(C_STREAMS)
    head = (lanes // 4) * 2 + (lanes % 4) // 2
    direction = lanes % 2
    i_cols = direction * 2 * C_HEADS + head
    f_cols = i_cols + C_HEADS
    pad = LANES - C_STREAMS
    wi = jnp.pad(w_gate[:, i_cols], ((0, 0), (0, pad)))
    wf = jnp.pad(w_gate[:, f_cols], ((0, 0), (0, pad)))
    return wi, wf, jnp.pad(b_gate[i_cols], (0, pad)), jnp.pad(b_gate[f_cols], (0, pad))


def _layer(x2, shape, tables, norm1_g, w_in, a_q_gain, a_k_gain, b_q_gain, b_k_gain, c_conv_w, c_conv_b,
           c_gate_b, c_out_g, w_branch_a, w_branch_b, w_branch_c, w_out, norm2_g, w_up, w_down):
    bsz, seq = shape
    cos_a, sin_a, cos_b, sin_b = tables
    sizes = (A_WIDTH, A_WIDTH, A_WIDTH, B_Q_WIDTH, B_KV_WIDTH, B_KV_WIDTH, 2 * C_QK_WIDTH, C_V_WIDTH, C_V_WIDTH,
             4 * C_HEADS, N_BRANCHES * x2.shape[1])
    offs = [0]
    for sz in sizes:
        offs.append(offs[-1] + sz)
    seg = lambda i: w_in[:, offs[i]:offs[i + 1]]
    wi, wf, bias_i, bias_f = _stream_gate_columns(seg(9), c_gate_b)
    w_y = jnp.concatenate([seg(10), seg(6), seg(8), seg(3), seg(0), seg(1), seg(4), wi, wf], axis=1).astype(BF16)
    w_v = jnp.concatenate([seg(2), seg(5), seg(7)], axis=1).astype(BF16)

    xn = rmsnorm(x2, norm1_g)
    y2 = matmul(xn, w_y, out_dtype=F32)
    v2 = matmul(xn, w_v, out_dtype=BF16)
    y3 = y2.reshape(bsz, seq, Y_COLS)
    v3 = v2.reshape(bsz, seq, V_COLS)

    ya = dilated_mixer(y3, v3, cos_a, sin_a, a_q_gain, a_k_gain).reshape(bsz * seq, A_WIDTH)
    yb = gqa_attention(y3, v3, cos_b, sin_b, b_q_gain, b_k_gain).reshape(bsz * seq, B_Q_WIDTH)

    qk_c, kt_c = mlstm_conv(y3, c_conv_w, c_conv_b)
    cb, inter, e, rt, wkt, dec = mlstm_gates(y3, bias_i, bias_f)
    dec_flat = dec[:, :, :C_STREAMS].reshape(-1)
    yc = mlstm_scan(dec_flat, qk_c, kt_c, v3, y3, c_out_g, cb, inter, e, rt, wkt).reshape(bsz * seq, C_V_WIDTH)

    merged = branch_merge(ya, yb, yc, y2, w_branch_a.astype(BF16), w_branch_b.astype(BF16), w_branch_c.astype(BF16))
    h = matmul(merged, w_out.astype(BF16), out_dtype=F32, res=x2)
    hn = rmsnorm(h, norm2_g)
    u2 = matmul(hn, w_up.astype(BF16), out_dtype=BF16, relu2=True)
    return matmul(u2, w_down.astype(BF16), out_dtype=F32, res=h)


def kernel(x, norm1_g, w_in, a_q_gain, a_k_gain, b_q_gain, b_k_gain, c_conv_w, c_conv_b, c_gate_b, c_out_g, w_branch_a, w_branch_b, w_branch_c, w_out, norm2_g, w_up, w_down):
    bsz, seq, d_model = x.shape
    pos = jnp.arange(seq)
    cos1, sin1 = _rope_tables(pos, HEAD_DIM)
    cos_r, sin_r = _rope_tables(pos // GRID_W, HEAD_DIM // 2)
    cos_c, sin_c = _rope_tables(pos % GRID_W, HEAD_DIM // 2)
    tables = (jnp.concatenate([cos1, cos1], axis=1), jnp.concatenate([-sin1, sin1], axis=1),
              jnp.concatenate([cos_r, cos_r, cos_c, cos_c], axis=1),
              jnp.concatenate([-sin_r, sin_r, -sin_c, sin_c], axis=1))
    x2 = x.reshape(bsz * seq, d_model)
    for l in range(norm1_g.shape[0]):
        x2 = _layer(x2, (bsz, seq), tables, norm1_g[l], w_in[l], a_q_gain[l], a_k_gain[l], b_q_gain[l], b_k_gain[l],
                    c_conv_w[l], c_conv_b[l], c_gate_b[l], c_out_g[l], w_branch_a[l], w_branch_b[l], w_branch_c[l],
                    w_out[l], norm2_g[l], w_up[l], w_down[l])
    return x2.reshape(bsz, seq, d_model)
```

```python
import functools
import math

import jax
import jax.numpy as jnp
from jax import lax
from jax.experimental import pallas as pl
from jax.experimental.pallas import tpu as pltpu

F32 = jnp.float32
BF16 = jnp.bfloat16

HEAD_DIM = 128
GRID_W = 64
ROPE_THETA = 10000.0
EPS = 1e-6
NEG_INF = -1e30

A_HEADS = 6
A_PATTERNS = ((128, 1), (512, 4), (2048, 16))
A_REACH_BLOCK = 64
A_QUERY_BLOCK = 128
A_GROUP = 4
A_GROUP_MATMUL = 8
B_Q_HEADS = 8
B_KV_HEADS = 2
C_HEADS = 8
C_QK_DIM = 64
C_V_DIM = 128
C_CONV = 5
N_BRANCHES = 3

A_WIDTH = A_HEADS * HEAD_DIM
B_Q_WIDTH = B_Q_HEADS * HEAD_DIM
B_KV_WIDTH = B_KV_HEADS * HEAD_DIM
C_QK_WIDTH = C_HEADS * C_QK_DIM
C_V_WIDTH = C_HEADS * C_V_DIM

LANES = 128
MLSTM_CHUNK = 128
C_PAIRS = C_HEADS // 2
C_STREAMS = 2 * C_HEADS
VMEM_LIMIT = 56 * 1024 * 1024

Y_G = 0
Y_CQK = 6144
Y_CO = 7168
Y_BQ = 8192
Y_AQ = 9216
Y_AK = 9984
Y_BK = 10752
Y_GI = 11008
Y_GF = 11136
Y_COLS = 11264
V_AV = 0
V_BV = 768
V_CV = 1024
V_COLS = 2048


def _cparams(sem):
    return pltpu.CompilerParams(dimension_semantics=sem, vmem_limit_bytes=VMEM_LIMIT)


def _rmsnorm_kernel(x_ref, g_ref, o_ref):
    x = x_ref[...]
    r = lax.rsqrt(jnp.mean(x * x, axis=-1, keepdims=True) + EPS)
    o_ref[...] = ((x * r) * g_ref[...]).astype(o_ref.dtype)


def rmsnorm(x, g, tm=512):
    m, d = x.shape
    return pl.pallas_call(
        _rmsnorm_kernel,
        grid=(m // tm,),
        in_specs=[pl.BlockSpec((tm, d), lambda i: (i, 0)), pl.BlockSpec((1, d), lambda i: (0, 0))],
        out_specs=pl.BlockSpec((tm, d), lambda i: (i, 0)),
        out_shape=jax.ShapeDtypeStruct((m, d), BF16),
        compiler_params=_cparams(("parallel",)),
        name="rmsnorm",
    )(x, g.reshape(1, d))


def _matmul_kernel(*refs, nk, relu2, has_res):
    if has_res:
        a_ref, w_ref, r_ref, o_ref, acc_ref = refs
    else:
        a_ref, w_ref, o_ref, acc_ref = refs
        r_ref = None

    def finish(acc):
        if relu2:
            u = jnp.maximum(acc, 0.0)
            acc = u * u
        if has_res:
            acc = r_ref[...] + acc
        o_ref[...] = acc.astype(o_ref.dtype)

    prod = jnp.dot(a_ref[...], w_ref[...], preferred_element_type=F32)
    if nk == 1:
        finish(prod)
    else:
        k = pl.program_id(2)

        @pl.when(k == 0)
        def _():
            acc_ref[...] = prod

        @pl.when(k > 0)
        def _():
            acc_ref[...] += prod

        @pl.when(k == nk - 1)
        def _():
            finish(acc_ref[...])


def matmul(a, w, *, out_dtype, res=None, relu2=False, tm=1024, tn=1024, tk=2048):
    m, kdim = a.shape
    n = w.shape[1]
    tn = min(tn, n)
    tk = min(tk, kdim)
    nk = kdim // tk
    in_specs = [pl.BlockSpec((tm, tk), lambda i, j, k: (i, k)),
                pl.BlockSpec((tk, tn), lambda i, j, k: (k, j))]
    args = [a, w]
    if res is not None:
        in_specs.append(pl.BlockSpec((tm, tn), lambda i, j, k: (i, j)))
        args.append(res)
    return pl.pallas_call(
        functools.partial(_matmul_kernel, nk=nk, relu2=relu2, has_res=res is not None),
        grid=(m // tm, n // tn, nk),
        in_specs=in_specs,
        out_specs=pl.BlockSpec((tm, tn), lambda i, j, k: (i, j)),
        out_shape=jax.ShapeDtypeStruct((m, n), out_dtype),
        scratch_shapes=[pltpu.VMEM((tm, tn) if nk > 1 else (8, LANES), F32)],
        compiler_params=_cparams(("parallel", "parallel", "arbitrary")),
        name="matmul",
    )(*args)


def _norm_rope(x, gain, cos_full, sin_signed, half):
    y = (x * lax.rsqrt(jnp.mean(x * x, axis=-1, keepdims=True) + EPS)) * gain
    if 2 * half == LANES:
        partner = pltpu.roll(y, half, 1)
    else:
        lane = lax.broadcasted_iota(jnp.int32, y.shape, 1)
        partner = jnp.where(lane % (2 * half) < half, pltpu.roll(y, LANES - half, 1), pltpu.roll(y, half, 1))
    return y * cos_full + partner * sin_signed


def _dilated_kernel(q_ref, k_ref, v_ref, cos_ref, sin_ref, qg_ref, kg_ref, y_ref,
                    qn, kn, vn, kd, vd, acc, mrun, zrun, sbuf, pbuf, mblk, zblk, bias, *, seq):
    blk = A_REACH_BLOCK
    qb = A_QUERY_BLOCK
    kw = qb + 2 * blk
    grp = A_GROUP
    grp_mm = A_GROUP_MATMUL
    tile = 512
    scale = HEAD_DIM ** -0.5

    def prep(i, carry):
        rows = pl.ds(pl.multiple_of(i * tile, tile), tile)
        cos = cos_ref[rows, :]
        sin = sin_ref[rows, :]
        qn[rows, :] = _norm_rope(q_ref[0, rows, :], qg_ref[...], cos, sin, HEAD_DIM // 2)
        kn[rows, :] = _norm_rope(k_ref[0, rows, :], kg_ref[...], cos, sin, HEAD_DIM // 2)
        vn[rows, :] = v_ref[0, rows, :].astype(F32)
        return carry

    lax.fori_loop(0, seq // tile, prep, 0)
    zpad = jnp.zeros((blk, HEAD_DIM), BF16)
    kd[0:blk, :] = zpad
    kd[blk + seq:, :] = zpad
    vd[0:blk, :] = zpad
    vd[blk + seq:, :] = zpad

    t_io = lax.broadcasted_iota(jnp.int32, (qb, kw), 0)
    j_io = lax.broadcasted_iota(jnp.int32, (qb, kw), 1)
    in_reach = jnp.abs(j_io - blk - t_io) <= blk
    bias[0] = jnp.where(in_reach, 0.0, NEG_INF)
    bias[1] = jnp.where(in_reach & (j_io >= blk), 0.0, NEG_INF)
    bias[2] = jnp.where(in_reach & (j_io < qb + blk), 0.0, NEG_INF)
    nblk = seq // qb

    for pi, (_, d) in enumerate(A_PATTERNS):
        sub_len = seq // d
        chunk = 256
        per_res = sub_len // chunk
        blk_per_res = sub_len // qb
        first = pi == 0

        def gather(idx, carry, d=d, sub_len=sub_len, per_res=per_res):
            r = idx // per_res
            j = idx % per_res
            if d == 1:
                src = pl.ds(pl.multiple_of(j * chunk, chunk), chunk)
            else:
                src = pl.ds(r + j * chunk * d, chunk, stride=d)
            dst = pl.ds(pl.multiple_of(blk + r * sub_len + j * chunk, blk), chunk)
            kd[dst, :] = kn[src, :].astype(BF16)
            vd[dst, :] = vn[src, :].astype(BF16)
            return carry

        lax.fori_loop(0, seq // chunk, gather, 0)

        def natural_rows(n, d=d, blk_per_res=blk_per_res):
            if d == 1:
                return pl.ds(pl.multiple_of(n * qb, qb), qb)
            return pl.ds(n // blk_per_res + (n % blk_per_res) * qb * d, qb, stride=d)

        def packed_rows(n, size=qb):
            return pl.ds(pl.multiple_of(n * qb, qb), size)

        def scores(it, carry, blk_per_res=blk_per_res):
            for g in range(grp_mm):
                n = it * grp_mm + g
                q = qn[natural_rows(n), :].astype(BF16)
                s = lax.dot_general(q, kd[packed_rows(n, kw), :], (((1,), (1,)), ((), ())),
                                    preferred_element_type=F32) * scale
                i = n % blk_per_res
                kind = jnp.where(i == 0, 1, jnp.where(i == blk_per_res - 1, 2, 0))
                sbuf[packed_rows(n), :] = s + bias[kind]
            return carry

        lax.fori_loop(0, nblk // grp_mm, scores, 0)

        def softmax(it, carry, first=first):
            for g in range(grp):
                rows = packed_rows(it * grp + g)
                s = sbuf[rows, :]
                m = jnp.max(s, axis=-1, keepdims=True)
                p = jnp.exp(s - m)
                z = jnp.sum(p, axis=-1, keepdims=True)
                pbuf[rows, :] = p.astype(BF16)
                (mrun if first else mblk)[rows, :] = jnp.broadcast_to(m, (qb, HEAD_DIM))
                (zrun if first else zblk)[rows, :] = jnp.broadcast_to(z, (qb, HEAD_DIM))
            return carry

        lax.fori_loop(0, nblk // grp, softmax, 0)

        def values(it, carry, first=first):
            for g in range(grp_mm):
                n = it * grp_mm + g
                o = jnp.dot(pbuf[packed_rows(n), :], vd[packed_rows(n, kw), :], preferred_element_type=F32)
                if first:
                    acc[packed_rows(n), :] = o
                else:
                    rw = natural_rows(n)
                    m_old = mrun[rw, :]
                    mb = mblk[packed_rows(n), :]
                    m_new = jnp.maximum(m_old, mb)
                    w_old = jnp.exp(m_old - m_new)
                    w_blk = jnp.exp(mb - m_new)
                    acc[rw, :] = w_old * acc[rw, :] + w_blk * o
                    zrun[rw, :] = w_old * zrun[rw, :] + w_blk * zblk[packed_rows(n), :]
                    mrun[rw, :] = m_new
            return carry

        lax.fori_loop(0, nblk // grp_mm, values, 0)

    def finish(i, carry):
        rows = pl.ds(pl.multiple_of(i * tile, tile), tile)
        y_ref[0, rows, :] = (acc[rows, :] / zrun[rows, :]).astype(y_ref.dtype)
        return carry

    lax.fori_loop(0, seq // tile, finish, 0)


def dilated_mixer(y3, v3, cos_full, sin_signed, q_gain, k_gain):
    b, s, _ = y3.shape
    assert all(w // (2 * d) == A_REACH_BLOCK for w, d in A_PATTERNS)
    assert all((s // d) % 256 == 0 for _, d in A_PATTERNS) and (s // A_QUERY_BLOCK) % A_GROUP_MATMUL == 0
    kw = A_QUERY_BLOCK + 2 * A_REACH_BLOCK
    head = lambda off: pl.BlockSpec((1, s, HEAD_DIM), lambda bi, h: (bi, 0, off // HEAD_DIM + h))
    table = pl.BlockSpec((s, HEAD_DIM), lambda bi, h: (0, 0))
    vec = pl.BlockSpec((1, HEAD_DIM), lambda bi, h: (0, 0))
    pad_rows = s + 2 * A_REACH_BLOCK
    return pl.pallas_call(
        functools.partial(_dilated_kernel, seq=s),
        grid=(b, A_HEADS),
        in_specs=[head(Y_AQ), head(Y_AK), head(V_AV), table, table, vec, vec],
        out_specs=pl.BlockSpec((1, s, HEAD_DIM), lambda bi, h: (bi, 0, h)),
        out_shape=jax.ShapeDtypeStruct((b, s, A_WIDTH), BF16),
        scratch_shapes=[pltpu.VMEM((s, HEAD_DIM), F32)] * 3 + [pltpu.VMEM((pad_rows, HEAD_DIM), BF16)] * 2
        + [pltpu.VMEM((s, HEAD_DIM), F32)] * 3
        + [pltpu.VMEM((s, kw), F32), pltpu.VMEM((s, kw), BF16)] + [pltpu.VMEM((s, HEAD_DIM), F32)] * 2
        + [pltpu.VMEM((3, A_QUERY_BLOCK, kw), F32)],
        compiler_params=_cparams(("parallel", "parallel")),
        name="dilated_mixer",
    )(y3, y3, v3, cos_full, sin_signed, q_gain.reshape(1, HEAD_DIM), k_gain.reshape(1, HEAD_DIM))


def _gqa_kernel(q_ref, k_ref, v_ref, cosq_ref, sinq_ref, cosk_ref, sink_ref, qg_ref, kg_ref, o_ref, kn, *, group, seq):
    half = HEAD_DIM // 4
    tile = 512

    @pl.when(pl.program_id(2) == 0)
    def _():
        def prep(i, carry):
            rows = pl.ds(pl.multiple_of(i * tile, tile), tile)
            kn[rows, :] = _norm_rope(k_ref[0, rows, :], kg_ref[...], cosk_ref[rows, :], sink_ref[rows, :],
                                     half).astype(kn.dtype)
            return carry

        lax.fori_loop(0, seq // tile, prep, 0)

    k = kn[...]
    v = v_ref[0]
    c = (HEAD_DIM ** -0.5) * math.log2(math.e)
    for g in range(group):
        cols = slice(g * HEAD_DIM, (g + 1) * HEAD_DIM)
        q = _norm_rope(q_ref[0, :, cols], qg_ref[...], cosq_ref[...], sinq_ref[...], half).astype(BF16)
        s = lax.dot_general(q, k, (((1,), (1,)), ((), ())), preferred_element_type=F32)
        m = jnp.max(s, axis=-1, keepdims=True)
        p = jnp.exp2((s - m) * c)
        l = jnp.sum(p, axis=-1, keepdims=True)
        o = jnp.dot(p.astype(BF16), v, preferred_element_type=F32) / l
        o_ref[0, :, cols] = o.astype(o_ref.dtype)


def gqa_attention(y3, v3, cos_full, sin_signed, q_gain, k_gain, tq=256):
    b, s, _ = y3.shape
    group = B_Q_HEADS // B_KV_HEADS
    gw = group * HEAD_DIM
    vec = pl.BlockSpec((1, HEAD_DIM), lambda bi, h, t: (0, 0))
    qtab = pl.BlockSpec((tq, HEAD_DIM), lambda bi, h, t: (t, 0))
    ktab = pl.BlockSpec((s, HEAD_DIM), lambda bi, h, t: (0, 0))
    return pl.pallas_call(
        functools.partial(_gqa_kernel, group=group, seq=s),
        grid=(b, B_KV_HEADS, s // tq),
        in_specs=[pl.BlockSpec((1, tq, gw), lambda bi, h, t: (bi, t, Y_BQ // gw + h)),
                  pl.BlockSpec((1, s, HEAD_DIM), lambda bi, h, t: (bi, 0, Y_BK // HEAD_DIM + h)),
                  pl.BlockSpec((1, s, HEAD_DIM), lambda bi, h, t: (bi, 0, V_BV // HEAD_DIM + h)),
                  qtab, qtab, ktab, ktab, vec, vec],
        out_specs=pl.BlockSpec((1, tq, gw), lambda bi, h, t: (bi, t, h)),
        out_shape=jax.ShapeDtypeStruct((b, s, B_Q_WIDTH), BF16),
        scratch_shapes=[pltpu.VMEM((s, HEAD_DIM), BF16)],
        compiler_params=_cparams(("parallel", "parallel", "arbitrary")),
        name="gqa_attention",
    )(y3, y3, v3, cos_full, sin_signed, cos_full, sin_signed, q_gain.reshape(1, HEAD_DIM), k_gain.reshape(1, HEAD_DIM))


def _conv_kernel(xp_ref, xm_ref, xn_ref, w_ref, b_ref, qk_ref, kt_ref, buf, *, tile, ntiles):
    ti = pl.program_id(1)
    halo = 8
    buf[0:halo] = jnp.where(ti > 0, xp_ref[0], 0.0)
    buf[halo:halo + tile] = xm_ref[0]
    buf[halo + tile:] = jnp.where(ti < ntiles - 1, xn_ref[0], 0.0)
    pad = C_CONV // 2
    acc = w_ref[0:1, :] * buf[halo - pad:halo - pad + tile, :]
    for j in range(1, C_CONV):
        acc = acc + w_ref[j:j + 1, :] * buf[halo - pad + j:halo - pad + j + tile, :]
    acc = acc + b_ref[...]
    act = acc * jax.nn.sigmoid(acc)
    q = act[:, :C_QK_WIDTH]
    k = act[:, C_QK_WIDTH:] * (C_QK_DIM ** -0.5)
    qk_ref[0, :, :C_QK_WIDTH] = q.astype(qk_ref.dtype)
    qk_ref[0, :, C_QK_WIDTH:] = k.astype(qk_ref.dtype)
    for c in range(tile // MLSTM_CHUNK):
        kt_ref[0, c] = k[c * MLSTM_CHUNK:(c + 1) * MLSTM_CHUNK, :].T.astype(kt_ref.dtype)


def mlstm_conv(y3, conv_w, conv_b, tile=512):
    b, s, _ = y3.shape
    ntiles = s // tile
    w2 = 2 * C_QK_WIDTH
    cb = Y_CQK // w2
    per = tile // 8
    return pl.pallas_call(
        functools.partial(_conv_kernel, tile=tile, ntiles=ntiles),
        grid=(b, ntiles),
        in_specs=[pl.BlockSpec((1, 8, w2), lambda bi, t: (bi, jnp.maximum(t * per - 1, 0), cb)),
                  pl.BlockSpec((1, tile, w2), lambda bi, t: (bi, t, cb)),
                  pl.BlockSpec((1, 8, w2), lambda bi, t: (bi, jnp.minimum((t + 1) * per, s // 8 - 1), cb)),
                  pl.BlockSpec((C_CONV, w2), lambda bi, t: (0, 0)),
                  pl.BlockSpec((1, w2), lambda bi, t: (0, 0))],
        out_specs=[pl.BlockSpec((1, tile, w2), lambda bi, t: (bi, t, 0)),
                   pl.BlockSpec((1, tile // MLSTM_CHUNK, C_QK_WIDTH, MLSTM_CHUNK), lambda bi, t: (bi, t, 0, 0))],
        out_shape=[jax.ShapeDtypeStruct((b, s, w2), BF16),
                   jax.ShapeDtypeStruct((b, s // MLSTM_CHUNK, C_QK_WIDTH, MLSTM_CHUNK), BF16)],
        scratch_shapes=[pltpu.VMEM((tile + 16, w2), F32)],
        compiler_params=_cparams(("parallel", "parallel")),
        name="mlstm_conv",
    )(y3, y3, y3, conv_w, conv_b.reshape(1, w2))


def _gate_kernel(gi_ref, gf_ref, bi_ref, bf_ref, cb_ref, inter_ref, e_ref, rt_ref, wkt_ref, dec_ref,
                 b_s, r_s, dmax_s, btot_s, gmax_s, mpf_s, mpb_s, *, nc):
    lc = MLSTM_CHUNK
    t_in = lax.broadcasted_iota(jnp.int32, (lc, LANES), 0)
    fwd = lax.broadcasted_iota(jnp.int32, (lc, LANES), 1) % 2 == 0
    fwd_row = fwd[0:1]
    steps = [1 << i for i in range(lc.bit_length() - 1)]

    def chunk_rows(c):
        return pl.ds(pl.multiple_of(c * lc, lc), lc)

    def local(c, carry):
        rows = chunk_rows(c)
        ii = gi_ref[0, rows, :] + bi_ref[...]
        ff = gf_ref[0, rows, :] + bf_ref[...]
        lf = jnp.minimum(ff, 0.0) - jnp.log1p(jnp.exp(-jnp.abs(ff)))
        cs = lf
        rs = lf
        for k in steps:
            cs = cs + jnp.where(t_in >= k, pltpu.roll(cs, k, 0), 0.0)
            rs = rs + jnp.where(t_in + k < lc, pltpu.roll(rs, lc - k, 0), 0.0)
        b = jnp.where(fwd, cs, rs)
        btot = cs + rs - lf
        r = ii - b
        pm = r
        sm = r
        for k in steps:
            pm = jnp.where(t_in >= k, jnp.maximum(pm, pltpu.roll(pm, k, 0)), pm)
            sm = jnp.where(t_in + k < lc, jnp.maximum(sm, pltpu.roll(sm, lc - k, 0)), sm)
        b_s[rows, :] = b
        r_s[rows, :] = r
        dmax_s[rows, :] = b + jnp.where(fwd, pm, sm)
        btot_s[pl.ds(c, 1), :] = btot[0:1]
        gmax_s[pl.ds(c, 1), :] = (btot + jnp.maximum(pm, sm))[0:1]
        return carry

    lax.fori_loop(0, nc, local, 0)

    def scan(k, carry):
        mf, mb = carry
        mpf_s[pl.ds(k, 1), :] = mf
        mf = jnp.maximum(btot_s[pl.ds(k, 1), :] + mf, gmax_s[pl.ds(k, 1), :])
        kb = nc - 1 - k
        mpb_s[pl.ds(kb, 1), :] = mb
        mb = jnp.maximum(btot_s[pl.ds(kb, 1), :] + mb, gmax_s[pl.ds(kb, 1), :])
        return mf, mb

    zero = jnp.zeros((1, LANES), F32)
    lax.fori_loop(0, nc, scan, (zero, zero))

    def emit(c, carry):
        rows = chunk_rows(c)
        b = b_s[rows, :]
        r = r_s[rows, :]
        btot = btot_s[pl.ds(c, 1), :]
        mprev = jnp.where(fwd_row, mpf_s[pl.ds(c, 1), :], mpb_s[pl.ds(c, 1), :])
        mnew = jnp.maximum(btot + mprev, gmax_s[pl.ds(c, 1), :])
        dec_ref[0, pl.ds(c, 1), :] = jnp.exp(btot + mprev - mnew)
        m_inter = b + mprev
        m_t = jnp.maximum(m_inter, dmax_s[rows, :])
        cb_ref[0, rows, :] = b - m_t
        inter_ref[0, rows, :] = jnp.exp(m_inter - m_t)
        e_ref[0, rows, :] = jnp.exp(-m_t)
        wk = jnp.exp(btot + r - mnew)
        rt_ref[0, c] = r.T[:C_STREAMS, :]
        wkt_ref[0, c] = wk.T[:C_STREAMS, :]
        return carry

    lax.fori_loop(0, nc, emit, 0)


def mlstm_gates(y3, bias_i, bias_f):
    b, s, _ = y3.shape
    nc = s // MLSTM_CHUNK
    col = pl.BlockSpec((1, s, LANES), lambda bi: (bi, 0, 0))
    row = pl.BlockSpec((1, nc, C_STREAMS, MLSTM_CHUNK), lambda bi: (bi, 0, 0, 0))
    vec = pl.BlockSpec((1, LANES), lambda bi: (0, 0))
    return pl.pallas_call(
        functools.partial(_gate_kernel, nc=nc),
        grid=(b,),
        in_specs=[pl.BlockSpec((1, s, LANES), lambda bi: (bi, 0, Y_GI // LANES)),
                  pl.BlockSpec((1, s, LANES), lambda bi: (bi, 0, Y_GF // LANES)), vec, vec],
        out_specs=[col, col, col, row, row, pl.BlockSpec((1, nc, LANES), lambda bi: (bi, 0, 0))],
        out_shape=[jax.ShapeDtypeStruct((b, s, LANES), F32)] * 3
        + [jax.ShapeDtypeStruct((b, nc, C_STREAMS, MLSTM_CHUNK), F32)] * 2
        + [jax.ShapeDtypeStruct((b, nc, LANES), F32)],
        scratch_shapes=[pltpu.VMEM((s, LANES), F32)] * 3 + [pltpu.VMEM((nc, LANES), F32)] * 4,
        compiler_params=_cparams(("parallel",)),
        name="mlstm_gates",
    )(y3, y3, bias_i.reshape(1, LANES), bias_f.reshape(1, LANES))


def _mlstm_kernel(dec_ref, q_ref, kt_ref, v_ref, o_ref, g_ref, cb_ref, inter_ref, e_ref, rt_ref, wkt_ref,
                  y_ref, cbs, inters, es, hf, hb, ct, *, seq, nc):
    lc = MLSTM_CHUNK
    bi = pl.program_id(0)
    pair = pl.program_id(1)
    shift = (LANES - pair * 4) % LANES

    def align(c, carry):
        rows = pl.ds(pl.multiple_of(c * lc, lc), lc)
        cbs[rows, :] = pltpu.roll(cb_ref[0, rows, :], shift, 1)
        inters[rows, :] = pltpu.roll(inter_ref[0, rows, :], shift, 1)
        es[rows, :] = pltpu.roll(e_ref[0, rows, :], shift, 1)
        return carry

    lax.fori_loop(0, nc, align, 0)
    ct[...] = jnp.zeros_like(ct)
    lane = lax.broadcasted_iota(jnp.int32, (lc, LANES), 1)
    trow = lax.broadcasted_iota(jnp.int32, (lc, lc), 0)
    scol = lax.broadcasted_iota(jnp.int32, (lc, lc), 1)
    ones = jnp.ones((lc, C_V_DIM), BF16)

    def step(it, carry):
        for direction in range(2):
            c = it if direction == 0 else nc - 1 - it
            rows = pl.ds(pl.multiple_of(c * lc, lc), lc)
            mask = (scol <= trow) if direction == 0 else (scol >= trow)
            q2 = q_ref[0, rows, :]
            kt2 = kt_ref[0, c]
            for hh in range(2):
                sl = hh * 2 + direction
                stream = pair * 4 + sl
                dec = dec_ref[(bi * nc + c) * C_STREAMS + stream]
                qh = jnp.where((lane >= hh * C_QK_DIM) & (lane < (hh + 1) * C_QK_DIM), q2, jnp.zeros_like(q2))
                vaug = jnp.concatenate([v_ref[0, rows, hh * C_V_DIM:(hh + 1) * C_V_DIM], ones], axis=1)
                sc = jnp.dot(qh, kt2, preferred_element_type=F32)
                arg = cbs[rows, sl:sl + 1] + rt_ref[0, c, pl.ds(stream, 1), :]
                a = jnp.exp(jnp.where(mask, arg, NEG_INF)) * sc
                num_a = jnp.dot(a.astype(BF16), vaug, preferred_element_type=F32)
                ct_pair = ct[direction]
                num_c = jnp.dot(qh, ct_pair.astype(BF16), preferred_element_type=F32)
                num = num_a + inters[rows, sl:sl + 1] * num_c
                den = jnp.maximum(jnp.abs(num[:, C_V_DIM:]), es[rows, sl:sl + 1])
                h = num[:, :C_V_DIM] / den
                dst = hf if direction == 0 else hb
                dst[rows, hh * C_V_DIM:(hh + 1) * C_V_DIM] = h
                kw = (kt2[hh * C_QK_DIM:(hh + 1) * C_QK_DIM, :].astype(F32)
                      * wkt_ref[0, c, pl.ds(stream, 1), :]).astype(BF16)
                u = jnp.dot(kw, vaug, preferred_element_type=F32)
                d_rows = slice(hh * C_QK_DIM, (hh + 1) * C_QK_DIM)
                ct[direction, d_rows, :] = dec * ct[direction, d_rows, :] + u
        return carry

    lax.fori_loop(0, nc, step, 0, unroll=2)

    tile = 512

    def finish(i, carry):
        rows = pl.ds(pl.multiple_of(i * tile, tile), tile)
        for hh in range(2):
            cols = slice(hh * C_V_DIM, (hh + 1) * C_V_DIM)
            h = hf[rows, cols] + hb[rows, cols]
            h = h * lax.rsqrt(jnp.mean(h * h, axis=-1, keepdims=True) + EPS)
            h = h * g_ref[:, cols]
            y_ref[0, rows, cols] = (h * jax.nn.sigmoid(o_ref[0, rows, cols])).astype(y_ref.dtype)
        return carry

    lax.fori_loop(0, seq // tile, finish, 0)


def mlstm_scan(dec, qk, kt, v3, y3, out_g, cb, inter, e, rt, wkt):
    b, s, _ = qk.shape
    nc = s // MLSTM_CHUNK
    pw = 2 * C_V_DIM
    col = pl.BlockSpec((1, s, LANES), lambda bi, p, d: (bi, 0, 0))
    row = pl.BlockSpec((1, nc, C_STREAMS, MLSTM_CHUNK), lambda bi, p, d: (bi, 0, 0, 0))
    grid_spec = pltpu.PrefetchScalarGridSpec(
        num_scalar_prefetch=1,
        grid=(b, C_PAIRS),
        in_specs=[pl.BlockSpec((1, s, LANES), lambda bi, p, d: (bi, 0, p)),
                  pl.BlockSpec((1, nc, LANES, MLSTM_CHUNK), lambda bi, p, d: (bi, 0, p, 0)),
                  pl.BlockSpec((1, s, pw), lambda bi, p, d: (bi, 0, V_CV // pw + p)),
                  pl.BlockSpec((1, s, pw), lambda bi, p, d: (bi, 0, Y_CO // pw + p)),
                  pl.BlockSpec((1, pw), lambda bi, p, d: (0, p)),
                  col, col, col, row, row],
        out_specs=pl.BlockSpec((1, s, pw), lambda bi, p, d: (bi, 0, p)),
        scratch_shapes=[pltpu.VMEM((s, LANES), F32)] * 3 + [pltpu.VMEM((s, pw), F32)] * 2
        + [pltpu.VMEM((2, LANES, 2 * C_V_DIM), F32)],
    )
    return pl.pallas_call(
        functools.partial(_mlstm_kernel, seq=s, nc=nc),
        grid_spec=grid_spec,
        out_shape=jax.ShapeDtypeStruct((b, s, C_HEADS * C_V_DIM), BF16),
        compiler_params=_cparams(("parallel", "arbitrary")),
        name="mlstm_scan",
    )(dec, qk, kt, v3, y3, out_g.reshape(1, C_V_WIDTH), cb, inter, e, rt, wkt)


def _merge_kernel(ya_ref, yb_ref, yc_ref, ga_ref, gb_ref, gc_ref, wa_ref, wb_ref, wc_ref, o_ref):
    acc = jax.nn.sigmoid(ga_ref[...]) * jnp.dot(ya_ref[...], wa_ref[...], preferred_element_type=F32)
    acc = acc + jax.nn.sigmoid(gb_ref[...]) * jnp.dot(yb_ref[...], wb_ref[...], preferred_element_type=F32)
    acc = acc + jax.nn.sigmoid(gc_ref[...]) * jnp.dot(yc_ref[...], wc_ref[...], preferred_element_type=F32)
    o_ref[...] = acc.astype(o_ref.dtype)


def branch_merge(ya, yb, yc, y2, wa, wb, wc, tm=512, tn=1024):
    m = ya.shape[0]
    n = wa.shape[1]
    nj = n // tn

    def act(width):
        return pl.BlockSpec((tm, width), lambda j, i: (i, 0))

    def gate(branch):
        return pl.BlockSpec((tm, tn), lambda j, i: (i, Y_G // tn + branch * nj + j))

    def weight(width):
        return pl.BlockSpec((width, tn), lambda j, i: (0, j))

    return pl.pallas_call(
        _merge_kernel,
        grid=(nj, m // tm),
        in_specs=[act(A_WIDTH), act(B_Q_WIDTH), act(C_V_WIDTH), gate(0), gate(1), gate(2),
                  weight(A_WIDTH), weight(B_Q_WIDTH), weight(C_V_WIDTH)],
        out_specs=pl.BlockSpec((tm, tn), lambda j, i: (i, j)),
        out_shape=jax.ShapeDtypeStruct((m, n), BF16),
        compiler_params=_cparams(("parallel", "parallel")),
        name="branch_merge",
    )(ya, yb, yc, y2, y2, y2, wa, wb, wc)


def _rope_tables(pos, dim):
    inv_freq = ROPE_THETA ** (-jnp.arange(0, dim, 2, dtype=F32) / dim)
    ang = pos.astype(F32)[:, None] * inv_freq[None, :]
    return jnp.cos(ang), jnp.sin(ang)


def _stream_gate_columns(w_gate, b_gate):
    lanes = jnp.arange(C_STREAMS)
    head = (lanes // 4) * 2 + (lanes % 4) // 2
    direction = lanes % 2
    i_cols = direction * 2 * C_HEADS + head
    f_cols = i_cols + C_HEADS
    pad = LANES - C_STREAMS
    wi = jnp.pad(w_gate[:, i_cols], ((0, 0), (0, pad)))
    wf = jnp.pad(w_gate[:, f_cols], ((0, 0), (0, pad)))
    return wi, wf, jnp.pad(b_gate[i_cols], (0, pad)), jnp.pad(b_gate[f_cols], (0, pad))


def _layer(x2, shape, tables, norm1_g, w_in, a_q_gain, a_k_gain, b_q_gain, b_k_gain, c_conv_w, c_conv_b,
           c_gate_b, c_out_g, w_branch_a, w_branch_b, w_branch_c, w_out, norm2_g, w_up, w_down):
    bsz, seq = shape
    cos_a, sin_a, cos_b, sin_b = tables
    sizes = (A_WIDTH, A_WIDTH, A_WIDTH, B_Q_WIDTH, B_KV_WIDTH, B_KV_WIDTH, 2 * C_QK_WIDTH, C_V_WIDTH, C_V_WIDTH,
             4 * C_HEADS, N_BRANCHES * x2.shape[1])
    offs = [0]
    for sz in sizes:
        offs.append(offs[-1] + sz)
    seg = lambda i: w_in[:, offs[i]:offs[i + 1]]
    wi, wf, bias_i, bias_f = _stream_gate_columns(seg(9), c_gate_b)
    w_y = jnp.concatenate([seg(10), seg(6), seg(8), seg(3), seg(0), seg(1), seg(4), wi, wf], axis=1).astype(BF16)
    w_v = jnp.concatenate([seg(2), seg(5), seg(7)], axis=1).astype(BF16)

    xn = rmsnorm(x2, norm1_g)
    y2 = matmul(xn, w_y, out_dtype=F32, tm=2048)
    v2 = matmul(xn, w_v, out_dtype=BF16, tm=2048)
    y3 = y2.reshape(bsz, seq, Y_COLS)
    v3 = v2.reshape(bsz, seq, V_COLS)

    ya = dilated_mixer(y3, v3, cos_a, sin_a, a_q_gain, a_k_gain).reshape(bsz * seq, A_WIDTH)
    yb = gqa_attention(y3, v3, cos_b, sin_b, b_q_gain, b_k_gain).reshape(bsz * seq, B_Q_WIDTH)

    qk_c, kt_c = mlstm_conv(y3, c_conv_w, c_conv_b)
    cb, inter, e, rt, wkt, dec = mlstm_gates(y3, bias_i, bias_f)
    dec_flat = dec[:, :, :C_STREAMS].reshape(-1)
    yc = mlstm_scan(dec_flat, qk_c, kt_c, v3, y3, c_out_g, cb, inter, e, rt, wkt).reshape(bsz * seq, C_V_WIDTH)

    merged = branch_merge(ya, yb, yc, y2, w_branch_a.astype(BF16), w_branch_b.astype(BF16), w_branch_c.astype(BF16))
    h = matmul(merged, w_out.astype(BF16), out_dtype=F32, res=x2)
    hn = rmsnorm(h, norm2_g)
    u2 = matmul(hn, w_up.astype(BF16), out_dtype=BF16, relu2=True, tm=2048)
    return matmul(u2, w_down.astype(BF16), out_dtype=F32, res=h)


def kernel(x, norm1_g, w_in, a_q_gain, a_k_gain, b_q_gain, b_k_gain, c_conv_w, c_conv_b, c_gate_b, c_out_g, w_branch_a, w_branch_b, w_branch_c, w_out, norm2_g, w_up, w_down):
    bsz, seq, d_model = x.shape
    pos = jnp.arange(seq)
    cos1, sin1 = _rope_tables(pos, HEAD_DIM)
    cos_r, sin_r = _rope_tables(pos // GRID_W, HEAD_DIM // 2)
    cos_c, sin_c = _rope_tables(pos % GRID_W, HEAD_DIM // 2)
    tables = (jnp.concatenate([cos1, cos1], axis=1), jnp.concatenate([-sin1, sin1], axis=1),
              jnp.concatenate([cos_r, cos_r, cos_c, cos_c], axis=1),
              jnp.concatenate([-sin_r, sin_r, -sin_c, sin_c], axis=1))
    x2 = x.reshape(bsz * seq, d_model)
    for l in range(norm1_g.shape[0]):
        x2 = _layer(x2, (bsz, seq), tables, norm1_g[l], w_in[l], a_q_gain[l], a_k_gain[l], b_q_gain[l], b_k_gain[l],
                    c_conv_w[l], c_conv_b[l], c_gate_b[l], c_out_g[l], w_branch_a[l], w_branch_b[l], w_branch_c[l],
                    w_out[l], norm2_g[l], w_up[l], w_down[l])
    return x2.reshape(bsz, seq, d_model)
```

```python
import functools
import math

import jax
import jax.numpy as jnp
from jax import lax
from jax.experimental import pallas as pl
from jax.experimental.pallas import tpu as pltpu

F32 = jnp.float32
BF16 = jnp.bfloat16

HEAD_DIM = 128
GRID_W = 64
ROPE_THETA = 10000.0
EPS = 1e-6
NEG_INF = -1e30

A_HEADS = 6
A_PATTERNS = ((128, 1), (512, 4), (2048, 16))
A_REACH_BLOCK = 64
A_QUERY_BLOCK = 128
A_GROUP = 4
A_GROUP_MATMUL = 8
B_Q_HEADS = 8
B_KV_HEADS = 2
C_HEADS = 8
C_QK_DIM = 64
C_V_DIM = 128
C_CONV = 5
N_BRANCHES = 3

A_WIDTH = A_HEADS * HEAD_DIM
B_Q_WIDTH = B_Q_HEADS * HEAD_DIM
B_KV_WIDTH = B_KV_HEADS * HEAD_DIM
C_QK_WIDTH = C_HEADS * C_QK_DIM
C_V_WIDTH = C_HEADS * C_V_DIM

LANES = 128
MLSTM_CHUNK = 128
C_PAIRS = C_HEADS // 2
C_STREAMS = 2 * C_HEADS
VMEM_LIMIT = 60 * 1024 * 1024

Y_G = 0
Y_CQK = 6144
Y_CO = 7168
Y_BQ = 8192
Y_AQ = 9216
Y_AK = 9984
Y_BK = 10752
Y_GI = 11008
Y_GF = 11136
Y_COLS = 11264
V_AV = 0
V_BV = 768
V_CV = 1024
V_COLS = 2048


def _cparams(sem):
    return pltpu.CompilerParams(dimension_semantics=sem, vmem_limit_bytes=VMEM_LIMIT)


def _rmsnorm_kernel(x_ref, g_ref, o_ref):
    x = x_ref[...]
    r = lax.rsqrt(jnp.mean(x * x, axis=-1, keepdims=True) + EPS)
    o_ref[...] = ((x * r) * g_ref[...]).astype(o_ref.dtype)


def rmsnorm(x, g, tm=512):
    m, d = x.shape
    return pl.pallas_call(
        _rmsnorm_kernel,
        grid=(m // tm,),
        in_specs=[pl.BlockSpec((tm, d), lambda i: (i, 0)), pl.BlockSpec((1, d), lambda i: (0, 0))],
        out_specs=pl.BlockSpec((tm, d), lambda i: (i, 0)),
        out_shape=jax.ShapeDtypeStruct((m, d), BF16),
        compiler_params=_cparams(("parallel",)),
        name="rmsnorm",
    )(x, g.reshape(1, d))


def _matmul_kernel(*refs, nk, relu2, has_res):
    if has_res:
        a_ref, w_ref, r_ref, o_ref, acc_ref = refs
    else:
        a_ref, w_ref, o_ref, acc_ref = refs
        r_ref = None

    def finish(acc):
        if relu2:
            u = jnp.maximum(acc, 0.0)
            acc = u * u
        if has_res:
            acc = r_ref[...] + acc
        o_ref[...] = acc.astype(o_ref.dtype)

    prod = jnp.dot(a_ref[...], w_ref[...], preferred_element_type=F32)
    if nk == 1:
        finish(prod)
    else:
        k = pl.program_id(2)

        @pl.when(k == 0)
        def _():
            acc_ref[...] = prod

        @pl.when(k > 0)
        def _():
            acc_ref[...] += prod

        @pl.when(k == nk - 1)
        def _():
            finish(acc_ref[...])


def matmul(a, w, *, out_dtype, res=None, relu2=False, tm=1024, tn=1024, tk=2048):
    m, kdim = a.shape
    n = w.shape[1]
    tn = min(tn, n)
    tk = min(tk, kdim)
    nk = kdim // tk
    in_specs = [pl.BlockSpec((tm, tk), lambda i, j, k: (i, k)),
                pl.BlockSpec((tk, tn), lambda i, j, k: (k, j))]
    args = [a, w]
    if res is not None:
        in_specs.append(pl.BlockSpec((tm, tn), lambda i, j, k: (i, j)))
        args.append(res)
    return pl.pallas_call(
        functools.partial(_matmul_kernel, nk=nk, relu2=relu2, has_res=res is not None),
        grid=(m // tm, n // tn, nk),
        in_specs=in_specs,
        out_specs=pl.BlockSpec((tm, tn), lambda i, j, k: (i, j)),
        out_shape=jax.ShapeDtypeStruct((m, n), out_dtype),
        scratch_shapes=[pltpu.VMEM((tm, tn) if nk > 1 else (8, LANES), F32)],
        compiler_params=_cparams(("parallel", "parallel", "arbitrary")),
        name="matmul",
    )(*args)


def _norm_rows_to(x_ref, g_ref, dst, copy_to=None, sub=256):
    def body(i, carry):
        rows = pl.ds(pl.multiple_of(i * sub, sub), sub)
        x = x_ref[rows, :]
        r = lax.rsqrt(jnp.mean(x * x, axis=-1, keepdims=True) + EPS)
        dst[rows, :] = ((x * r) * g_ref[...]).astype(dst.dtype)
        if copy_to is not None:
            copy_to[rows, :] = x
        return carry

    lax.fori_loop(0, x_ref.shape[0] // sub, body, 0)


def _norm_matmul_kernel(x_ref, g_ref, w_ref, o_ref, xn):
    @pl.when(pl.program_id(1) == 0)
    def _():
        _norm_rows_to(x_ref, g_ref, xn)

    o_ref[...] = jnp.dot(xn[...], w_ref[...], preferred_element_type=F32).astype(o_ref.dtype)


def norm_matmul(x, g, w, *, out_dtype, tm=1024, tn=1024):
    m, d = x.shape
    n = w.shape[1]
    return pl.pallas_call(
        _norm_matmul_kernel,
        grid=(m // tm, n // tn),
        in_specs=[pl.BlockSpec((tm, d), lambda i, j: (i, 0), pipeline_mode=pl.Buffered(1)),
                  pl.BlockSpec((1, d), lambda i, j: (0, 0)),
                  pl.BlockSpec((d, tn), lambda i, j: (0, j))],
        out_specs=pl.BlockSpec((tm, tn), lambda i, j: (i, j)),
        out_shape=jax.ShapeDtypeStruct((m, n), out_dtype),
        scratch_shapes=[pltpu.VMEM((tm, d), BF16)],
        compiler_params=_cparams(("parallel", "arbitrary")),
        name="norm_matmul",
    )(x, g.reshape(1, d), w)


def _ffn_kernel(h_ref, g_ref, wu_ref, wd_ref, o_ref, hn):
    @pl.when(pl.program_id(1) == 0)
    def _():
        _norm_rows_to(h_ref, g_ref, hn, copy_to=o_ref)

    u = jnp.maximum(jnp.dot(hn[...], wu_ref[...], preferred_element_type=F32), 0.0)
    o_ref[...] += jnp.dot((u * u).astype(BF16), wd_ref[...], preferred_element_type=F32)


def ffn(h, g, w_up, w_down, tm=1024, tf=512):
    m, d = h.shape
    f = w_up.shape[1]
    return pl.pallas_call(
        _ffn_kernel,
        grid=(m // tm, f // tf),
        in_specs=[pl.BlockSpec((tm, d), lambda i, j: (i, 0), pipeline_mode=pl.Buffered(1)),
                  pl.BlockSpec((1, d), lambda i, j: (0, 0)),
                  pl.BlockSpec((d, tf), lambda i, j: (0, j)),
                  pl.BlockSpec((tf, d), lambda i, j: (j, 0))],
        out_specs=pl.BlockSpec((tm, d), lambda i, j: (i, 0)),
        out_shape=jax.ShapeDtypeStruct((m, d), F32),
        scratch_shapes=[pltpu.VMEM((tm, d), BF16)],
        compiler_params=_cparams(("parallel", "arbitrary")),
        name="ffn",
    )(h, g.reshape(1, d), w_up, w_down)


def _norm_rope(x, gain, cos_full, sin_signed, half):
    y = (x * lax.rsqrt(jnp.mean(x * x, axis=-1, keepdims=True) + EPS)) * gain
    if 2 * half == LANES:
        partner = pltpu.roll(y, half, 1)
    else:
        lane = lax.broadcasted_iota(jnp.int32, y.shape, 1)
        partner = jnp.where(lane % (2 * half) < half, pltpu.roll(y, LANES - half, 1), pltpu.roll(y, half, 1))
    return y * cos_full + partner * sin_signed


def _dilated_kernel(q_ref, k_ref, v_ref, cos_ref, sin_ref, qg_ref, kg_ref, y_ref,
                    qn, kn, vn, kd, vd, acc, mrun, zrun, sbuf, pbuf, mblk, bias, *, seq):
    blk = A_REACH_BLOCK
    qb = A_QUERY_BLOCK
    kw = qb + 2 * blk
    grp = A_GROUP
    grp_mm = A_GROUP_MATMUL
    tile = 512
    scale = HEAD_DIM ** -0.5

    def prep(i, carry):
        rows = pl.ds(pl.multiple_of(i * tile, tile), tile)
        cos = cos_ref[rows, :]
        sin = sin_ref[rows, :]
        qn[rows, :] = _norm_rope(q_ref[0, rows, :], qg_ref[...], cos, sin, HEAD_DIM // 2)
        kn[rows, :] = _norm_rope(k_ref[0, rows, :], kg_ref[...], cos, sin, HEAD_DIM // 2)
        vn[rows, :] = v_ref[0, rows, :].astype(F32)
        return carry

    lax.fori_loop(0, seq // tile, prep, 0)
    zpad = jnp.zeros((blk, HEAD_DIM), BF16)
    kd[0:blk, :] = zpad
    kd[blk + seq:, :] = zpad
    vd[0:blk, :] = zpad
    vd[blk + seq:, :] = zpad

    t_io = lax.broadcasted_iota(jnp.int32, (qb, kw), 0)
    j_io = lax.broadcasted_iota(jnp.int32, (qb, kw), 1)
    in_reach = jnp.abs(j_io - blk - t_io) <= blk
    bias[0] = jnp.where(in_reach, 0.0, NEG_INF)
    bias[1] = jnp.where(in_reach & (j_io >= blk), 0.0, NEG_INF)
    bias[2] = jnp.where(in_reach & (j_io < qb + blk), 0.0, NEG_INF)
    nblk = seq // qb
    ones = jnp.ones((kw, HEAD_DIM), BF16)

    for pi, (_, d) in enumerate(A_PATTERNS):
        sub_len = seq // d
        chunk = 256
        per_res = sub_len // chunk
        blk_per_res = sub_len // qb
        first = pi == 0

        def gather(idx, carry, d=d, sub_len=sub_len, per_res=per_res):
            r = idx // per_res
            j = idx % per_res
            if d == 1:
                src = pl.ds(pl.multiple_of(j * chunk, chunk), chunk)
            else:
                src = pl.ds(r + j * chunk * d, chunk, stride=d)
            dst = pl.ds(pl.multiple_of(blk + r * sub_len + j * chunk, blk), chunk)
            kd[dst, :] = kn[src, :].astype(BF16)
            vd[dst, :] = vn[src, :].astype(BF16)
            return carry

        lax.fori_loop(0, seq // chunk, gather, 0)

        def natural_rows(n, d=d, blk_per_res=blk_per_res):
            if d == 1:
                return pl.ds(pl.multiple_of(n * qb, qb), qb)
            return pl.ds(n // blk_per_res + (n % blk_per_res) * qb * d, qb, stride=d)

        def packed_rows(n, size=qb):
            return pl.ds(pl.multiple_of(n * qb, qb), size)

        def scores(it, carry, blk_per_res=blk_per_res):
            for g in range(grp_mm):
                n = it * grp_mm + g
                q = qn[natural_rows(n), :].astype(BF16)
                s = lax.dot_general(q, kd[packed_rows(n, kw), :], (((1,), (1,)), ((), ())),
                                    preferred_element_type=F32) * scale
                i = n % blk_per_res
                kind = jnp.where(i == 0, 1, jnp.where(i == blk_per_res - 1, 2, 0))
                sbuf[packed_rows(n), :] = s + bias[kind]
            return carry

        lax.fori_loop(0, nblk // grp_mm, scores, 0)

        def softmax(it, carry, first=first):
            for g in range(grp):
                rows = packed_rows(it * grp + g)
                s = sbuf[rows, :]
                m = jnp.max(s, axis=-1, keepdims=True)
                pbuf[rows, :] = jnp.exp(s - m).astype(BF16)
                (mrun if first else mblk)[rows, :] = jnp.broadcast_to(m, (qb, HEAD_DIM))
            return carry

        lax.fori_loop(0, nblk // grp, softmax, 0)

        def values(it, carry, first=first):
            for g in range(grp_mm):
                n = it * grp_mm + g
                vaug = jnp.concatenate([vd[packed_rows(n, kw), :], ones], axis=1)
                oz = jnp.dot(pbuf[packed_rows(n), :], vaug, preferred_element_type=F32)
                o = oz[:, :HEAD_DIM]
                z = oz[:, HEAD_DIM:]
                if first:
                    acc[packed_rows(n), :] = o
                    zrun[packed_rows(n), :] = z
                else:
                    rw = natural_rows(n)
                    m_old = mrun[rw, :]
                    mb = mblk[packed_rows(n), :]
                    m_new = jnp.maximum(m_old, mb)
                    w_old = jnp.exp(m_old - m_new)
                    w_blk = jnp.exp(mb - m_new)
                    acc[rw, :] = w_old * acc[rw, :] + w_blk * o
                    zrun[rw, :] = w_old * zrun[rw, :] + w_blk * z
                    mrun[rw, :] = m_new
            return carry

        lax.fori_loop(0, nblk // grp_mm, values, 0)

    def finish(i, carry):
        rows = pl.ds(pl.multiple_of(i * tile, tile), tile)
        y_ref[0, rows, :] = (acc[rows, :] / zrun[rows, :]).astype(y_ref.dtype)
        return carry

    lax.fori_loop(0, seq // tile, finish, 0)


def dilated_mixer(y3, v3, cos_full, sin_signed, q_gain, k_gain):
    b, s, _ = y3.shape
    assert all(w // (2 * d) == A_REACH_BLOCK for w, d in A_PATTERNS)
    assert all((s // d) % 256 == 0 for _, d in A_PATTERNS) and (s // A_QUERY_BLOCK) % A_GROUP_MATMUL == 0
    kw = A_QUERY_BLOCK + 2 * A_REACH_BLOCK
    head = lambda off: pl.BlockSpec((1, s, HEAD_DIM), lambda bi, h: (bi, 0, off // HEAD_DIM + h))
    table = pl.BlockSpec((s, HEAD_DIM), lambda bi, h: (0, 0))
    vec = pl.BlockSpec((1, HEAD_DIM), lambda bi, h: (0, 0))
    pad_rows = s + 2 * A_REACH_BLOCK
    return pl.pallas_call(
        functools.partial(_dilated_kernel, seq=s),
        grid=(b, A_HEADS),
        in_specs=[head(Y_AQ), head(Y_AK), head(V_AV), table, table, vec, vec],
        out_specs=pl.BlockSpec((1, s, HEAD_DIM), lambda bi, h: (bi, 0, h)),
        out_shape=jax.ShapeDtypeStruct((b, s, A_WIDTH), BF16),
        scratch_shapes=[pltpu.VMEM((s, HEAD_DIM), F32)] * 3 + [pltpu.VMEM((pad_rows, HEAD_DIM), BF16)] * 2
        + [pltpu.VMEM((s, HEAD_DIM), F32)] * 3
        + [pltpu.VMEM((s, kw), F32), pltpu.VMEM((s, kw), BF16), pltpu.VMEM((s, HEAD_DIM), F32)]
        + [pltpu.VMEM((3, A_QUERY_BLOCK, kw), F32)],
        compiler_params=_cparams(("parallel", "parallel")),
        name="dilated_mixer",
    )(y3, y3, v3, cos_full, sin_signed, q_gain.reshape(1, HEAD_DIM), k_gain.reshape(1, HEAD_DIM))


def _gqa_kernel(q_ref, k_ref, v_ref, cosq_ref, sinq_ref, cosk_ref, sink_ref, qg_ref, kg_ref, o_ref, kn, vaug, *, group, seq):
    half = HEAD_DIM // 4
    tile = 512

    @pl.when(pl.program_id(2) == 0)
    def _():
        def prep(i, carry):
            rows = pl.ds(pl.multiple_of(i * tile, tile), tile)
            kn[rows, :] = _norm_rope(k_ref[0, rows, :], kg_ref[...], cosk_ref[rows, :], sink_ref[rows, :],
                                     half).astype(kn.dtype)
            return carry

        lax.fori_loop(0, seq // tile, prep, 0)

        vaug[:, :HEAD_DIM] = v_ref[0]
        vaug[:, HEAD_DIM:] = jnp.ones((seq, HEAD_DIM), vaug.dtype)

    k = kn[...]
    v = vaug[...]
    c = (HEAD_DIM ** -0.5) * math.log2(math.e)
    for g in range(group):
        cols = slice(g * HEAD_DIM, (g + 1) * HEAD_DIM)
        q = _norm_rope(q_ref[0, :, cols], qg_ref[...], cosq_ref[...], sinq_ref[...], half).astype(BF16)
        s = lax.dot_general(q, k, (((1,), (1,)), ((), ())), preferred_element_type=F32)
        m = jnp.max(s, axis=-1, keepdims=True)
        p = jnp.exp2((s - m) * c)
        o = jnp.dot(p.astype(BF16), v, preferred_element_type=F32)
        o_ref[0, :, cols] = (o[:, :HEAD_DIM] / o[:, HEAD_DIM:]).astype(o_ref.dtype)


def gqa_attention(y3, v3, cos_full, sin_signed, q_gain, k_gain, tq=256):
    b, s, _ = y3.shape
    group = B_Q_HEADS // B_KV_HEADS
    gw = group * HEAD_DIM
    vec = pl.BlockSpec((1, HEAD_DIM), lambda bi, h, t: (0, 0))
    qtab = pl.BlockSpec((tq, HEAD_DIM), lambda bi, h, t: (t, 0))
    ktab = pl.BlockSpec((s, HEAD_DIM), lambda bi, h, t: (0, 0))
    return pl.pallas_call(
        functools.partial(_gqa_kernel, group=group, seq=s),
        grid=(b, B_KV_HEADS, s // tq),
        in_specs=[pl.BlockSpec((1, tq, gw), lambda bi, h, t: (bi, t, Y_BQ // gw + h)),
                  pl.BlockSpec((1, s, HEAD_DIM), lambda bi, h, t: (bi, 0, Y_BK // HEAD_DIM + h)),
                  pl.BlockSpec((1, s, HEAD_DIM), lambda bi, h, t: (bi, 0, V_BV // HEAD_DIM + h)),
                  qtab, qtab, ktab, ktab, vec, vec],
        out_specs=pl.BlockSpec((1, tq, gw), lambda bi, h, t: (bi, t, h)),
        out_shape=jax.ShapeDtypeStruct((b, s, B_Q_WIDTH), BF16),
        scratch_shapes=[pltpu.VMEM((s, HEAD_DIM), BF16), pltpu.VMEM((s, 2 * HEAD_DIM), BF16)],
        compiler_params=_cparams(("parallel", "parallel", "arbitrary")),
        name="gqa_attention",
    )(y3, y3, v3, cos_full, sin_signed, cos_full, sin_signed, q_gain.reshape(1, HEAD_DIM), k_gain.reshape(1, HEAD_DIM))


def _conv_kernel(xp_ref, xm_ref, xn_ref, w_ref, b_ref, qk_ref, kt_ref, buf, *, tile, ntiles):
    ti = pl.program_id(1)
    halo = 8
    buf[0:halo] = jnp.where(ti > 0, xp_ref[0], 0.0)
    buf[halo:halo + tile] = xm_ref[0]
    buf[halo + tile:] = jnp.where(ti < ntiles - 1, xn_ref[0], 0.0)
    pad = C_CONV // 2
    acc = w_ref[0:1, :] * buf[halo - pad:halo - pad + tile, :]
    for j in range(1, C_CONV):
        acc = acc + w_ref[j:j + 1, :] * buf[halo - pad + j:halo - pad + j + tile, :]
    acc = acc + b_ref[...]
    act = acc * jax.nn.sigmoid(acc)
    q = act[:, :C_QK_WIDTH]
    k = act[:, C_QK_WIDTH:] * (C_QK_DIM ** -0.5)
    qk_ref[0, :, :C_QK_WIDTH] = q.astype(qk_ref.dtype)
    qk_ref[0, :, C_QK_WIDTH:] = k.astype(qk_ref.dtype)
    for c in range(tile // MLSTM_CHUNK):
        kt_ref[0, c] = k[c * MLSTM_CHUNK:(c + 1) * MLSTM_CHUNK, :].T.astype(kt_ref.dtype)


def mlstm_conv(y3, conv_w, conv_b, tile=512):
    b, s, _ = y3.shape
    ntiles = s // tile
    w2 = 2 * C_QK_WIDTH
    cb = Y_CQK // w2
    per = tile // 8
    return pl.pallas_call(
        functools.partial(_conv_kernel, tile=tile, ntiles=ntiles),
        grid=(b, ntiles),
        in_specs=[pl.BlockSpec((1, 8, w2), lambda bi, t: (bi, jnp.maximum(t * per - 1, 0), cb)),
                  pl.BlockSpec((1, tile, w2), lambda bi, t: (bi, t, cb)),
                  pl.BlockSpec((1, 8, w2), lambda bi, t: (bi, jnp.minimum((t + 1) * per, s // 8 - 1), cb)),
                  pl.BlockSpec((C_CONV, w2), lambda bi, t: (0, 0)),
                  pl.BlockSpec((1, w2), lambda bi, t: (0, 0))],
        out_specs=[pl.BlockSpec((1, tile, w2), lambda bi, t: (bi, t, 0)),
                   pl.BlockSpec((1, tile // MLSTM_CHUNK, C_QK_WIDTH, MLSTM_CHUNK), lambda bi, t: (bi, t, 0, 0))],
        out_shape=[jax.ShapeDtypeStruct((b, s, w2), BF16),
                   jax.ShapeDtypeStruct((b, s // MLSTM_CHUNK, C_QK_WIDTH, MLSTM_CHUNK), BF16)],
        scratch_shapes=[pltpu.VMEM((tile + 16, w2), F32)],
        compiler_params=_cparams(("parallel", "parallel")),
        name="mlstm_conv",
    )(y3, y3, y3, conv_w, conv_b.reshape(1, w2))


def _gate_kernel(gi_ref, gf_ref, bi_ref, bf_ref, cb_ref, inter_ref, e_ref, rt_ref, wkt_ref, dec_ref,
                 b_s, r_s, dmax_s, btot_s, gmax_s, mpf_s, mpb_s, *, nc):
    lc = MLSTM_CHUNK
    t_in = lax.broadcasted_iota(jnp.int32, (lc, LANES), 0)
    fwd = lax.broadcasted_iota(jnp.int32, (lc, LANES), 1) % 2 == 0
    fwd_row = fwd[0:1]
    steps = [1 << i for i in range(lc.bit_length() - 1)]

    def chunk_rows(c):
        return pl.ds(pl.multiple_of(c * lc, lc), lc)

    def local(c, carry):
        rows = chunk_rows(c)
        ii = gi_ref[0, rows, :] + bi_ref[...]
        ff = gf_ref[0, rows, :] + bf_ref[...]
        lf = jnp.minimum(ff, 0.0) - jnp.log1p(jnp.exp(-jnp.abs(ff)))
        cs = lf
        rs = lf
        for k in steps:
            cs = cs + jnp.where(t_in >= k, pltpu.roll(cs, k, 0), 0.0)
            rs = rs + jnp.where(t_in + k < lc, pltpu.roll(rs, lc - k, 0), 0.0)
        b = jnp.where(fwd, cs, rs)
        btot = cs + rs - lf
        r = ii - b
        pm = r
        sm = r
        for k in steps:
            pm = jnp.where(t_in >= k, jnp.maximum(pm, pltpu.roll(pm, k, 0)), pm)
            sm = jnp.where(t_in + k < lc, jnp.maximum(sm, pltpu.roll(sm, lc - k, 0)), sm)
        b_s[rows, :] = b
        r_s[rows, :] = r
        dmax_s[rows, :] = b + jnp.where(fwd, pm, sm)
        btot_s[pl.ds(c, 1), :] = btot[0:1]
        gmax_s[pl.ds(c, 1), :] = (btot + jnp.maximum(pm, sm))[0:1]
        return carry

    lax.fori_loop(0, nc, local, 0)

    def scan(k, carry):
        mf, mb = carry
        mpf_s[pl.ds(k, 1), :] = mf
        mf = jnp.maximum(btot_s[pl.ds(k, 1), :] + mf, gmax_s[pl.ds(k, 1), :])
        kb = nc - 1 - k
        mpb_s[pl.ds(kb, 1), :] = mb
        mb = jnp.maximum(btot_s[pl.ds(kb, 1), :] + mb, gmax_s[pl.ds(kb, 1), :])
        return mf, mb

    zero = jnp.zeros((1, LANES), F32)
    lax.fori_loop(0, nc, scan, (zero, zero))

    def emit(c, carry):
        rows = chunk_rows(c)
        b = b_s[rows, :]
        r = r_s[rows, :]
        btot = btot_s[pl.ds(c, 1), :]
        mprev = jnp.where(fwd_row, mpf_s[pl.ds(c, 1), :], mpb_s[pl.ds(c, 1), :])
        mnew = jnp.maximum(btot + mprev, gmax_s[pl.ds(c, 1), :])
        dec_ref[0, pl.ds(c, 1), :] = jnp.exp(btot + mprev - mnew)
        m_inter = b + mprev
        m_t = jnp.maximum(m_inter, dmax_s[rows, :])
        cb_ref[0, rows, :] = b - m_t
        inter_ref[0, rows, :] = jnp.exp(m_inter - m_t)
        e_ref[0, rows, :] = jnp.exp(-m_t)
        wk = jnp.exp(btot + r - mnew)
        rt_ref[0, c] = r.T[:C_STREAMS, :]
        wkt_ref[0, c] = wk.T[:C_STREAMS, :]
        return carry

    lax.fori_loop(0, nc, emit, 0)


def mlstm_gates(y3, bias_i, bias_f):
    b, s, _ = y3.shape
    nc = s // MLSTM_CHUNK
    col = pl.BlockSpec((1, s, LANES), lambda bi: (bi, 0, 0))
    row = pl.BlockSpec((1, nc, C_STREAMS, MLSTM_CHUNK), lambda bi: (bi, 0, 0, 0))
    vec = pl.BlockSpec((1, LANES), lambda bi: (0, 0))
    return pl.pallas_call(
        functools.partial(_gate_kernel, nc=nc),
        grid=(b,),
        in_specs=[pl.BlockSpec((1, s, LANES), lambda bi: (bi, 0, Y_GI // LANES)),
                  pl.BlockSpec((1, s, LANES), lambda bi: (bi, 0, Y_GF // LANES)), vec, vec],
        out_specs=[col, col, col, row, row, pl.BlockSpec((1, nc, LANES), lambda bi: (bi, 0, 0))],
        out_shape=[jax.ShapeDtypeStruct((b, s, LANES), F32)] * 3
        + [jax.ShapeDtypeStruct((b, nc, C_STREAMS, MLSTM_CHUNK), F32)] * 2
        + [jax.ShapeDtypeStruct((b, nc, LANES), F32)],
        scratch_shapes=[pltpu.VMEM((s, LANES), F32)] * 3 + [pltpu.VMEM((nc, LANES), F32)] * 4,
        compiler_params=_cparams(("parallel",)),
        name="mlstm_gates",
    )(y3, y3, bias_i.reshape(1, LANES), bias_f.reshape(1, LANES))


def _mlstm_kernel(dec_ref, q_ref, kt_ref, v_ref, o_ref, g_ref, cb_ref, inter_ref, e_ref, rt_ref, wkt_ref,
                  y_ref, cbs, inters, es, hf, hb, ct, *, seq, nc):
    lc = MLSTM_CHUNK
    bi = pl.program_id(0)
    pair = pl.program_id(1)
    shift = (LANES - pair * 4) % LANES

    def align(c, carry):
        rows = pl.ds(pl.multiple_of(c * lc, lc), lc)
        cbs[rows, :] = pltpu.roll(cb_ref[0, rows, :], shift, 1)
        inters[rows, :] = pltpu.roll(inter_ref[0, rows, :], shift, 1)
        es[rows, :] = pltpu.roll(e_ref[0, rows, :], shift, 1)
        return carry

    lax.fori_loop(0, nc, align, 0)
    ct[...] = jnp.zeros_like(ct)
    lane = lax.broadcasted_iota(jnp.int32, (lc, LANES), 1)
    trow = lax.broadcasted_iota(jnp.int32, (lc, lc), 0)
    scol = lax.broadcasted_iota(jnp.int32, (lc, lc), 1)
    ones = jnp.ones((lc, C_V_DIM), BF16)

    def step(it, carry):
        for direction in range(2):
            c = it if direction == 0 else nc - 1 - it
            rows = pl.ds(pl.multiple_of(c * lc, lc), lc)
            mask = (scol <= trow) if direction == 0 else (scol >= trow)
            q2 = q_ref[0, rows, :]
            kt2 = kt_ref[0, c]
            for hh in range(2):
                sl = hh * 2 + direction
                stream = pair * 4 + sl
                dec = dec_ref[(bi * nc + c) * C_STREAMS + stream]
                qh = jnp.where((lane >= hh * C_QK_DIM) & (lane < (hh + 1) * C_QK_DIM), q2, jnp.zeros_like(q2))
                vaug = jnp.concatenate([v_ref[0, rows, hh * C_V_DIM:(hh + 1) * C_V_DIM], ones], axis=1)
                sc = jnp.dot(qh, kt2, preferred_element_type=F32)
                arg = cbs[rows, sl:sl + 1] + rt_ref[0, c, pl.ds(stream, 1), :]
                a = jnp.exp(jnp.where(mask, arg, NEG_INF)) * sc
                num_a = jnp.dot(a.astype(BF16), vaug, preferred_element_type=F32)
                ct_pair = ct[direction]
                num_c = jnp.dot(qh, ct_pair.astype(BF16), preferred_element_type=F32)
                num = num_a + inters[rows, sl:sl + 1] * num_c
                den = jnp.maximum(jnp.abs(num[:, C_V_DIM:]), es[rows, sl:sl + 1])
                h = num[:, :C_V_DIM] / den
                dst = hf if direction == 0 else hb
                dst[rows, hh * C_V_DIM:(hh + 1) * C_V_DIM] = h
                kw = (kt2[hh * C_QK_DIM:(hh + 1) * C_QK_DIM, :].astype(F32)
                      * wkt_ref[0, c, pl.ds(stream, 1), :]).astype(BF16)
                u = jnp.dot(kw, vaug, preferred_element_type=F32)
                d_rows = slice(hh * C_QK_DIM, (hh + 1) * C_QK_DIM)
                ct[direction, d_rows, :] = dec * ct[direction, d_rows, :] + u
        return carry

    lax.fori_loop(0, nc, step, 0, unroll=2)

    tile = 512

    def finish(i, carry):
        rows = pl.ds(pl.multiple_of(i * tile, tile), tile)
        for hh in range(2):
            cols = slice(hh * C_V_DIM, (hh + 1) * C_V_DIM)
            h = hf[rows, cols] + hb[rows, cols]
            h = h * lax.rsqrt(jnp.mean(h * h, axis=-1, keepdims=True) + EPS)
            h = h * g_ref[:, cols]
            y_ref[0, rows, cols] = (h * jax.nn.sigmoid(o_ref[0, rows, cols])).astype(y_ref.dtype)
        return carry

    lax.fori_loop(0, seq // tile, finish, 0)


def mlstm_scan(dec, qk, kt, v3, y3, out_g, cb, inter, e, rt, wkt):
    b, s, _ = qk.shape
    nc = s // MLSTM_CHUNK
    pw = 2 * C_V_DIM
    col = pl.BlockSpec((1, s, LANES), lambda bi, p, d: (bi, 0, 0))
    row = pl.BlockSpec((1, nc, C_STREAMS, MLSTM_CHUNK), lambda bi, p, d: (bi, 0, 0, 0))
    grid_spec = pltpu.PrefetchScalarGridSpec(
        num_scalar_prefetch=1,
        grid=(b, C_PAIRS),
        in_specs=[pl.BlockSpec((1, s, LANES), lambda bi, p, d: (bi, 0, p)),
                  pl.BlockSpec((1, nc, LANES, MLSTM_CHUNK), lambda bi, p, d: (bi, 0, p, 0)),
                  pl.BlockSpec((1, s, pw), lambda bi, p, d: (bi, 0, V_CV // pw + p)),
                  pl.BlockSpec((1, s, pw), lambda bi, p, d: (bi, 0, Y_CO // pw + p)),
                  pl.BlockSpec((1, pw), lambda bi, p, d: (0, p)),
                  col, col, col, row, row],
        out_specs=pl.BlockSpec((1, s, pw), lambda bi, p, d: (bi, 0, p)),
        scratch_shapes=[pltpu.VMEM((s, LANES), F32)] * 3 + [pltpu.VMEM((s, pw), F32)] * 2
        + [pltpu.VMEM((2, LANES, 2 * C_V_DIM), F32)],
    )
    return pl.pallas_call(
        functools.partial(_mlstm_kernel, seq=s, nc=nc),
        grid_spec=grid_spec,
        out_shape=jax.ShapeDtypeStruct((b, s, C_HEADS * C_V_DIM), BF16),
        compiler_params=_cparams(("parallel", "arbitrary")),
        name="mlstm_scan",
    )(dec, qk, kt, v3, y3, out_g.reshape(1, C_V_WIDTH), cb, inter, e, rt, wkt)


def _merge_kernel(ya_ref, yb_ref, yc_ref, ga_ref, gb_ref, gc_ref, wa_ref, wb_ref, wc_ref, o_ref):
    acc = jax.nn.sigmoid(ga_ref[...]) * jnp.dot(ya_ref[...], wa_ref[...], preferred_element_type=F32)
    acc = acc + jax.nn.sigmoid(gb_ref[...]) * jnp.dot(yb_ref[...], wb_ref[...], preferred_element_type=F32)
    acc = acc + jax.nn.sigmoid(gc_ref[...]) * jnp.dot(yc_ref[...], wc_ref[...], preferred_element_type=F32)
    o_ref[...] = acc.astype(o_ref.dtype)


def branch_merge(ya, yb, yc, y2, wa, wb, wc, tm=512, tn=1024):
    m = ya.shape[0]
    n = wa.shape[1]
    nj = n // tn

    def act(width):
        return pl.BlockSpec((tm, width), lambda j, i: (i, 0))

    def gate(branch):
        return pl.BlockSpec((tm, tn), lambda j, i: (i, Y_G // tn + branch * nj + j))

    def weight(width):
        return pl.BlockSpec((width, tn), lambda j, i: (0, j))

    return pl.pallas_call(
        _merge_kernel,
        grid=(nj, m // tm),
        in_specs=[act(A_WIDTH), act(B_Q_WIDTH), act(C_V_WIDTH), gate(0), gate(1), gate(2),
                  weight(A_WIDTH), weight(B_Q_WIDTH), weight(C_V_WIDTH)],
        out_specs=pl.BlockSpec((tm, tn), lambda j, i: (i, j)),
        out_shape=jax.ShapeDtypeStruct((m, n), BF16),
        compiler_params=_cparams(("parallel", "parallel")),
        name="branch_merge",
    )(ya, yb, yc, y2, y2, y2, wa, wb, wc)


def _rope_tables(pos, dim):
    inv_freq = ROPE_THETA ** (-jnp.arange(0, dim, 2, dtype=F32) / dim)
    ang = pos.astype(F32)[:, None] * inv_freq[None, :]
    return jnp.cos(ang), jnp.sin(ang)


def _stream_gate_columns(w_gate, b_gate):
    lanes = jnp.arange(C_STREAMS)
    head = (lanes // 4) * 2 + (lanes % 4) // 2
    direction = lanes % 2
    i_cols = direction * 2 * C_HEADS + head
    f_cols = i_cols + C_HEADS
    pad = LANES - C_STREAMS
    wi = jnp.pad(w_gate[:, i_cols], ((0, 0), (0, pad)))
    wf = jnp.pad(w_gate[:, f_cols], ((0, 0), (0, pad)))
    return wi, wf, jnp.pad(b_gate[i_cols], (0, pad)), jnp.pad(b_gate[f_cols], (0, pad))


def _layer(x2, shape, tables, norm1_g, w_in, a_q_gain, a_k_gain, b_q_gain, b_k_gain, c_conv_w, c_conv_b,
           c_gate_b, c_out_g, w_branch_a, w_branch_b, w_branch_c, w_out, norm2_g, w_up, w_down):
    bsz, seq = shape
    cos_a, sin_a, cos_b, sin_b = tables
    sizes = (A_WIDTH, A_WIDTH, A_WIDTH, B_Q_WIDTH, B_KV_WIDTH, B_KV_WIDTH, 2 * C_QK_WIDTH, C_V_WIDTH, C_V_WIDTH,
             4 * C_HEADS, N_BRANCHES * x2.shape[1])
    offs = [0]
    for sz in sizes:
        offs.append(offs[-1] + sz)
    seg = lambda i: w_in[:, offs[i]:offs[i + 1]]
    wi, wf, bias_i, bias_f = _stream_gate_columns(seg(9), c_gate_b)
    w_y = jnp.concatenate([seg(10), seg(6), seg(8), seg(3), seg(0), seg(1), seg(4), wi, wf], axis=1).astype(BF16)
    w_v = jnp.concatenate([seg(2), seg(5), seg(7)], axis=1).astype(BF16)

    y2 = norm_matmul(x2, norm1_g, w_y, out_dtype=F32)
    v2 = norm_matmul(x2, norm1_g, w_v, out_dtype=BF16)
    y3 = y2.reshape(bsz, seq, Y_COLS)
    v3 = v2.reshape(bsz, seq, V_COLS)

    ya = dilated_mixer(y3, v3, cos_a, sin_a, a_q_gain, a_k_gain).reshape(bsz * seq, A_WIDTH)
    yb = gqa_attention(y3, v3, cos_b, sin_b, b_q_gain, b_k_gain).reshape(bsz * seq, B_Q_WIDTH)

    qk_c, kt_c = mlstm_conv(y3, c_conv_w, c_conv_b)
    cb, inter, e, rt, wkt, dec = mlstm_gates(y3, bias_i, bias_f)
    dec_flat = dec[:, :, :C_STREAMS].reshape(-1)
    yc = mlstm_scan(dec_flat, qk_c, kt_c, v3, y3, c_out_g, cb, inter, e, rt, wkt).reshape(bsz * seq, C_V_WIDTH)

    merged = branch_merge(ya, yb, yc, y2, w_branch_a.astype(BF16), w_branch_b.astype(BF16), w_branch_c.astype(BF16))
    h = matmul(merged, w_out.astype(BF16), out_dtype=F32, res=x2)
    return ffn(h, norm2_g, w_up.astype(BF16), w_down.astype(BF16))


def kernel(x, norm1_g, w_in, a_q_gain, a_k_gain, b_q_gain, b_k_gain, c_conv_w, c_conv_b, c_gate_b, c_out_g, w_branch_a, w_branch_b, w_branch_c, w_out, norm2_g, w_up, w_down):
    bsz, seq, d_model = x.shape
    pos = jnp.arange(seq)
    cos1, sin1 = _rope_tables(pos, HEAD_DIM)
    cos_r, sin_r = _rope_tables(pos // GRID_W, HEAD_DIM // 2)
    cos_c, sin_c = _rope_tables(pos % GRID_W, HEAD_DIM // 2)
    tables = (jnp.concatenate([cos1, cos1], axis=1), jnp.concatenate([-sin1, sin1], axis=1),
              jnp.concatenate([cos_r, cos_r, cos_c, cos_c], axis=1),
              jnp.concatenate([-sin_r, sin_r, -sin_c, sin_c], axis=1))
    x2 = x.reshape(bsz * seq, d_model)
    for l in range(norm1_g.shape[0]):
        x2 = _layer(x2, (bsz, seq), tables, norm1_g[l], w_in[l], a_q_gain[l], a_k_gain[l], b_q_gain[l], b_k_gain[l],
                    c_conv_w[l], c_conv_b[l], c_gate_b[l], c_out_g[l], w_branch_a[l], w_branch_b[l], w_branch_c[l],
                    w_out[l], norm2_g[l], w_up[l], w_down[l])
    return x2.reshape(bsz, seq, d_model)
```

```python
import functools
import math

import jax
import jax.numpy as jnp
from jax import lax
from jax.experimental import pallas as pl
from jax.experimental.pallas import tpu as pltpu

F32 = jnp.float32
BF16 = jnp.bfloat16

HEAD_DIM = 128
GRID_W = 64
ROPE_THETA = 10000.0
EPS = 1e-6
NEG_INF = -1e30

A_HEADS = 6
A_PATTERNS = ((128, 1), (512, 4), (2048, 16))
A_REACH_BLOCK = 64
A_QUERY_BLOCK = 128
A_GROUP = 4
A_GROUP_MATMUL = 8
B_Q_HEADS = 8
B_KV_HEADS = 2
C_HEADS = 8
C_QK_DIM = 64
C_V_DIM = 128
C_CONV = 5
N_BRANCHES = 3

A_WIDTH = A_HEADS * HEAD_DIM
B_Q_WIDTH = B_Q_HEADS * HEAD_DIM
B_KV_WIDTH = B_KV_HEADS * HEAD_DIM
C_QK_WIDTH = C_HEADS * C_QK_DIM
C_V_WIDTH = C_HEADS * C_V_DIM

LANES = 128
MLSTM_CHUNK = 128
C_PAIRS = C_HEADS // 2
C_STREAMS = 2 * C_HEADS
VMEM_LIMIT = 60 * 1024 * 1024

Y_G = 0
Y_CQK = 6144
Y_CO = 7168
Y_BQ = 8192
Y_AQ = 9216
Y_AK = 9984
Y_BK = 10752
Y_GI = 11008
Y_GF = 11136
Y_COLS = 11264
V_AV = 0
V_BV = 768
V_CV = 1024
V_COLS = 2048


def _cparams(sem):
    return pltpu.CompilerParams(dimension_semantics=sem, vmem_limit_bytes=VMEM_LIMIT)


def _rmsnorm_kernel(x_ref, g_ref, o_ref):
    x = x_ref[...]
    r = lax.rsqrt(jnp.mean(x * x, axis=-1, keepdims=True) + EPS)
    o_ref[...] = ((x * r) * g_ref[...]).astype(o_ref.dtype)


def rmsnorm(x, g, tm=512):
    m, d = x.shape
    return pl.pallas_call(
        _rmsnorm_kernel,
        grid=(m // tm,),
        in_specs=[pl.BlockSpec((tm, d), lambda i: (i, 0)), pl.BlockSpec((1, d), lambda i: (0, 0))],
        out_specs=pl.BlockSpec((tm, d), lambda i: (i, 0)),
        out_shape=jax.ShapeDtypeStruct((m, d), BF16),
        compiler_params=_cparams(("parallel",)),
        name="rmsnorm",
    )(x, g.reshape(1, d))


def _matmul_kernel(*refs, nk, relu2, has_res):
    if has_res:
        a_ref, w_ref, r_ref, o_ref, acc_ref = refs
    else:
        a_ref, w_ref, o_ref, acc_ref = refs
        r_ref = None

    def finish(acc):
        if relu2:
            u = jnp.maximum(acc, 0.0)
            acc = u * u
        if has_res:
            acc = r_ref[...] + acc
        o_ref[...] = acc.astype(o_ref.dtype)

    prod = jnp.dot(a_ref[...], w_ref[...], preferred_element_type=F32)
    if nk == 1:
        finish(prod)
    else:
        k = pl.program_id(2)

        @pl.when(k == 0)
        def _():
            acc_ref[...] = prod

        @pl.when(k > 0)
        def _():
            acc_ref[...] += prod

        @pl.when(k == nk - 1)
        def _():
            finish(acc_ref[...])


def matmul(a, w, *, out_dtype, res=None, relu2=False, tm=1024, tn=1024, tk=2048):
    m, kdim = a.shape
    n = w.shape[1]
    tn = min(tn, n)
    tk = min(tk, kdim)
    nk = kdim // tk
    in_specs = [pl.BlockSpec((tm, tk), lambda i, j, k: (i, k)),
                pl.BlockSpec((tk, tn), lambda i, j, k: (k, j))]
    args = [a, w]
    if res is not None:
        in_specs.append(pl.BlockSpec((tm, tn), lambda i, j, k: (i, j)))
        args.append(res)
    return pl.pallas_call(
        functools.partial(_matmul_kernel, nk=nk, relu2=relu2, has_res=res is not None),
        grid=(m // tm, n // tn, nk),
        in_specs=in_specs,
        out_specs=pl.BlockSpec((tm, tn), lambda i, j, k: (i, j)),
        out_shape=jax.ShapeDtypeStruct((m, n), out_dtype),
        scratch_shapes=[pltpu.VMEM((tm, tn) if nk > 1 else (8, LANES), F32)],
        compiler_params=_cparams(("parallel", "parallel", "arbitrary")),
        name="matmul",
    )(*args)


def _norm_rope(x, gain, cos_full, sin_signed, half):
    y = (x * lax.rsqrt(jnp.mean(x * x, axis=-1, keepdims=True) + EPS)) * gain
    if 2 * half == LANES:
        partner = pltpu.roll(y, half, 1)
    else:
        lane = lax.broadcasted_iota(jnp.int32, y.shape, 1)
        partner = jnp.where(lane % (2 * half) < half, pltpu.roll(y, LANES - half, 1), pltpu.roll(y, half, 1))
    return y * cos_full + partner * sin_signed


def _dilated_kernel(q_ref, k_ref, v_ref, cos_ref, sin_ref, qg_ref, kg_ref, y_ref,
                    qn, kn, vn, kd, vd, acc, mrun, zrun, sbuf, pbuf, mblk, bias, *, seq):
    blk = A_REACH_BLOCK
    qb = A_QUERY_BLOCK
    kw = qb + 2 * blk
    grp = A_GROUP
    grp_mm = A_GROUP_MATMUL
    tile = 512
    scale = HEAD_DIM ** -0.5

    def prep(i, carry):
        rows = pl.ds(pl.multiple_of(i * tile, tile), tile)
        cos = cos_ref[rows, :]
        sin = sin_ref[rows, :]
        qn[rows, :] = _norm_rope(q_ref[0, rows, :], qg_ref[...], cos, sin, HEAD_DIM // 2)
        kn[rows, :] = _norm_rope(k_ref[0, rows, :], kg_ref[...], cos, sin, HEAD_DIM // 2)
        vn[rows, :] = v_ref[0, rows, :].astype(F32)
        return carry

    lax.fori_loop(0, seq // tile, prep, 0)
    zpad = jnp.zeros((blk, HEAD_DIM), BF16)
    kd[0:blk, :] = zpad
    kd[blk + seq:, :] = zpad
    vd[0:blk, :] = zpad
    vd[blk + seq:, :] = zpad

    t_io = lax.broadcasted_iota(jnp.int32, (qb, kw), 0)
    j_io = lax.broadcasted_iota(jnp.int32, (qb, kw), 1)
    in_reach = jnp.abs(j_io - blk - t_io) <= blk
    bias[0] = jnp.where(in_reach, 0.0, NEG_INF)
    bias[1] = jnp.where(in_reach & (j_io >= blk), 0.0, NEG_INF)
    bias[2] = jnp.where(in_reach & (j_io < qb + blk), 0.0, NEG_INF)
    nblk = seq // qb
    ones = jnp.ones((kw, HEAD_DIM), BF16)

    for pi, (_, d) in enumerate(sorted(A_PATTERNS, key=lambda wd: -wd[1])):
        sub_len = seq // d
        chunk = 256
        per_res = sub_len // chunk
        blk_per_res = sub_len // qb
        first = pi == 0

        def gather(idx, carry, d=d, sub_len=sub_len, per_res=per_res):
            r = idx // per_res
            j = idx % per_res
            if d == 1:
                src = pl.ds(pl.multiple_of(j * chunk, chunk), chunk)
            else:
                src = pl.ds(r + j * chunk * d, chunk, stride=d)
            dst = pl.ds(pl.multiple_of(blk + r * sub_len + j * chunk, blk), chunk)
            kd[dst, :] = kn[src, :].astype(BF16)
            vd[dst, :] = vn[src, :].astype(BF16)
            return carry

        lax.fori_loop(0, seq // chunk, gather, 0)

        def natural_rows(n, d=d, blk_per_res=blk_per_res):
            if d == 1:
                return pl.ds(pl.multiple_of(n * qb, qb), qb)
            return pl.ds(n // blk_per_res + (n % blk_per_res) * qb * d, qb, stride=d)

        def packed_rows(n, size=qb):
            return pl.ds(pl.multiple_of(n * qb, qb), size)

        def scores(it, carry, blk_per_res=blk_per_res):
            for g in range(grp_mm):
                n = it * grp_mm + g
                q = qn[natural_rows(n), :].astype(BF16)
                s = lax.dot_general(q, kd[packed_rows(n, kw), :], (((1,), (1,)), ((), ())),
                                    preferred_element_type=F32) * scale
                i = n % blk_per_res
                kind = jnp.where(i == 0, 1, jnp.where(i == blk_per_res - 1, 2, 0))
                sbuf[packed_rows(n), :] = s + bias[kind]
            return carry

        lax.fori_loop(0, nblk // grp_mm, scores, 0)

        def softmax(it, carry):
            for g in range(grp):
                rows = packed_rows(it * grp + g)
                s = sbuf[rows, :]
                m = jnp.max(s, axis=-1, keepdims=True)
                pbuf[rows, :] = jnp.exp(s - m).astype(BF16)
                mblk[rows, :] = jnp.broadcast_to(m, (qb, HEAD_DIM))
            return carry

        lax.fori_loop(0, nblk // grp, softmax, 0)

        def values(it, carry, first=first):
            for g in range(grp_mm):
                n = it * grp_mm + g
                vaug = jnp.concatenate([vd[packed_rows(n, kw), :], ones], axis=1)
                oz = jnp.dot(pbuf[packed_rows(n), :], vaug, preferred_element_type=F32)
                o = oz[:, :HEAD_DIM]
                z = oz[:, HEAD_DIM:]
                if first:
                    rw = natural_rows(n)
                    acc[rw, :] = o
                    zrun[rw, :] = z
                    mrun[rw, :] = mblk[packed_rows(n), :]
                else:
                    rw = natural_rows(n)
                    m_old = mrun[rw, :]
                    mb = mblk[packed_rows(n), :]
                    m_new = jnp.maximum(m_old, mb)
                    w_old = jnp.exp(m_old - m_new)
                    w_blk = jnp.exp(mb - m_new)
                    acc[rw, :] = w_old * acc[rw, :] + w_blk * o
                    zrun[rw, :] = w_old * zrun[rw, :] + w_blk * z
                    mrun[rw, :] = m_new
            return carry

        lax.fori_loop(0, nblk // grp_mm, values, 0)

    def finish(i, carry):
        rows = pl.ds(pl.multiple_of(i * tile, tile), tile)
        y_ref[0, rows, :] = (acc[rows, :] / zrun[rows, :]).astype(y_ref.dtype)
        return carry

    lax.fori_loop(0, seq // tile, finish, 0)


def dilated_mixer(y3, v3, cos_full, sin_signed, q_gain, k_gain):
    b, s, _ = y3.shape
    assert all(w // (2 * d) == A_REACH_BLOCK for w, d in A_PATTERNS)
    assert all((s // d) % 256 == 0 for _, d in A_PATTERNS) and (s // A_QUERY_BLOCK) % A_GROUP_MATMUL == 0
    kw = A_QUERY_BLOCK + 2 * A_REACH_BLOCK
    head = lambda off: pl.BlockSpec((1, s, HEAD_DIM), lambda bi, h: (bi, 0, off // HEAD_DIM + h))
    table = pl.BlockSpec((s, HEAD_DIM), lambda bi, h: (0, 0))
    vec = pl.BlockSpec((1, HEAD_DIM), lambda bi, h: (0, 0))
    pad_rows = s + 2 * A_REACH_BLOCK
    return pl.pallas_call(
        functools.partial(_dilated_kernel, seq=s),
        grid=(b, A_HEADS),
        in_specs=[head(Y_AQ), head(Y_AK), head(V_AV), table, table, vec, vec],
        out_specs=pl.BlockSpec((1, s, HEAD_DIM), lambda bi, h: (bi, 0, h)),
        out_shape=jax.ShapeDtypeStruct((b, s, A_WIDTH), BF16),
        scratch_shapes=[pltpu.VMEM((s, HEAD_DIM), F32)] * 3 + [pltpu.VMEM((pad_rows, HEAD_DIM), BF16)] * 2
        + [pltpu.VMEM((s, HEAD_DIM), F32)] * 3
        + [pltpu.VMEM((s, kw), F32), pltpu.VMEM((s, kw), BF16), pltpu.VMEM((s, HEAD_DIM), F32)]
        + [pltpu.VMEM((3, A_QUERY_BLOCK, kw), F32)],
        compiler_params=_cparams(("parallel", "parallel")),
        name="dilated_mixer",
    )(y3, y3, v3, cos_full, sin_signed, q_gain.reshape(1, HEAD_DIM), k_gain.reshape(1, HEAD_DIM))


def _gqa_kernel(q_ref, k_ref, v_ref, cosq_ref, sinq_ref, cosk_ref, sink_ref, qg_ref, kg_ref, o_ref, kn, vaug, *, group, seq):
    half = HEAD_DIM // 4
    tile = 512

    @pl.when(pl.program_id(2) == 0)
    def _():
        def prep(i, carry):
            rows = pl.ds(pl.multiple_of(i * tile, tile), tile)
            kn[rows, :] = _norm_rope(k_ref[0, rows, :], kg_ref[...], cosk_ref[rows, :], sink_ref[rows, :],
                                     half).astype(kn.dtype)
            return carry

        lax.fori_loop(0, seq // tile, prep, 0)

        vaug[:, :HEAD_DIM] = v_ref[0]
        vaug[:, HEAD_DIM:] = jnp.ones((seq, HEAD_DIM), vaug.dtype)

    k = kn[...]
    v = vaug[...]
    c = (HEAD_DIM ** -0.5) * math.log2(math.e)
    for g in range(group):
        cols = slice(g * HEAD_DIM, (g + 1) * HEAD_DIM)
        q = _norm_rope(q_ref[0, :, cols], qg_ref[...], cosq_ref[...], sinq_ref[...], half).astype(BF16)
        s = lax.dot_general(q, k, (((1,), (1,)), ((), ())), preferred_element_type=F32)
        m = jnp.max(s, axis=-1, keepdims=True)
        p = jnp.exp2((s - m) * c)
        o = jnp.dot(p.astype(BF16), v, preferred_element_type=F32)
        o_ref[0, :, cols] = (o[:, :HEAD_DIM] / o[:, HEAD_DIM:]).astype(o_ref.dtype)


def gqa_attention(y3, v3, cos_full, sin_signed, q_gain, k_gain, tq=256):
    b, s, _ = y3.shape
    group = B_Q_HEADS // B_KV_HEADS
    gw = group * HEAD_DIM
    vec = pl.BlockSpec((1, HEAD_DIM), lambda bi, h, t: (0, 0))
    qtab = pl.BlockSpec((tq, HEAD_DIM), lambda bi, h, t: (t, 0))
    ktab = pl.BlockSpec((s, HEAD_DIM), lambda bi, h, t: (0, 0))
    return pl.pallas_call(
        functools.partial(_gqa_kernel, group=group, seq=s),
        grid=(b, B_KV_HEADS, s // tq),
        in_specs=[pl.BlockSpec((1, tq, gw), lambda bi, h, t: (bi, t, Y_BQ // gw + h)),
                  pl.BlockSpec((1, s, HEAD_DIM), lambda bi, h, t: (bi, 0, Y_BK // HEAD_DIM + h)),
                  pl.BlockSpec((1, s, HEAD_DIM), lambda bi, h, t: (bi, 0, V_BV // HEAD_DIM + h)),
                  qtab, qtab, ktab, ktab, vec, vec],
        out_specs=pl.BlockSpec((1, tq, gw), lambda bi, h, t: (bi, t, h)),
        out_shape=jax.ShapeDtypeStruct((b, s, B_Q_WIDTH), BF16),
        scratch_shapes=[pltpu.VMEM((s, HEAD_DIM), BF16), pltpu.VMEM((s, 2 * HEAD_DIM), BF16)],
        compiler_params=_cparams(("parallel", "parallel", "arbitrary")),
        name="gqa_attention",
    )(y3, y3, v3, cos_full, sin_signed, cos_full, sin_signed, q_gain.reshape(1, HEAD_DIM), k_gain.reshape(1, HEAD_DIM))


def _conv_kernel(xp_ref, xm_ref, xn_ref, w_ref, b_ref, qk_ref, kt_ref, buf, *, tile, ntiles):
    ti = pl.program_id(1)
    halo = 8
    buf[0:halo] = jnp.where(ti > 0, xp_ref[0], 0.0)
    buf[halo:halo + tile] = xm_ref[0]
    buf[halo + tile:] = jnp.where(ti < ntiles - 1, xn_ref[0], 0.0)
    pad = C_CONV // 2
    acc = w_ref[0:1, :] * buf[halo - pad:halo - pad + tile, :]
    for j in range(1, C_CONV):
        acc = acc + w_ref[j:j + 1, :] * buf[halo - pad + j:halo - pad + j + tile, :]
    acc = acc + b_ref[...]
    act = acc * jax.nn.sigmoid(acc)
    q = act[:, :C_QK_WIDTH]
    k = act[:, C_QK_WIDTH:] * (C_QK_DIM ** -0.5)
    qk_ref[0, :, :C_QK_WIDTH] = q.astype(qk_ref.dtype)
    qk_ref[0, :, C_QK_WIDTH:] = k.astype(qk_ref.dtype)
    for c in range(tile // MLSTM_CHUNK):
        kt_ref[0, c] = k[c * MLSTM_CHUNK:(c + 1) * MLSTM_CHUNK, :].T.astype(kt_ref.dtype)


def mlstm_conv(y3, conv_w, conv_b, tile=512):
    b, s, _ = y3.shape
    ntiles = s // tile
    w2 = 2 * C_QK_WIDTH
    cb = Y_CQK // w2
    per = tile // 8
    return pl.pallas_call(
        functools.partial(_conv_kernel, tile=tile, ntiles=ntiles),
        grid=(b, ntiles),
        in_specs=[pl.BlockSpec((1, 8, w2), lambda bi, t: (bi, jnp.maximum(t * per - 1, 0), cb)),
                  pl.BlockSpec((1, tile, w2), lambda bi, t: (bi, t, cb)),
                  pl.BlockSpec((1, 8, w2), lambda bi, t: (bi, jnp.minimum((t + 1) * per, s // 8 - 1), cb)),
                  pl.BlockSpec((C_CONV, w2), lambda bi, t: (0, 0)),
                  pl.BlockSpec((1, w2), lambda bi, t: (0, 0))],
        out_specs=[pl.BlockSpec((1, tile, w2), lambda bi, t: (bi, t, 0)),
                   pl.BlockSpec((1, tile // MLSTM_CHUNK, C_QK_WIDTH, MLSTM_CHUNK), lambda bi, t: (bi, t, 0, 0))],
        out_shape=[jax.ShapeDtypeStruct((b, s, w2), BF16),
                   jax.ShapeDtypeStruct((b, s // MLSTM_CHUNK, C_QK_WIDTH, MLSTM_CHUNK), BF16)],
        scratch_shapes=[pltpu.VMEM((tile + 16, w2), F32)],
        compiler_params=_cparams(("parallel", "parallel")),
        name="mlstm_conv",
    )(y3, y3, y3, conv_w, conv_b.reshape(1, w2))


def _gate_kernel(gi_ref, gf_ref, bi_ref, bf_ref, cols_ref, rt_ref, wkt_ref, dec_ref,
                 b_s, r_s, dmax_s, btot_s, gmax_s, mpf_s, mpb_s, *, nc):
    lc = MLSTM_CHUNK
    t_in = lax.broadcasted_iota(jnp.int32, (lc, LANES), 0)
    lane = lax.broadcasted_iota(jnp.int32, (lc, LANES), 1)
    fwd = lane % 2 == 0
    fwd_row = fwd[0:1]
    steps = [1 << i for i in range(lc.bit_length() - 1)]

    def chunk_rows(c):
        return pl.ds(pl.multiple_of(c * lc, lc), lc)

    def local(c, carry):
        rows = chunk_rows(c)
        ii = gi_ref[0, rows, :] + bi_ref[...]
        ff = gf_ref[0, rows, :] + bf_ref[...]
        lf = jnp.minimum(ff, 0.0) - jnp.log1p(jnp.exp(-jnp.abs(ff)))
        cs = lf
        rs = lf
        for k in steps:
            cs = cs + jnp.where(t_in >= k, pltpu.roll(cs, k, 0), 0.0)
            rs = rs + jnp.where(t_in + k < lc, pltpu.roll(rs, lc - k, 0), 0.0)
        b = jnp.where(fwd, cs, rs)
        btot = cs + rs - lf
        r = ii - b
        pm = r
        sm = r
        for k in steps:
            pm = jnp.where(t_in >= k, jnp.maximum(pm, pltpu.roll(pm, k, 0)), pm)
            sm = jnp.where(t_in + k < lc, jnp.maximum(sm, pltpu.roll(sm, lc - k, 0)), sm)
        b_s[rows, :] = b
        r_s[rows, :] = r
        dmax_s[rows, :] = b + jnp.where(fwd, pm, sm)
        btot_s[pl.ds(c, 1), :] = btot[0:1]
        gmax_s[pl.ds(c, 1), :] = (btot + jnp.maximum(pm, sm))[0:1]
        return carry

    lax.fori_loop(0, nc, local, 0)

    def scan(k, carry):
        mf, mb = carry
        mpf_s[pl.ds(k, 1), :] = mf
        mf = jnp.maximum(btot_s[pl.ds(k, 1), :] + mf, gmax_s[pl.ds(k, 1), :])
        kb = nc - 1 - k
        mpb_s[pl.ds(kb, 1), :] = mb
        mb = jnp.maximum(btot_s[pl.ds(kb, 1), :] + mb, gmax_s[pl.ds(kb, 1), :])
        return mf, mb

    zero = jnp.zeros((1, LANES), F32)
    lax.fori_loop(0, nc, scan, (zero, zero))

    def emit(c, carry):
        rows = chunk_rows(c)
        b = b_s[rows, :]
        r = r_s[rows, :]
        btot = btot_s[pl.ds(c, 1), :]
        mprev = jnp.where(fwd_row, mpf_s[pl.ds(c, 1), :], mpb_s[pl.ds(c, 1), :])
        mnew = jnp.maximum(btot + mprev, gmax_s[pl.ds(c, 1), :])
        dec_ref[0, pl.ds(c, 1), :] = jnp.exp(btot + mprev - mnew)
        m_inter = b + mprev
        m_t = jnp.maximum(m_inter, dmax_s[rows, :])
        cb = b - m_t
        inter = jnp.exp(m_inter - m_t)
        e = jnp.exp(-m_t)
        for p in range(C_PAIRS):
            cols_ref[0, p, rows, :] = jnp.where(
                lane < 4, pltpu.roll(cb, (LANES - 4 * p) % LANES, 1),
                jnp.where(lane < 8, pltpu.roll(inter, (LANES + 4 - 4 * p) % LANES, 1),
                          pltpu.roll(e, (LANES + 8 - 4 * p) % LANES, 1)))
        wk = jnp.exp(btot + r - mnew)
        rt_ref[0, c] = r.T[:C_STREAMS, :]
        wkt_ref[0, c] = wk.T[:C_STREAMS, :]
        return carry

    lax.fori_loop(0, nc, emit, 0)


def mlstm_gates(y3, bias_i, bias_f):
    b, s, _ = y3.shape
    nc = s // MLSTM_CHUNK
    row = pl.BlockSpec((1, nc, C_STREAMS, MLSTM_CHUNK), lambda bi: (bi, 0, 0, 0))
    vec = pl.BlockSpec((1, LANES), lambda bi: (0, 0))
    return pl.pallas_call(
        functools.partial(_gate_kernel, nc=nc),
        grid=(b,),
        in_specs=[pl.BlockSpec((1, s, LANES), lambda bi: (bi, 0, Y_GI // LANES)),
                  pl.BlockSpec((1, s, LANES), lambda bi: (bi, 0, Y_GF // LANES)), vec, vec],
        out_specs=[pl.BlockSpec((1, C_PAIRS, s, LANES), lambda bi: (bi, 0, 0, 0)), row, row,
                   pl.BlockSpec((1, nc, LANES), lambda bi: (bi, 0, 0))],
        out_shape=[jax.ShapeDtypeStruct((b, C_PAIRS, s, LANES), F32)]
        + [jax.ShapeDtypeStruct((b, nc, C_STREAMS, MLSTM_CHUNK), F32)] * 2
        + [jax.ShapeDtypeStruct((b, nc, LANES), F32)],
        scratch_shapes=[pltpu.VMEM((s, LANES), F32)] * 3 + [pltpu.VMEM((nc, LANES), F32)] * 4,
        compiler_params=_cparams(("parallel",)),
        name="mlstm_gates",
    )(y3, y3, bias_i.reshape(1, LANES), bias_f.reshape(1, LANES))


def _mlstm_kernel(dec_ref, q_ref, kt_ref, v_ref, o_ref, g_ref, cols_ref, rt_ref, wkt_ref,
                  y_ref, abuf, kwbuf, hsum, ct, *, seq, nc):
    lc = MLSTM_CHUNK
    bi = pl.program_id(0)
    pair = pl.program_id(1)
    lane = lax.broadcasted_iota(jnp.int32, (lc, LANES), 1)
    trow = lax.broadcasted_iota(jnp.int32, (lc, lc), 0)
    scol = lax.broadcasted_iota(jnp.int32, (lc, lc), 1)
    ones = jnp.ones((lc, C_V_DIM), BF16)

    def head_q(q2, hh):
        return jnp.where((lane >= hh * C_QK_DIM) & (lane < (hh + 1) * C_QK_DIM), q2, jnp.zeros_like(q2))

    def intra(c, carry):
        rows = pl.ds(pl.multiple_of(c * lc, lc), lc)
        q2 = q_ref[0, rows, :]
        kt2 = kt_ref[0, c]
        for hh in range(2):
            sc = jnp.dot(head_q(q2, hh), kt2, preferred_element_type=F32)
            kth = kt2[hh * C_QK_DIM:(hh + 1) * C_QK_DIM, :].astype(F32)
            for direction in range(2):
                sl = hh * 2 + direction
                stream = pair * 4 + sl
                mask = (scol <= trow) if direction == 0 else (scol >= trow)
                arg = cols_ref[0, 0, rows, sl:sl + 1] + rt_ref[0, c, pl.ds(stream, 1), :]
                abuf[sl, c] = (jnp.exp(jnp.where(mask, arg, NEG_INF)) * sc).astype(BF16)
                kwbuf[sl, c] = (kth * wkt_ref[0, c, pl.ds(stream, 1), :]).astype(BF16)
        return carry

    lax.fori_loop(0, nc, intra, 0, unroll=2)

    ct[...] = jnp.zeros_like(ct)
    hsum[...] = jnp.zeros_like(hsum)

    def step(it, carry):
        for direction in range(2):
            c = it if direction == 0 else nc - 1 - it
            rows = pl.ds(pl.multiple_of(c * lc, lc), lc)
            q2 = q_ref[0, rows, :]
            ct_pair = ct[direction].astype(BF16)
            for hh in range(2):
                sl = hh * 2 + direction
                dec = dec_ref[(bi * nc + c) * C_STREAMS + pair * 4 + sl]
                vaug = jnp.concatenate([v_ref[0, rows, hh * C_V_DIM:(hh + 1) * C_V_DIM], ones], axis=1)
                num_a = jnp.dot(abuf[sl, c], vaug, preferred_element_type=F32)
                num_c = jnp.dot(head_q(q2, hh), ct_pair, preferred_element_type=F32)
                num = num_a + cols_ref[0, 0, rows, 4 + sl:5 + sl] * num_c
                den = jnp.maximum(jnp.abs(num[:, C_V_DIM:]), cols_ref[0, 0, rows, 8 + sl:9 + sl])
                hsum[rows, hh * C_V_DIM:(hh + 1) * C_V_DIM] += num[:, :C_V_DIM] / den
                u = jnp.dot(kwbuf[sl, c], vaug, preferred_element_type=F32)
                d_rows = slice(hh * C_QK_DIM, (hh + 1) * C_QK_DIM)
                ct[direction, d_rows, :] = dec * ct[direction, d_rows, :] + u
        return carry

    lax.fori_loop(0, nc, step, 0, unroll=2)

    tile = 512

    def finish(i, carry):
        rows = pl.ds(pl.multiple_of(i * tile, tile), tile)
        for hh in range(2):
            cols = slice(hh * C_V_DIM, (hh + 1) * C_V_DIM)
            h = hsum[rows, cols]
            h = h * lax.rsqrt(jnp.mean(h * h, axis=-1, keepdims=True) + EPS)
            h = h * g_ref[:, cols]
            y_ref[0, rows, cols] = (h * jax.nn.sigmoid(o_ref[0, rows, cols])).astype(y_ref.dtype)
        return carry

    lax.fori_loop(0, seq // tile, finish, 0)


def mlstm_scan(dec, qk, kt, v3, y3, out_g, cols, rt, wkt):
    b, s, _ = qk.shape
    nc = s // MLSTM_CHUNK
    pw = 2 * C_V_DIM
    col = pl.BlockSpec((1, 1, s, LANES), lambda bi, p, d: (bi, p, 0, 0))
    row = pl.BlockSpec((1, nc, C_STREAMS, MLSTM_CHUNK), lambda bi, p, d: (bi, 0, 0, 0))
    grid_spec = pltpu.PrefetchScalarGridSpec(
        num_scalar_prefetch=1,
        grid=(b, C_PAIRS),
        in_specs=[pl.BlockSpec((1, s, LANES), lambda bi, p, d: (bi, 0, p)),
                  pl.BlockSpec((1, nc, LANES, MLSTM_CHUNK), lambda bi, p, d: (bi, 0, p, 0)),
                  pl.BlockSpec((1, s, pw), lambda bi, p, d: (bi, 0, V_CV // pw + p)),
                  pl.BlockSpec((1, s, pw), lambda bi, p, d: (bi, 0, Y_CO // pw + p)),
                  pl.BlockSpec((1, pw), lambda bi, p, d: (0, p)),
                  col, row, row],
        out_specs=pl.BlockSpec((1, s, pw), lambda bi, p, d: (bi, 0, p)),
        scratch_shapes=[pltpu.VMEM((4, nc, MLSTM_CHUNK, MLSTM_CHUNK), BF16),
                        pltpu.VMEM((4, nc, C_QK_DIM, MLSTM_CHUNK), BF16),
                        pltpu.VMEM((s, pw), F32),
                        pltpu.VMEM((2, LANES, 2 * C_V_DIM), F32)],
    )
    return pl.pallas_call(
        functools.partial(_mlstm_kernel, seq=s, nc=nc),
        grid_spec=grid_spec,
        out_shape=jax.ShapeDtypeStruct((b, s, C_HEADS * C_V_DIM), BF16),
        compiler_params=_cparams(("parallel", "arbitrary")),
        name="mlstm_scan",
    )(dec, qk, kt, v3, y3, out_g.reshape(1, C_V_WIDTH), cols, rt, wkt)


def _merge_kernel(ya_ref, yb_ref, yc_ref, ga_ref, gb_ref, gc_ref, wa_ref, wb_ref, wc_ref, o_ref):
    acc = jax.nn.sigmoid(ga_ref[...]) * jnp.dot(ya_ref[...], wa_ref[...], preferred_element_type=F32)
    acc = acc + jax.nn.sigmoid(gb_ref[...]) * jnp.dot(yb_ref[...], wb_ref[...], preferred_element_type=F32)
    acc = acc + jax.nn.sigmoid(gc_ref[...]) * jnp.dot(yc_ref[...], wc_ref[...], preferred_element_type=F32)
    o_ref[...] = acc.astype(o_ref.dtype)


def branch_merge(ya, yb, yc, y2, wa, wb, wc, tm=512, tn=1024):
    m = ya.shape[0]
    n = wa.shape[1]
    nj = n // tn

    def act(width):
        return pl.BlockSpec((tm, width), lambda j, i: (i, 0))

    def gate(branch):
        return pl.BlockSpec((tm, tn), lambda j, i: (i, Y_G // tn + branch * nj + j))

    def weight(width):
        return pl.BlockSpec((width, tn), lambda j, i: (0, j))

    return pl.pallas_call(
        _merge_kernel,
        grid=(nj, m // tm),
        in_specs=[act(A_WIDTH), act(B_Q_WIDTH), act(C_V_WIDTH), gate(0), gate(1), gate(2),
                  weight(A_WIDTH), weight(B_Q_WIDTH), weight(C_V_WIDTH)],
        out_specs=pl.BlockSpec((tm, tn), lambda j, i: (i, j)),
        out_shape=jax.ShapeDtypeStruct((m, n), BF16),
        compiler_params=_cparams(("parallel", "parallel")),
        name="branch_merge",
    )(ya, yb, yc, y2, y2, y2, wa, wb, wc)


def _rope_tables(pos, dim):
    inv_freq = ROPE_THETA ** (-jnp.arange(0, dim, 2, dtype=F32) / dim)
    ang = pos.astype(F32)[:, None] * inv_freq[None, :]
    return jnp.cos(ang), jnp.sin(ang)


def _stream_gate_columns(w_gate, b_gate):
    lanes = jnp.arange(C_STREAMS)
    head = (lanes // 4) * 2 + (lanes % 4) // 2
    direction = lanes % 2
    i_cols = direction * 2 * C_HEADS + head
    f_cols = i_cols + C_HEADS
    pad = LANES - C_STREAMS
    wi = jnp.pad(w_gate[:, i_cols], ((0, 0), (0, pad)))
    wf = jnp.pad(w_gate[:, f_cols], ((0, 0), (0, pad)))
    return wi, wf, jnp.pad(b_gate[i_cols], (0, pad)), jnp.pad(b_gate[f_cols], (0, pad))


def _layer(x2, shape, tables, norm1_g, w_in, a_q_gain, a_k_gain, b_q_gain, b_k_gain, c_conv_w, c_conv_b,
           c_gate_b, c_out_g, w_branch_a, w_branch_b, w_branch_c, w_out, norm2_g, w_up, w_down):
    bsz, seq = shape
    cos_a, sin_a, cos_b, sin_b = tables
    sizes = (A_WIDTH, A_WIDTH, A_WIDTH, B_Q_WIDTH, B_KV_WIDTH, B_KV_WIDTH, 2 * C_QK_WIDTH, C_V_WIDTH, C_V_WIDTH,
             4 * C_HEADS, N_BRANCHES * x2.shape[1])
    offs = [0]
    for sz in sizes:
        offs.append(offs[-1] + sz)
    seg = lambda i: w_in[:, offs[i]:offs[i + 1]]
    wi, wf, bias_i, bias_f = _stream_gate_columns(seg(9), c_gate_b)
    w_y = jnp.concatenate([seg(10), seg(6), seg(8), seg(3), seg(0), seg(1), seg(4), wi, wf], axis=1).astype(BF16)
    w_v = jnp.concatenate([seg(2), seg(5), seg(7)], axis=1).astype(BF16)

    xn = rmsnorm(x2, norm1_g)
    y2 = matmul(xn, w_y, out_dtype=F32, tm=2048)
    v2 = matmul(xn, w_v, out_dtype=BF16, tm=2048)
    y3 = y2.reshape(bsz, seq, Y_COLS)
    v3 = v2.reshape(bsz, seq, V_COLS)

    ya = dilated_mixer(y3, v3, cos_a, sin_a, a_q_gain, a_k_gain).reshape(bsz * seq, A_WIDTH)
    yb = gqa_attention(y3, v3, cos_b, sin_b, b_q_gain, b_k_gain).reshape(bsz * seq, B_Q_WIDTH)

    qk_c, kt_c = mlstm_conv(y3, c_conv_w, c_conv_b)
    cols, rt, wkt, dec = mlstm_gates(y3, bias_i, bias_f)
    dec_flat = dec[:, :, :C_STREAMS].reshape(-1)
    yc = mlstm_scan(dec_flat, qk_c, kt_c, v3, y3, c_out_g, cols, rt, wkt).reshape(bsz * seq, C_V_WIDTH)

    merged = branch_merge(ya, yb, yc, y2, w_branch_a.astype(BF16), w_branch_b.astype(BF16), w_branch_c.astype(BF16))
    h = matmul(merged, w_out.astype(BF16), out_dtype=F32, res=x2)
    hn = rmsnorm(h, norm2_g)
    u2 = matmul(hn, w_up.astype(BF16), out_dtype=BF16, relu2=True, tm=2048)
    return matmul(u2, w_down.astype(BF16), out_dtype=F32, res=h)


def kernel(x, norm1_g, w_in, a_q_gain, a_k_gain, b_q_gain, b_k_gain, c_conv_w, c_conv_b, c_gate_b, c_out_g, w_branch_a, w_branch_b, w_branch_c, w_out, norm2_g, w_up, w_down):
    bsz, seq, d_model = x.shape
    pos = jnp.arange(seq)
    cos1, sin1 = _rope_tables(pos, HEAD_DIM)
    cos_r, sin_r = _rope_tables(pos // GRID_W, HEAD_DIM // 2)
    cos_c, sin_c = _rope_tables(pos % GRID_W, HEAD_DIM // 2)
    tables = (jnp.concatenate([cos1, cos1], axis=1), jnp.concatenate([-sin1, sin1], axis=1),
              jnp.concatenate([cos_r, cos_r, cos_c, cos_c], axis=1),
              jnp.concatenate([-sin_r, sin_r, -sin_c, sin_c], axis=1))
    x2 = x.reshape(bsz * seq, d_model)
    for l in range(norm1_g.shape[0]):
        x2 = _layer(x2, (bsz, seq), tables, norm1_g[l], w_in[l], a_q_gain[l], a_k_gain[l], b_q_gain[l], b_k_gain[l],
                    c_conv_w[l], c_conv_b[l], c_gate_b[l], c_out_g[l], w_branch_a[l], w_branch_b[l], w_branch_c[l],
                    w_out[l], norm2_g[l], w_up[l], w_down[l])
    return x2.reshape(bsz, seq, d_model)
```

```python
import functools
import math

import jax
import jax.numpy as jnp
from jax import lax
from jax.experimental import pallas as pl
from jax.experimental.pallas import tpu as pltpu

F32 = jnp.float32
BF16 = jnp.bfloat16

HEAD_DIM = 128
GRID_W = 64
ROPE_THETA = 10000.0
EPS = 1e-6
NEG_INF = -1e30

A_HEADS = 6
A_PATTERNS = ((128, 1), (512, 4), (2048, 16))
A_REACH_BLOCK = 64
A_QUERY_BLOCK = 128
A_GROUP = 4
A_GROUP_MATMUL = 8
B_Q_HEADS = 8
B_KV_HEADS = 2
C_HEADS = 8
C_QK_DIM = 64
C_V_DIM = 128
C_CONV = 5
N_BRANCHES = 3

A_WIDTH = A_HEADS * HEAD_DIM
B_Q_WIDTH = B_Q_HEADS * HEAD_DIM
B_KV_WIDTH = B_KV_HEADS * HEAD_DIM
C_QK_WIDTH = C_HEADS * C_QK_DIM
C_V_WIDTH = C_HEADS * C_V_DIM

LANES = 128
MLSTM_CHUNK = 128
C_PAIRS = C_HEADS // 2
C_STREAMS = 2 * C_HEADS
VMEM_LIMIT = 60 * 1024 * 1024

Y_G = 0
Y_CQK = 6144
Y_CO = 7168
Y_BQ = 8192
Y_AQ = 9216
Y_AK = 9984
Y_BK = 10752
Y_GI = 11008
Y_GF = 11136
Y_COLS = 11264
V_AV = 0
V_BV = 768
V_CV = 1024
V_COLS = 2048


def _cparams(sem):
    return pltpu.CompilerParams(dimension_semantics=sem, vmem_limit_bytes=VMEM_LIMIT)


def _rmsnorm_kernel(x_ref, g_ref, o_ref):
    x = x_ref[...]
    r = lax.rsqrt(jnp.mean(x * x, axis=-1, keepdims=True) + EPS)
    o_ref[...] = ((x * r) * g_ref[...]).astype(o_ref.dtype)


def rmsnorm(x, g, tm=512):
    m, d = x.shape
    return pl.pallas_call(
        _rmsnorm_kernel,
        grid=(m // tm,),
        in_specs=[pl.BlockSpec((tm, d), lambda i: (i, 0)), pl.BlockSpec((1, d), lambda i: (0, 0))],
        out_specs=pl.BlockSpec((tm, d), lambda i: (i, 0)),
        out_shape=jax.ShapeDtypeStruct((m, d), BF16),
        compiler_params=_cparams(("parallel",)),
        name="rmsnorm",
    )(x, g.reshape(1, d))


def _matmul_kernel(*refs, nk, relu2, has_res):
    if has_res:
        a_ref, w_ref, r_ref, o_ref = refs
    else:
        a_ref, w_ref, o_ref = refs
        r_ref = None

    prod = jnp.dot(a_ref[...], w_ref[...], preferred_element_type=F32)
    if relu2:
        u = jnp.maximum(prod, 0.0)
        prod = u * u
    if nk == 1:
        o_ref[...] = (prod if r_ref is None else r_ref[...] + prod).astype(o_ref.dtype)
    else:
        k = pl.program_id(2)

        @pl.when(k == 0)
        def _():
            o_ref[...] = prod if r_ref is None else r_ref[...] + prod

        @pl.when(k > 0)
        def _():
            o_ref[...] += prod


def matmul(a, w, *, out_dtype, res=None, relu2=False, tm=1024, tn=1024, tk=2048):
    m, kdim = a.shape
    n = w.shape[1]
    tn = min(tn, n)
    tk = min(tk, kdim)
    nk = kdim // tk
    assert nk == 1 or (out_dtype == F32 and not relu2)
    in_specs = [pl.BlockSpec((tm, tk), lambda i, j, k: (i, k)),
                pl.BlockSpec((tk, tn), lambda i, j, k: (k, j))]
    args = [a, w]
    if res is not None:
        in_specs.append(pl.BlockSpec((tm, tn), lambda i, j, k: (i, j)))
        args.append(res)
    return pl.pallas_call(
        functools.partial(_matmul_kernel, nk=nk, relu2=relu2, has_res=res is not None),
        grid=(m // tm, n // tn, nk),
        in_specs=in_specs,
        out_specs=pl.BlockSpec((tm, tn), lambda i, j, k: (i, j)),
        out_shape=jax.ShapeDtypeStruct((m, n), out_dtype),
        compiler_params=_cparams(("parallel", "parallel", "arbitrary")),
        name="matmul",
    )(*args)


def _out_norm_kernel(a_ref, w_ref, r_ref, g_ref, h_ref, hn_ref):
    h = r_ref[...] + jnp.dot(a_ref[...], w_ref[...], preferred_element_type=F32)
    h_ref[...] = h
    r = lax.rsqrt(jnp.mean(h * h, axis=-1, keepdims=True) + EPS)
    hn_ref[...] = ((h * r) * g_ref[...]).astype(hn_ref.dtype)


def matmul_residual_norm(a, w, res, g, tm=512):
    m, kdim = a.shape
    n = w.shape[1]
    row = lambda width: pl.BlockSpec((tm, width), lambda i: (i, 0))
    return pl.pallas_call(
        _out_norm_kernel,
        grid=(m // tm,),
        in_specs=[row(kdim), pl.BlockSpec((kdim, n), lambda i: (0, 0)), row(n), pl.BlockSpec((1, n), lambda i: (0, 0))],
        out_specs=[row(n), row(n)],
        out_shape=[jax.ShapeDtypeStruct((m, n), F32), jax.ShapeDtypeStruct((m, n), BF16)],
        compiler_params=_cparams(("parallel",)),
        name="matmul_residual_norm",
    )(a, w, res, g.reshape(1, n))


def _norm_rope(x, gain, cos_full, sin_signed, half):
    y = (x * lax.rsqrt(jnp.mean(x * x, axis=-1, keepdims=True) + EPS)) * gain
    if 2 * half == LANES:
        partner = pltpu.roll(y, half, 1)
    else:
        lane = lax.broadcasted_iota(jnp.int32, y.shape, 1)
        partner = jnp.where(lane % (2 * half) < half, pltpu.roll(y, LANES - half, 1), pltpu.roll(y, half, 1))
    return y * cos_full + partner * sin_signed


def _dilated_kernel(q_ref, k_ref, v_ref, cos_ref, sin_ref, qg_ref, kg_ref, y_ref,
                    qn, kn, vn, kd, vd, acc, mrun, zrun, sbuf, pbuf, mblk, bias, *, seq):
    blk = A_REACH_BLOCK
    qb = A_QUERY_BLOCK
    kw = qb + 2 * blk
    grp = A_GROUP
    grp_mm = A_GROUP_MATMUL
    tile = 512
    scale = HEAD_DIM ** -0.5

    def prep(i, carry):
        rows = pl.ds(pl.multiple_of(i * tile, tile), tile)
        cos = cos_ref[rows, :]
        sin = sin_ref[rows, :]
        qn[rows, :] = _norm_rope(q_ref[0, rows, :], qg_ref[...], cos, sin, HEAD_DIM // 2)
        kn[rows, :] = _norm_rope(k_ref[0, rows, :], kg_ref[...], cos, sin, HEAD_DIM // 2)
        vn[rows, :] = v_ref[0, rows, :].astype(F32)
        return carry

    lax.fori_loop(0, seq // tile, prep, 0)
    zpad = jnp.zeros((blk, HEAD_DIM), BF16)
    kd[0:blk, :] = zpad
    kd[blk + seq:, :] = zpad
    vd[0:blk, :] = zpad
    vd[blk + seq:, :] = zpad

    t_io = lax.broadcasted_iota(jnp.int32, (qb, kw), 0)
    j_io = lax.broadcasted_iota(jnp.int32, (qb, kw), 1)
    in_reach = jnp.abs(j_io - blk - t_io) <= blk
    bias[0] = jnp.where(in_reach, 0.0, NEG_INF)
    bias[1] = jnp.where(in_reach & (j_io >= blk), 0.0, NEG_INF)
    bias[2] = jnp.where(in_reach & (j_io < qb + blk), 0.0, NEG_INF)
    nblk = seq // qb
    ones = jnp.ones((kw, HEAD_DIM), BF16)

    for pi, (_, d) in enumerate(sorted(A_PATTERNS, key=lambda wd: -wd[1])):
        sub_len = seq // d
        chunk = 256
        per_res = sub_len // chunk
        blk_per_res = sub_len // qb
        first = pi == 0

        def gather(idx, carry, d=d, sub_len=sub_len, per_res=per_res):
            r = idx // per_res
            j = idx % per_res
            if d == 1:
                src = pl.ds(pl.multiple_of(j * chunk, chunk), chunk)
            else:
                src = pl.ds(r + j * chunk * d, chunk, stride=d)
            dst = pl.ds(pl.multiple_of(blk + r * sub_len + j * chunk, blk), chunk)
            kd[dst, :] = kn[src, :].astype(BF16)
            vd[dst, :] = vn[src, :].astype(BF16)
            return carry

        lax.fori_loop(0, seq // chunk, gather, 0)

        def natural_rows(n, d=d, blk_per_res=blk_per_res):
            if d == 1:
                return pl.ds(pl.multiple_of(n * qb, qb), qb)
            return pl.ds(n // blk_per_res + (n % blk_per_res) * qb * d, qb, stride=d)

        def packed_rows(n, size=qb):
            return pl.ds(pl.multiple_of(n * qb, qb), size)

        def scores(it, carry, blk_per_res=blk_per_res):
            for g in range(grp_mm):
                n = it * grp_mm + g
                q = qn[natural_rows(n), :].astype(BF16)
                s = lax.dot_general(q, kd[packed_rows(n, kw), :], (((1,), (1,)), ((), ())),
                                    preferred_element_type=F32) * scale
                i = n % blk_per_res
                kind = jnp.where(i == 0, 1, jnp.where(i == blk_per_res - 1, 2, 0))
                sbuf[packed_rows(n), :] = s + bias[kind]
            return carry

        lax.fori_loop(0, nblk // grp_mm, scores, 0)

        def softmax(it, carry):
            for g in range(grp):
                rows = packed_rows(it * grp + g)
                s = sbuf[rows, :]
                m = jnp.max(s, axis=-1, keepdims=True)
                pbuf[rows, :] = jnp.exp(s - m).astype(BF16)
                mblk[rows, :] = jnp.broadcast_to(m, (qb, HEAD_DIM))
            return carry

        lax.fori_loop(0, nblk // grp, softmax, 0)

        def values(it, carry, first=first):
            for g in range(grp_mm):
                n = it * grp_mm + g
                vaug = jnp.concatenate([vd[packed_rows(n, kw), :], ones], axis=1)
                oz = jnp.dot(pbuf[packed_rows(n), :], vaug, preferred_element_type=F32)
                o = oz[:, :HEAD_DIM]
                z = oz[:, HEAD_DIM:]
                if first:
                    rw = natural_rows(n)
                    acc[rw, :] = o
                    zrun[rw, :] = z
                    mrun[rw, :] = mblk[packed_rows(n), :]
                else:
                    rw = natural_rows(n)
                    m_old = mrun[rw, :]
                    mb = mblk[packed_rows(n), :]
                    m_new = jnp.maximum(m_old, mb)
                    w_old = jnp.exp(m_old - m_new)
                    w_blk = jnp.exp(mb - m_new)
                    acc[rw, :] = w_old * acc[rw, :] + w_blk * o
                    zrun[rw, :] = w_old * zrun[rw, :] + w_blk * z
                    mrun[rw, :] = m_new
            return carry

        lax.fori_loop(0, nblk // grp_mm, values, 0)

    def finish(i, carry):
        rows = pl.ds(pl.multiple_of(i * tile, tile), tile)
        y_ref[0, rows, :] = (acc[rows, :] / zrun[rows, :]).astype(y_ref.dtype)
        return carry

    lax.fori_loop(0, seq // tile, finish, 0)


def dilated_mixer(y3, v3, cos_full, sin_signed, q_gain, k_gain):
    b, s, _ = y3.shape
    assert all(w // (2 * d) == A_REACH_BLOCK for w, d in A_PATTERNS)
    assert all((s // d) % 256 == 0 for _, d in A_PATTERNS) and (s // A_QUERY_BLOCK) % A_GROUP_MATMUL == 0
    kw = A_QUERY_BLOCK + 2 * A_REACH_BLOCK
    head = lambda off: pl.BlockSpec((1, s, HEAD_DIM), lambda bi, h: (bi, 0, off // HEAD_DIM + h))
    table = pl.BlockSpec((s, HEAD_DIM), lambda bi, h: (0, 0))
    vec = pl.BlockSpec((1, HEAD_DIM), lambda bi, h: (0, 0))
    pad_rows = s + 2 * A_REACH_BLOCK
    return pl.pallas_call(
        functools.partial(_dilated_kernel, seq=s),
        grid=(b, A_HEADS),
        in_specs=[head(Y_AQ), head(Y_AK), head(V_AV), table, table, vec, vec],
        out_specs=pl.BlockSpec((1, s, HEAD_DIM), lambda bi, h: (bi, 0, h)),
        out_shape=jax.ShapeDtypeStruct((b, s, A_WIDTH), BF16),
        scratch_shapes=[pltpu.VMEM((s, HEAD_DIM), F32)] * 3 + [pltpu.VMEM((pad_rows, HEAD_DIM), BF16)] * 2
        + [pltpu.VMEM((s, HEAD_DIM), F32)] * 3
        + [pltpu.VMEM((s, kw), F32), pltpu.VMEM((s, kw), BF16), pltpu.VMEM((s, HEAD_DIM), F32)]
        + [pltpu.VMEM((3, A_QUERY_BLOCK, kw), F32)],
        compiler_params=_cparams(("parallel", "parallel")),
        name="dilated_mixer",
    )(y3, y3, v3, cos_full, sin_signed, q_gain.reshape(1, HEAD_DIM), k_gain.reshape(1, HEAD_DIM))


def _gqa_kernel(q_ref, k_ref, v_ref, cosq_ref, sinq_ref, cosk_ref, sink_ref, qg_ref, kg_ref, o_ref, kn, vaug, *, group, seq):
    half = HEAD_DIM // 4
    tile = 512

    @pl.when(pl.program_id(2) == 0)
    def _():
        def prep(i, carry):
            rows = pl.ds(pl.multiple_of(i * tile, tile), tile)
            kn[rows, :] = _norm_rope(k_ref[0, rows, :], kg_ref[...], cosk_ref[rows, :], sink_ref[rows, :],
                                     half).astype(kn.dtype)
            return carry

        lax.fori_loop(0, seq // tile, prep, 0)

        vaug[:, :HEAD_DIM] = v_ref[0]
        vaug[:, HEAD_DIM:] = jnp.ones((seq, HEAD_DIM), vaug.dtype)

    k = kn[...]
    v = vaug[...]
    c = (HEAD_DIM ** -0.5) * math.log2(math.e)
    for g in range(group):
        cols = slice(g * HEAD_DIM, (g + 1) * HEAD_DIM)
        q = _norm_rope(q_ref[0, :, cols], qg_ref[...], cosq_ref[...], sinq_ref[...], half).astype(BF16)
        s = lax.dot_general(q, k, (((1,), (1,)), ((), ())), preferred_element_type=F32)
        m = jnp.max(s, axis=-1, keepdims=True)
        p = jnp.exp2((s - m) * c)
        o = jnp.dot(p.astype(BF16), v, preferred_element_type=F32)
        o_ref[0, :, cols] = (o[:, :HEAD_DIM] / o[:, HEAD_DIM:]).astype(o_ref.dtype)


def gqa_attention(y3, v3, cos_full, sin_signed, q_gain, k_gain, tq=512):
    b, s, _ = y3.shape
    group = B_Q_HEADS // B_KV_HEADS
    gw = group * HEAD_DIM
    vec = pl.BlockSpec((1, HEAD_DIM), lambda bi, h, t: (0, 0))
    qtab = pl.BlockSpec((tq, HEAD_DIM), lambda bi, h, t: (t, 0))
    ktab = pl.BlockSpec((s, HEAD_DIM), lambda bi, h, t: (0, 0))
    return pl.pallas_call(
        functools.partial(_gqa_kernel, group=group, seq=s),
        grid=(b, B_KV_HEADS, s // tq),
        in_specs=[pl.BlockSpec((1, tq, gw), lambda bi, h, t: (bi, t, Y_BQ // gw + h)),
                  pl.BlockSpec((1, s, HEAD_DIM), lambda bi, h, t: (bi, 0, Y_BK // HEAD_DIM + h)),
                  pl.BlockSpec((1, s, HEAD_DIM), lambda bi, h, t: (bi, 0, V_BV // HEAD_DIM + h)),
                  qtab, qtab, ktab, ktab, vec, vec],
        out_specs=pl.BlockSpec((1, tq, gw), lambda bi, h, t: (bi, t, h)),
        out_shape=jax.ShapeDtypeStruct((b, s, B_Q_WIDTH), BF16),
        scratch_shapes=[pltpu.VMEM((s, HEAD_DIM), BF16), pltpu.VMEM((s, 2 * HEAD_DIM), BF16)],
        compiler_params=_cparams(("parallel", "parallel", "arbitrary")),
        name="gqa_attention",
    )(y3, y3, v3, cos_full, sin_signed, cos_full, sin_signed, q_gain.reshape(1, HEAD_DIM), k_gain.reshape(1, HEAD_DIM))


def _conv_kernel(xp_ref, xm_ref, xn_ref, w_ref, b_ref, qk_ref, kt_ref, buf, *, tile, ntiles):
    ti = pl.program_id(1)
    halo = 8
    buf[0:halo] = jnp.where(ti > 0, xp_ref[0], 0.0)
    buf[halo:halo + tile] = xm_ref[0]
    buf[halo + tile:] = jnp.where(ti < ntiles - 1, xn_ref[0], 0.0)
    pad = C_CONV // 2
    acc = w_ref[0:1, :] * buf[halo - pad:halo - pad + tile, :]
    for j in range(1, C_CONV):
        acc = acc + w_ref[j:j + 1, :] * buf[halo - pad + j:halo - pad + j + tile, :]
    acc = acc + b_ref[...]
    act = acc * jax.nn.sigmoid(acc)
    q = act[:, :C_QK_WIDTH]
    k = act[:, C_QK_WIDTH:] * (C_QK_DIM ** -0.5)
    qk_ref[0, :, :C_QK_WIDTH] = q.astype(qk_ref.dtype)
    qk_ref[0, :, C_QK_WIDTH:] = k.astype(qk_ref.dtype)
    for c in range(tile // MLSTM_CHUNK):
        kt_ref[0, c] = k[c * MLSTM_CHUNK:(c + 1) * MLSTM_CHUNK, :].T.astype(kt_ref.dtype)


def mlstm_conv(y3, conv_w, conv_b, tile=512):
    b, s, _ = y3.shape
    ntiles = s // tile
    w2 = 2 * C_QK_WIDTH
    cb = Y_CQK // w2
    per = tile // 8
    return pl.pallas_call(
        functools.partial(_conv_kernel, tile=tile, ntiles=ntiles),
        grid=(b, ntiles),
        in_specs=[pl.BlockSpec((1, 8, w2), lambda bi, t: (bi, jnp.maximum(t * per - 1, 0), cb)),
                  pl.BlockSpec((1, tile, w2), lambda bi, t: (bi, t, cb)),
                  pl.BlockSpec((1, 8, w2), lambda bi, t: (bi, jnp.minimum((t + 1) * per, s // 8 - 1), cb)),
                  pl.BlockSpec((C_CONV, w2), lambda bi, t: (0, 0)),
                  pl.BlockSpec((1, w2), lambda bi, t: (0, 0))],
        out_specs=[pl.BlockSpec((1, tile, w2), lambda bi, t: (bi, t, 0)),
                   pl.BlockSpec((1, tile // MLSTM_CHUNK, C_QK_WIDTH, MLSTM_CHUNK), lambda bi, t: (bi, t, 0, 0))],
        out_shape=[jax.ShapeDtypeStruct((b, s, w2), BF16),
                   jax.ShapeDtypeStruct((b, s // MLSTM_CHUNK, C_QK_WIDTH, MLSTM_CHUNK), BF16)],
        scratch_shapes=[pltpu.VMEM((tile + 16, w2), F32)],
        compiler_params=_cparams(("parallel", "parallel")),
        name="mlstm_conv",
    )(y3, y3, y3, conv_w, conv_b.reshape(1, w2))


def _gate_kernel(gi_ref, gf_ref, bi_ref, bf_ref, cols_ref, rt_ref, wkt_ref, dec_ref,
                 b_s, r_s, dmax_s, btot_s, gmax_s, mpf_s, mpb_s, *, nc):
    lc = MLSTM_CHUNK
    t_in = lax.broadcasted_iota(jnp.int32, (lc, LANES), 0)
    lane = lax.broadcasted_iota(jnp.int32, (lc, LANES), 1)
    fwd = lane % 2 == 0
    fwd_row = fwd[0:1]
    steps = [1 << i for i in range(lc.bit_length() - 1)]

    def chunk_rows(c):
        return pl.ds(pl.multiple_of(c * lc, lc), lc)

    def local(c, carry):
        rows = chunk_rows(c)
        ii = gi_ref[0, rows, :] + bi_ref[...]
        ff = gf_ref[0, rows, :] + bf_ref[...]
        lf = jnp.minimum(ff, 0.0) - jnp.log1p(jnp.exp(-jnp.abs(ff)))
        cs = lf
        rs = lf
        for k in steps:
            cs = cs + jnp.where(t_in >= k, pltpu.roll(cs, k, 0), 0.0)
            rs = rs + jnp.where(t_in + k < lc, pltpu.roll(rs, lc - k, 0), 0.0)
        b = jnp.where(fwd, cs, rs)
        btot = cs + rs - lf
        r = ii - b
        pm = r
        sm = r
        for k in steps:
            pm = jnp.where(t_in >= k, jnp.maximum(pm, pltpu.roll(pm, k, 0)), pm)
            sm = jnp.where(t_in + k < lc, jnp.maximum(sm, pltpu.roll(sm, lc - k, 0)), sm)
        b_s[rows, :] = b
        r_s[rows, :] = r
        dmax_s[rows, :] = b + jnp.where(fwd, pm, sm)
        btot_s[pl.ds(c, 1), :] = btot[0:1]
        gmax_s[pl.ds(c, 1), :] = (btot + jnp.maximum(pm, sm))[0:1]
        return carry

    lax.fori_loop(0, nc, local, 0)

    def scan(k, carry):
        mf, mb = carry
        mpf_s[pl.ds(k, 1), :] = mf
        mf = jnp.maximum(btot_s[pl.ds(k, 1), :] + mf, gmax_s[pl.ds(k, 1), :])
        kb = nc - 1 - k
        mpb_s[pl.ds(kb, 1), :] = mb
        mb = jnp.maximum(btot_s[pl.ds(kb, 1), :] + mb, gmax_s[pl.ds(kb, 1), :])
        return mf, mb

    zero = jnp.zeros((1, LANES), F32)
    lax.fori_loop(0, nc, scan, (zero, zero))

    def emit(c, carry):
        rows = chunk_rows(c)
        b = b_s[rows, :]
        r = r_s[rows, :]
        btot = btot_s[pl.ds(c, 1), :]
        mprev = jnp.where(fwd_row, mpf_s[pl.ds(c, 1), :], mpb_s[pl.ds(c, 1), :])
        mnew = jnp.maximum(btot + mprev, gmax_s[pl.ds(c, 1), :])
        dec_ref[0, pl.ds(c, 1), :] = jnp.exp(btot + mprev - mnew)
        m_inter = b + mprev
        m_t = jnp.maximum(m_inter, dmax_s[rows, :])
        cb = b - m_t
        inter = jnp.exp(m_inter - m_t)
        e = jnp.exp(-m_t)
        for p in range(C_PAIRS):
            cols_ref[0, p, rows, :] = jnp.where(
                lane < 4, pltpu.roll(cb, (LANES - 4 * p) % LANES, 1),
                jnp.where(lane < 8, pltpu.roll(inter, (LANES + 4 - 4 * p) % LANES, 1),
                          pltpu.roll(e, (LANES + 8 - 4 * p) % LANES, 1)))
        wk = jnp.exp(btot + r - mnew)
        rt_ref[0, c] = r.T[:C_STREAMS, :]
        wkt_ref[0, c] = wk.T[:C_STREAMS, :]
        return carry

    lax.fori_loop(0, nc, emit, 0)


def mlstm_gates(y3, bias_i, bias_f):
    b, s, _ = y3.shape
    nc = s // MLSTM_CHUNK
    row = pl.BlockSpec((1, nc, C_STREAMS, MLSTM_CHUNK), lambda bi: (bi, 0, 0, 0))
    vec = pl.BlockSpec((1, LANES), lambda bi: (0, 0))
    return pl.pallas_call(
        functools.partial(_gate_kernel, nc=nc),
        grid=(b,),
        in_specs=[pl.BlockSpec((1, s, LANES), lambda bi: (bi, 0, Y_GI // LANES)),
                  pl.BlockSpec((1, s, LANES), lambda bi: (bi, 0, Y_GF // LANES)), vec, vec],
        out_specs=[pl.BlockSpec((1, C_PAIRS, s, LANES), lambda bi: (bi, 0, 0, 0)), row, row,
                   pl.BlockSpec((1, nc, LANES), lambda bi: (bi, 0, 0))],
        out_shape=[jax.ShapeDtypeStruct((b, C_PAIRS, s, LANES), F32)]
        + [jax.ShapeDtypeStruct((b, nc, C_STREAMS, MLSTM_CHUNK), F32)] * 2
        + [jax.ShapeDtypeStruct((b, nc, LANES), F32)],
        scratch_shapes=[pltpu.VMEM((s, LANES), F32)] * 3 + [pltpu.VMEM((nc, LANES), F32)] * 4,
        compiler_params=_cparams(("parallel",)),
        name="mlstm_gates",
    )(y3, y3, bias_i.reshape(1, LANES), bias_f.reshape(1, LANES))


def _mlstm_kernel(dec_ref, q_ref, kt_ref, v_ref, o_ref, g_ref, cols_ref, rt_ref, wkt_ref,
                  y_ref, abuf, kwbuf, hsum, ct, *, seq, nc):
    lc = MLSTM_CHUNK
    bi = pl.program_id(0)
    pair = pl.program_id(1)
    lane = lax.broadcasted_iota(jnp.int32, (lc, LANES), 1)
    trow = lax.broadcasted_iota(jnp.int32, (lc, lc), 0)
    scol = lax.broadcasted_iota(jnp.int32, (lc, lc), 1)
    ones = jnp.ones((lc, C_V_DIM), BF16)

    def head_q(q2, hh):
        return jnp.where((lane >= hh * C_QK_DIM) & (lane < (hh + 1) * C_QK_DIM), q2, jnp.zeros_like(q2))

    def intra(c, carry):
        rows = pl.ds(pl.multiple_of(c * lc, lc), lc)
        q2 = q_ref[0, rows, :]
        kt2 = kt_ref[0, c]
        for hh in range(2):
            sc = jnp.dot(head_q(q2, hh), kt2, preferred_element_type=F32)
            kth = kt2[hh * C_QK_DIM:(hh + 1) * C_QK_DIM, :].astype(F32)
            for direction in range(2):
                sl = hh * 2 + direction
                stream = pair * 4 + sl
                mask = (scol <= trow) if direction == 0 else (scol >= trow)
                arg = cols_ref[0, 0, rows, sl:sl + 1] + rt_ref[0, c, pl.ds(stream, 1), :]
                abuf[sl, c] = (jnp.exp(jnp.where(mask, arg, NEG_INF)) * sc).astype(BF16)
                kwbuf[sl, c] = (kth * wkt_ref[0, c, pl.ds(stream, 1), :]).astype(BF16)
        return carry

    lax.fori_loop(0, nc, intra, 0, unroll=2)

    ct[...] = jnp.zeros_like(ct)
    hsum[...] = jnp.zeros_like(hsum)

    def step(it, carry):
        for direction in range(2):
            c = it if direction == 0 else nc - 1 - it
            rows = pl.ds(pl.multiple_of(c * lc, lc), lc)
            q2 = q_ref[0, rows, :]
            ct_pair = ct[direction].astype(BF16)
            for hh in range(2):
                sl = hh * 2 + direction
                dec = dec_ref[(bi * nc + c) * C_STREAMS + pair * 4 + sl]
                vaug = jnp.concatenate([v_ref[0, rows, hh * C_V_DIM:(hh + 1) * C_V_DIM], ones], axis=1)
                num_a = jnp.dot(abuf[sl, c], vaug, preferred_element_type=F32)
                num_c = jnp.dot(head_q(q2, hh), ct_pair, preferred_element_type=F32)
                num = num_a + cols_ref[0, 0, rows, 4 + sl:5 + sl] * num_c
                den = jnp.maximum(jnp.abs(num[:, C_V_DIM:]), cols_ref[0, 0, rows, 8 + sl:9 + sl])
                hsum[rows, hh * C_V_DIM:(hh + 1) * C_V_DIM] += num[:, :C_V_DIM] / den
                u = jnp.dot(kwbuf[sl, c], vaug, preferred_element_type=F32)
                d_rows = slice(hh * C_QK_DIM, (hh + 1) * C_QK_DIM)
                ct[direction, d_rows, :] = dec * ct[direction, d_rows, :] + u
        return carry

    lax.fori_loop(0, nc, step, 0, unroll=2)

    tile = 512

    def finish(i, carry):
        rows = pl.ds(pl.multiple_of(i * tile, tile), tile)
        for hh in range(2):
            cols = slice(hh * C_V_DIM, (hh + 1) * C_V_DIM)
            h = hsum[rows, cols]
            h = h * lax.rsqrt(jnp.mean(h * h, axis=-1, keepdims=True) + EPS)
            h = h * g_ref[:, cols]
            y_ref[0, rows, cols] = (h * jax.nn.sigmoid(o_ref[0, rows, cols])).astype(y_ref.dtype)
        return carry

    lax.fori_loop(0, seq // tile, finish, 0)


def mlstm_scan(dec, qk, kt, v3, y3, out_g, cols, rt, wkt):
    b, s, _ = qk.shape
    nc = s // MLSTM_CHUNK
    pw = 2 * C_V_DIM
    col = pl.BlockSpec((1, 1, s, LANES), lambda bi, p, d: (bi, p, 0, 0))
    row = pl.BlockSpec((1, nc, C_STREAMS, MLSTM_CHUNK), lambda bi, p, d: (bi, 0, 0, 0))
    grid_spec = pltpu.PrefetchScalarGridSpec(
        num_scalar_prefetch=1,
        grid=(b, C_PAIRS),
        in_specs=[pl.BlockSpec((1, s, LANES), lambda bi, p, d: (bi, 0, p)),
                  pl.BlockSpec((1, nc, LANES, MLSTM_CHUNK), lambda bi, p, d: (bi, 0, p, 0)),
                  pl.BlockSpec((1, s, pw), lambda bi, p, d: (bi, 0, V_CV // pw + p)),
                  pl.BlockSpec((1, s, pw), lambda bi, p, d: (bi, 0, Y_CO // pw + p)),
                  pl.BlockSpec((1, pw), lambda bi, p, d: (0, p)),
                  col, row, row],
        out_specs=pl.BlockSpec((1, s, pw), lambda bi, p, d: (bi, 0, p)),
        scratch_shapes=[pltpu.VMEM((4, nc, MLSTM_CHUNK, MLSTM_CHUNK), BF16),
                        pltpu.VMEM((4, nc, C_QK_DIM, MLSTM_CHUNK), BF16),
                        pltpu.VMEM((s, pw), F32),
                        pltpu.VMEM((2, LANES, 2 * C_V_DIM), F32)],
    )
    return pl.pallas_call(
        functools.partial(_mlstm_kernel, seq=s, nc=nc),
        grid_spec=grid_spec,
        out_shape=jax.ShapeDtypeStruct((b, s, C_HEADS * C_V_DIM), BF16),
        compiler_params=_cparams(("parallel", "arbitrary")),
        name="mlstm_scan",
    )(dec, qk, kt, v3, y3, out_g.reshape(1, C_V_WIDTH), cols, rt, wkt)


def _merge_kernel(ya_ref, yb_ref, yc_ref, ga_ref, gb_ref, gc_ref, wa_ref, wb_ref, wc_ref, o_ref):
    acc = jax.nn.sigmoid(ga_ref[...]) * jnp.dot(ya_ref[...], wa_ref[...], preferred_element_type=F32)
    acc = acc + jax.nn.sigmoid(gb_ref[...]) * jnp.dot(yb_ref[...], wb_ref[...], preferred_element_type=F32)
    acc = acc + jax.nn.sigmoid(gc_ref[...]) * jnp.dot(yc_ref[...], wc_ref[...], preferred_element_type=F32)
    o_ref[...] = acc.astype(o_ref.dtype)


def branch_merge(ya, yb, yc, y2, wa, wb, wc, tm=512, tn=1024):
    m = ya.shape[0]
    n = wa.shape[1]
    nj = n // tn

    def act(width):
        return pl.BlockSpec((tm, width), lambda j, i: (i, 0))

    def gate(branch):
        return pl.BlockSpec((tm, tn), lambda j, i: (i, Y_G // tn + branch * nj + j))

    def weight(width):
        return pl.BlockSpec((width, tn), lambda j, i: (0, j))

    return pl.pallas_call(
        _merge_kernel,
        grid=(nj, m // tm),
        in_specs=[act(A_WIDTH), act(B_Q_WIDTH), act(C_V_WIDTH), gate(0), gate(1), gate(2),
                  weight(A_WIDTH), weight(B_Q_WIDTH), weight(C_V_WIDTH)],
        out_specs=pl.BlockSpec((tm, tn), lambda j, i: (i, j)),
        out_shape=jax.ShapeDtypeStruct((m, n), BF16),
        compiler_params=_cparams(("parallel", "parallel")),
        name="branch_merge",
    )(ya, yb, yc, y2, y2, y2, wa, wb, wc)


def _rope_tables(pos, dim):
    inv_freq = ROPE_THETA ** (-jnp.arange(0, dim, 2, dtype=F32) / dim)
    ang = pos.astype(F32)[:, None] * inv_freq[None, :]
    return jnp.cos(ang), jnp.sin(ang)


def _stream_gate_columns(w_gate, b_gate):
    lanes = jnp.arange(C_STREAMS)
    head = (lanes // 4) * 2 + (lanes % 4) // 2
    direction = lanes % 2
    i_cols = direction * 2 * C_HEADS + head
    f_cols = i_cols + C_HEADS
    pad = LANES - C_STREAMS
    wi = jnp.pad(w_gate[:, i_cols], ((0, 0), (0, pad)))
    wf = jnp.pad(w_gate[:, f_cols], ((0, 0), (0, pad)))
    return wi, wf, jnp.pad(b_gate[i_cols], (0, pad)), jnp.pad(b_gate[f_cols], (0, pad))


def _layer(x2, shape, tables, norm1_g, w_in, a_q_gain, a_k_gain, b_q_gain, b_k_gain, c_conv_w, c_conv_b,
           c_gate_b, c_out_g, w_branch_a, w_branch_b, w_branch_c, w_out, norm2_g, w_up, w_down):
    bsz, seq = shape
    cos_a, sin_a, cos_b, sin_b = tables
    sizes = (A_WIDTH, A_WIDTH, A_WIDTH, B_Q_WIDTH, B_KV_WIDTH, B_KV_WIDTH, 2 * C_QK_WIDTH, C_V_WIDTH, C_V_WIDTH,
             4 * C_HEADS, N_BRANCHES * x2.shape[1])
    offs = [0]
    for sz in sizes:
        offs.append(offs[-1] + sz)
    seg = lambda i: w_in[:, offs[i]:offs[i + 1]]
    wi, wf, bias_i, bias_f = _stream_gate_columns(seg(9), c_gate_b)
    w_y = jnp.concatenate([seg(10), seg(6), seg(8), seg(3), seg(0), seg(1), seg(4), wi, wf], axis=1).astype(BF16)
    w_v = jnp.concatenate([seg(2), seg(5), seg(7)], axis=1).astype(BF16)

    xn = rmsnorm(x2, norm1_g)
    y2 = matmul(xn, w_y, out_dtype=F32, tm=2048)
    v2 = matmul(xn, w_v, out_dtype=BF16, tm=2048)
    y3 = y2.reshape(bsz, seq, Y_COLS)
    v3 = v2.reshape(bsz, seq, V_COLS)

    ya = dilated_mixer(y3, v3, cos_a, sin_a, a_q_gain, a_k_gain).reshape(bsz * seq, A_WIDTH)
    yb = gqa_attention(y3, v3, cos_b, sin_b, b_q_gain, b_k_gain).reshape(bsz * seq, B_Q_WIDTH)

    qk_c, kt_c = mlstm_conv(y3, c_conv_w, c_conv_b)
    cols, rt, wkt, dec = mlstm_gates(y3, bias_i, bias_f)
    dec_flat = dec[:, :, :C_STREAMS].reshape(-1)
    yc = mlstm_scan(dec_flat, qk_c, kt_c, v3, y3, c_out_g, cols, rt, wkt).reshape(bsz * seq, C_V_WIDTH)

    merged = branch_merge(ya, yb, yc, y2, w_branch_a.astype(BF16), w_branch_b.astype(BF16), w_branch_c.astype(BF16))
    h, hn = matmul_residual_norm(merged, w_out.astype(BF16), x2, norm2_g)
    u2 = matmul(hn, w_up.astype(BF16), out_dtype=BF16, relu2=True, tm=2048)
    return matmul(u2, w_down.astype(BF16), out_dtype=F32, res=h, tm=1024, tn=2048, tk=1024)


def kernel(x, norm1_g, w_in, a_q_gain, a_k_gain, b_q_gain, b_k_gain, c_conv_w, c_conv_b, c_gate_b, c_out_g, w_branch_a, w_branch_b, w_branch_c, w_out, norm2_g, w_up, w_down):
    bsz, seq, d_model = x.shape
    pos = jnp.arange(seq)
    cos1, sin1 = _rope_tables(pos, HEAD_DIM)
    cos_r, sin_r = _rope_tables(pos // GRID_W, HEAD_DIM // 2)
    cos_c, sin_c = _rope_tables(pos % GRID_W, HEAD_DIM // 2)
    tables = (jnp.concatenate([cos1, cos1], axis=1), jnp.concatenate([-sin1, sin1], axis=1),
              jnp.concatenate([cos_r, cos_r, cos_c, cos_c], axis=1),
              jnp.concatenate([-sin_r, sin_r, -sin_c, sin_c], axis=1))
    x2 = x.reshape(bsz * seq, d_model)
    for l in range(norm1_g.shape[0]):
        x2 = _layer(x2, (bsz, seq), tables, norm1_g[l], w_in[l], a_q_gain[l], a_k_gain[l], b_q_gain[l], b_k_gain[l],
                    c_conv_w[l], c_conv_b[l], c_gate_b[l], c_out_g[l], w_branch_a[l], w_branch_b[l], w_branch_c[l],
                    w_out[l], norm2_g[l], w_up[l], w_down[l])
    return x2.reshape(bsz, seq, d_model)
```

```python
import functools
import math

import jax
import jax.numpy as jnp
from jax import lax
from jax.experimental import pallas as pl
from jax.experimental.pallas import tpu as pltpu

F32 = jnp.float32
BF16 = jnp.bfloat16

HEAD_DIM = 128
GRID_W = 64
ROPE_THETA = 10000.0
EPS = 1e-6
NEG_INF = -1e30

A_HEADS = 6
A_PATTERNS = ((128, 1), (512, 4), (2048, 16))
A_REACH_BLOCK = 64
A_QUERY_BLOCK = 128
A_GROUP = 4
A_GROUP_MATMUL = 8
B_Q_HEADS = 8
B_KV_HEADS = 2
B_ROW_SLAB = 256
C_HEADS = 8
C_QK_DIM = 64
C_V_DIM = 128
C_CONV = 5
N_BRANCHES = 3

A_WIDTH = A_HEADS * HEAD_DIM
B_Q_WIDTH = B_Q_HEADS * HEAD_DIM
B_KV_WIDTH = B_KV_HEADS * HEAD_DIM
C_QK_WIDTH = C_HEADS * C_QK_DIM
C_V_WIDTH = C_HEADS * C_V_DIM

LANES = 128
MLSTM_CHUNK = 128
C_PAIRS = C_HEADS // 2
C_STREAMS = 2 * C_HEADS
VMEM_LIMIT = 60 * 1024 * 1024

Y_G = 0
Y_CQK = 6144
Y_CO = 7168
Y_BQ = 8192
Y_AQ = 9216
Y_AK = 9984
Y_BK = 10752
Y_GI = 11008
Y_GF = 11136
Y_COLS = 11264
V_AV = 0
V_BV = 768
V_CV = 1024
V_COLS = 2048


def _cparams(sem):
    return pltpu.CompilerParams(dimension_semantics=sem, vmem_limit_bytes=VMEM_LIMIT)


def _rmsnorm_kernel(x_ref, g_ref, o_ref):
    x = x_ref[...]
    r = lax.rsqrt(jnp.mean(x * x, axis=-1, keepdims=True) + EPS)
    o_ref[...] = ((x * r) * g_ref[...]).astype(o_ref.dtype)


def rmsnorm(x, g, tm=512):
    m, d = x.shape
    return pl.pallas_call(
        _rmsnorm_kernel,
        grid=(m // tm,),
        in_specs=[pl.BlockSpec((tm, d), lambda i: (i, 0)), pl.BlockSpec((1, d), lambda i: (0, 0))],
        out_specs=pl.BlockSpec((tm, d), lambda i: (i, 0)),
        out_shape=jax.ShapeDtypeStruct((m, d), BF16),
        compiler_params=_cparams(("parallel",)),
        name="rmsnorm",
    )(x, g.reshape(1, d))


def _matmul_kernel(*refs, nk, relu2, has_res):
    if has_res:
        a_ref, w_ref, r_ref, o_ref = refs
    else:
        a_ref, w_ref, o_ref = refs
        r_ref = None

    prod = jnp.dot(a_ref[...], w_ref[...], preferred_element_type=F32)
    if relu2:
        u = jnp.maximum(prod, 0.0)
        prod = u * u
    if nk == 1:
        o_ref[...] = (prod if r_ref is None else r_ref[...] + prod).astype(o_ref.dtype)
    else:
        k = pl.program_id(2)

        @pl.when(k == 0)
        def _():
            o_ref[...] = prod if r_ref is None else r_ref[...] + prod

        @pl.when(k > 0)
        def _():
            o_ref[...] += prod


def matmul(a, w, *, out_dtype, res=None, relu2=False, tm=1024, tn=1024, tk=2048):
    m, kdim = a.shape
    n = w.shape[1]
    tn = min(tn, n)
    tk = min(tk, kdim)
    nk = kdim // tk
    assert nk == 1 or (out_dtype == F32 and not relu2)
    in_specs = [pl.BlockSpec((tm, tk), lambda i, j, k: (i, k)),
                pl.BlockSpec((tk, tn), lambda i, j, k: (k, j))]
    args = [a, w]
    if res is not None:
        in_specs.append(pl.BlockSpec((tm, tn), lambda i, j, k: (i, j)))
        args.append(res)
    return pl.pallas_call(
        functools.partial(_matmul_kernel, nk=nk, relu2=relu2, has_res=res is not None),
        grid=(m // tm, n // tn, nk),
        in_specs=in_specs,
        out_specs=pl.BlockSpec((tm, tn), lambda i, j, k: (i, j)),
        out_shape=jax.ShapeDtypeStruct((m, n), out_dtype),
        compiler_params=_cparams(("parallel", "parallel", "arbitrary")),
        name="matmul",
    )(*args)


def _out_norm_kernel(a_ref, w_ref, r_ref, g_ref, h_ref, hn_ref):
    h = r_ref[...] + jnp.dot(a_ref[...], w_ref[...], preferred_element_type=F32)
    h_ref[...] = h
    r = lax.rsqrt(jnp.mean(h * h, axis=-1, keepdims=True) + EPS)
    hn_ref[...] = ((h * r) * g_ref[...]).astype(hn_ref.dtype)


def matmul_residual_norm(a, w, res, g, tm=512):
    m, kdim = a.shape
    n = w.shape[1]
    row = lambda width: pl.BlockSpec((tm, width), lambda i: (i, 0))
    return pl.pallas_call(
        _out_norm_kernel,
        grid=(m // tm,),
        in_specs=[row(kdim), pl.BlockSpec((kdim, n), lambda i: (0, 0)), row(n), pl.BlockSpec((1, n), lambda i: (0, 0))],
        out_specs=[row(n), row(n)],
        out_shape=[jax.ShapeDtypeStruct((m, n), F32), jax.ShapeDtypeStruct((m, n), BF16)],
        compiler_params=_cparams(("parallel",)),
        name="matmul_residual_norm",
    )(a, w, res, g.reshape(1, n))


def _norm_rope(x, gain, cos_full, sin_signed, half):
    y = (x * lax.rsqrt(jnp.mean(x * x, axis=-1, keepdims=True) + EPS)) * gain
    if 2 * half == LANES:
        partner = pltpu.roll(y, half, 1)
    else:
        lane = lax.broadcasted_iota(jnp.int32, y.shape, 1)
        partner = jnp.where(lane % (2 * half) < half, pltpu.roll(y, LANES - half, 1), pltpu.roll(y, half, 1))
    return y * cos_full + partner * sin_signed


def _dilated_kernel(q_ref, k_ref, v_ref, cos_ref, sin_ref, qg_ref, kg_ref, y_ref,
                    qn, kn, vn, kd, vd, acc, mrun, zrun, sbuf, pbuf, mblk, bias, *, seq):
    blk = A_REACH_BLOCK
    qb = A_QUERY_BLOCK
    kw = qb + 2 * blk
    grp = A_GROUP
    grp_mm = A_GROUP_MATMUL
    tile = 512
    scale = HEAD_DIM ** -0.5

    def prep(i, carry):
        rows = pl.ds(pl.multiple_of(i * tile, tile), tile)
        cos = cos_ref[rows, :]
        sin = sin_ref[rows, :]
        qn[rows, :] = _norm_rope(q_ref[0, rows, :], qg_ref[...], cos, sin, HEAD_DIM // 2)
        kn[rows, :] = _norm_rope(k_ref[0, rows, :], kg_ref[...], cos, sin, HEAD_DIM // 2)
        vn[rows, :] = v_ref[0, rows, :].astype(F32)
        return carry

    lax.fori_loop(0, seq // tile, prep, 0)
    zpad = jnp.zeros((blk, HEAD_DIM), BF16)
    kd[0:blk, :] = zpad
    kd[blk + seq:, :] = zpad
    vd[0:blk, :] = zpad
    vd[blk + seq:, :] = zpad

    t_io = lax.broadcasted_iota(jnp.int32, (qb, kw), 0)
    j_io = lax.broadcasted_iota(jnp.int32, (qb, kw), 1)
    in_reach = jnp.abs(j_io - blk - t_io) <= blk
    bias[0] = jnp.where(in_reach, 0.0, NEG_INF)
    bias[1] = jnp.where(in_reach & (j_io >= blk), 0.0, NEG_INF)
    bias[2] = jnp.where(in_reach & (j_io < qb + blk), 0.0, NEG_INF)
    nblk = seq // qb
    ones = jnp.ones((kw, HEAD_DIM), BF16)

    for pi, (_, d) in enumerate(sorted(A_PATTERNS, key=lambda wd: -wd[1])):
        sub_len = seq // d
        chunk = 256
        per_res = sub_len // chunk
        blk_per_res = sub_len // qb
        first = pi == 0

        def gather(idx, carry, d=d, sub_len=sub_len, per_res=per_res):
            r = idx // per_res
            j = idx % per_res
            if d == 1:
                src = pl.ds(pl.multiple_of(j * chunk, chunk), chunk)
            else:
                src = pl.ds(r + j * chunk * d, chunk, stride=d)
            dst = pl.ds(pl.multiple_of(blk + r * sub_len + j * chunk, blk), chunk)
            kd[dst, :] = kn[src, :].astype(BF16)
            vd[dst, :] = vn[src, :].astype(BF16)
            return carry

        lax.fori_loop(0, seq // chunk, gather, 0)

        def natural_rows(n, d=d, blk_per_res=blk_per_res):
            if d == 1:
                return pl.ds(pl.multiple_of(n * qb, qb), qb)
            return pl.ds(n // blk_per_res + (n % blk_per_res) * qb * d, qb, stride=d)

        def packed_rows(n, size=qb):
            return pl.ds(pl.multiple_of(n * qb, qb), size)

        def scores(it, carry, blk_per_res=blk_per_res):
            for g in range(grp_mm):
                n = it * grp_mm + g
                q = qn[natural_rows(n), :].astype(BF16)
                s = lax.dot_general(q, kd[packed_rows(n, kw), :], (((1,), (1,)), ((), ())),
                                    preferred_element_type=F32) * scale
                i = n % blk_per_res
                kind = jnp.where(i == 0, 1, jnp.where(i == blk_per_res - 1, 2, 0))
                sbuf[packed_rows(n), :] = s + bias[kind]
            return carry

        lax.fori_loop(0, nblk // grp_mm, scores, 0)

        def softmax(it, carry):
            for g in range(grp):
                rows = packed_rows(it * grp + g)
                s = sbuf[rows, :]
                m = jnp.max(s, axis=-1, keepdims=True)
                pbuf[rows, :] = jnp.exp(s - m).astype(BF16)
                mblk[rows, :] = jnp.broadcast_to(m, (qb, HEAD_DIM))
            return carry

        lax.fori_loop(0, nblk // grp, softmax, 0)

        def values(it, carry, first=first):
            for g in range(grp_mm):
                n = it * grp_mm + g
                vaug = jnp.concatenate([vd[packed_rows(n, kw), :], ones], axis=1)
                oz = jnp.dot(pbuf[packed_rows(n), :], vaug, preferred_element_type=F32)
                o = oz[:, :HEAD_DIM]
                z = oz[:, HEAD_DIM:]
                if first:
                    rw = natural_rows(n)
                    acc[rw, :] = o
                    zrun[rw, :] = z
                    mrun[rw, :] = mblk[packed_rows(n), :]
                else:
                    rw = natural_rows(n)
                    m_old = mrun[rw, :]
                    mb = mblk[packed_rows(n), :]
                    m_new = jnp.maximum(m_old, mb)
                    w_old = jnp.exp(m_old - m_new)
                    w_blk = jnp.exp(mb - m_new)
                    acc[rw, :] = w_old * acc[rw, :] + w_blk * o
                    zrun[rw, :] = w_old * zrun[rw, :] + w_blk * z
                    mrun[rw, :] = m_new
            return carry

        lax.fori_loop(0, nblk // grp_mm, values, 0)

    def finish(i, carry):
        rows = pl.ds(pl.multiple_of(i * tile, tile), tile)
        y_ref[0, rows, :] = (acc[rows, :] / zrun[rows, :]).astype(y_ref.dtype)
        return carry

    lax.fori_loop(0, seq // tile, finish, 0)


def dilated_mixer(y3, v3, cos_full, sin_signed, q_gain, k_gain):
    b, s, _ = y3.shape
    assert all(w // (2 * d) == A_REACH_BLOCK for w, d in A_PATTERNS)
    assert all((s // d) % 256 == 0 for _, d in A_PATTERNS) and (s // A_QUERY_BLOCK) % A_GROUP_MATMUL == 0
    kw = A_QUERY_BLOCK + 2 * A_REACH_BLOCK
    head = lambda off: pl.BlockSpec((1, s, HEAD_DIM), lambda bi, h: (bi, 0, off // HEAD_DIM + h))
    table = pl.BlockSpec((s, HEAD_DIM), lambda bi, h: (0, 0))
    vec = pl.BlockSpec((1, HEAD_DIM), lambda bi, h: (0, 0))
    pad_rows = s + 2 * A_REACH_BLOCK
    return pl.pallas_call(
        functools.partial(_dilated_kernel, seq=s),
        grid=(b, A_HEADS),
        in_specs=[head(Y_AQ), head(Y_AK), head(V_AV), table, table, vec, vec],
        out_specs=pl.BlockSpec((1, s, HEAD_DIM), lambda bi, h: (bi, 0, h)),
        out_shape=jax.ShapeDtypeStruct((b, s, A_WIDTH), BF16),
        scratch_shapes=[pltpu.VMEM((s, HEAD_DIM), F32)] * 3 + [pltpu.VMEM((pad_rows, HEAD_DIM), BF16)] * 2
        + [pltpu.VMEM((s, HEAD_DIM), F32)] * 3
        + [pltpu.VMEM((s, kw), F32), pltpu.VMEM((s, kw), BF16), pltpu.VMEM((s, HEAD_DIM), F32)]
        + [pltpu.VMEM((3, A_QUERY_BLOCK, kw), F32)],
        compiler_params=_cparams(("parallel", "parallel")),
        name="dilated_mixer",
    )(y3, y3, v3, cos_full, sin_signed, q_gain.reshape(1, HEAD_DIM), k_gain.reshape(1, HEAD_DIM))


def _gqa_kernel(q_ref, k_ref, v_ref, cosq_ref, sinq_ref, cosk_ref, sink_ref, qg_ref, kg_ref, o_ref, kn, vaug, *, group, seq):
    half = HEAD_DIM // 4
    tile = 512

    @pl.when(pl.program_id(2) == 0)
    def _():
        def prep(i, carry):
            rows = pl.ds(pl.multiple_of(i * tile, tile), tile)
            kn[rows, :] = _norm_rope(k_ref[0, rows, :], kg_ref[...], cosk_ref[rows, :], sink_ref[rows, :],
                                     half).astype(kn.dtype)
            return carry

        lax.fori_loop(0, seq // tile, prep, 0)

        vaug[:, :HEAD_DIM] = v_ref[0]
        vaug[:, HEAD_DIM:] = jnp.ones((seq, HEAD_DIM), vaug.dtype)

    k = kn[...]
    v = vaug[...]
    c = (HEAD_DIM ** -0.5) * math.log2(math.e)
    slab = B_ROW_SLAB
    units = [(g, r) for g in range(group) for r in range(q_ref.shape[1] // slab)]

    def scores(unit):
        g, r = unit
        rows = slice(r * slab, (r + 1) * slab)
        q = _norm_rope(q_ref[0, rows, g * HEAD_DIM:(g + 1) * HEAD_DIM], qg_ref[...], cosq_ref[rows, :],
                       sinq_ref[rows, :], half)
        return lax.dot_general(q.astype(BF16), k, (((1,), (1,)), ((), ())), preferred_element_type=F32)

    def probs(s):
        return jnp.exp2((s - jnp.max(s, axis=-1, keepdims=True)) * c).astype(BF16)

    def values(unit, p):
        g, r = unit
        o = jnp.dot(p, v, preferred_element_type=F32)
        o_ref[0, r * slab:(r + 1) * slab, g * HEAD_DIM:(g + 1) * HEAD_DIM] = (
            o[:, :HEAD_DIM] / o[:, HEAD_DIM:]).astype(o_ref.dtype)

    s_next = scores(units[0])
    p_prev = None
    for u, unit in enumerate(units):
        s_cur = s_next
        if u + 1 < len(units):
            s_next = scores(units[u + 1])
        if p_prev is not None:
            values(units[u - 1], p_prev)
        p_prev = probs(s_cur)
    values(units[-1], p_prev)


def gqa_attention(y3, v3, cos_full, sin_signed, q_gain, k_gain, tq=512):
    b, s, _ = y3.shape
    group = B_Q_HEADS // B_KV_HEADS
    gw = group * HEAD_DIM
    vec = pl.BlockSpec((1, HEAD_DIM), lambda bi, h, t: (0, 0))
    qtab = pl.BlockSpec((tq, HEAD_DIM), lambda bi, h, t: (t, 0))
    ktab = pl.BlockSpec((s, HEAD_DIM), lambda bi, h, t: (0, 0))
    return pl.pallas_call(
        functools.partial(_gqa_kernel, group=group, seq=s),
        grid=(b, B_KV_HEADS, s // tq),
        in_specs=[pl.BlockSpec((1, tq, gw), lambda bi, h, t: (bi, t, Y_BQ // gw + h)),
                  pl.BlockSpec((1, s, HEAD_DIM), lambda bi, h, t: (bi, 0, Y_BK // HEAD_DIM + h)),
                  pl.BlockSpec((1, s, HEAD_DIM), lambda bi, h, t: (bi, 0, V_BV // HEAD_DIM + h)),
                  qtab, qtab, ktab, ktab, vec, vec],
        out_specs=pl.BlockSpec((1, tq, gw), lambda bi, h, t: (bi, t, h)),
        out_shape=jax.ShapeDtypeStruct((b, s, B_Q_WIDTH), BF16),
        scratch_shapes=[pltpu.VMEM((s, HEAD_DIM), BF16), pltpu.VMEM((s, 2 * HEAD_DIM), BF16)],
        compiler_params=_cparams(("parallel", "parallel", "arbitrary")),
        name="gqa_attention",
    )(y3, y3, v3, cos_full, sin_signed, cos_full, sin_signed, q_gain.reshape(1, HEAD_DIM), k_gain.reshape(1, HEAD_DIM))


def _conv_kernel(xp_ref, xm_ref, xn_ref, w_ref, b_ref, qk_ref, kt_ref, buf, *, tile, ntiles):
    ti = pl.program_id(1)
    halo = 8
    buf[0:halo] = jnp.where(ti > 0, xp_ref[0], 0.0)
    buf[halo:halo + tile] = xm_ref[0]
    buf[halo + tile:] = jnp.where(ti < ntiles - 1, xn_ref[0], 0.0)
    pad = C_CONV // 2
    acc = w_ref[0:1, :] * buf[halo - pad:halo - pad + tile, :]
    for j in range(1, C_CONV):
        acc = acc + w_ref[j:j + 1, :] * buf[halo - pad + j:halo - pad + j + tile, :]
    acc = acc + b_ref[...]
    act = acc * jax.nn.sigmoid(acc)
    q = act[:, :C_QK_WIDTH]
    k = act[:, C_QK_WIDTH:] * (C_QK_DIM ** -0.5)
    qk_ref[0, :, :C_QK_WIDTH] = q.astype(qk_ref.dtype)
    qk_ref[0, :, C_QK_WIDTH:] = k.astype(qk_ref.dtype)
    for c in range(tile // MLSTM_CHUNK):
        kt_ref[0, c] = k[c * MLSTM_CHUNK:(c + 1) * MLSTM_CHUNK, :].T.astype(kt_ref.dtype)


def mlstm_conv(y3, conv_w, conv_b, tile=512):
    b, s, _ = y3.shape
    ntiles = s // tile
    w2 = 2 * C_QK_WIDTH
    cb = Y_CQK // w2
    per = tile // 8
    return pl.pallas_call(
        functools.partial(_conv_kernel, tile=tile, ntiles=ntiles),
        grid=(b, ntiles),
        in_specs=[pl.BlockSpec((1, 8, w2), lambda bi, t: (bi, jnp.maximum(t * per - 1, 0), cb)),
                  pl.BlockSpec((1, tile, w2), lambda bi, t: (bi, t, cb)),
                  pl.BlockSpec((1, 8, w2), lambda bi, t: (bi, jnp.minimum((t + 1) * per, s // 8 - 1), cb)),
                  pl.BlockSpec((C_CONV, w2), lambda bi, t: (0, 0)),
                  pl.BlockSpec((1, w2), lambda bi, t: (0, 0))],
        out_specs=[pl.BlockSpec((1, tile, w2), lambda bi, t: (bi, t, 0)),
                   pl.BlockSpec((1, tile // MLSTM_CHUNK, C_QK_WIDTH, MLSTM_CHUNK), lambda bi, t: (bi, t, 0, 0))],
        out_shape=[jax.ShapeDtypeStruct((b, s, w2), BF16),
                   jax.ShapeDtypeStruct((b, s // MLSTM_CHUNK, C_QK_WIDTH, MLSTM_CHUNK), BF16)],
        scratch_shapes=[pltpu.VMEM((tile + 16, w2), F32)],
        compiler_params=_cparams(("parallel", "parallel")),
        name="mlstm_conv",
    )(y3, y3, y3, conv_w, conv_b.reshape(1, w2))


def _gate_kernel(gi_ref, gf_ref, bi_ref, bf_ref, cols_ref, rt_ref, wkt_ref, dec_ref,
                 b_s, r_s, dmax_s, btot_s, gmax_s, mpf_s, mpb_s, *, nc):
    lc = MLSTM_CHUNK
    t_in = lax.broadcasted_iota(jnp.int32, (lc, LANES), 0)
    lane = lax.broadcasted_iota(jnp.int32, (lc, LANES), 1)
    fwd = lane % 2 == 0
    fwd_row = fwd[0:1]
    steps = [1 << i for i in range(lc.bit_length() - 1)]

    def chunk_rows(c):
        return pl.ds(pl.multiple_of(c * lc, lc), lc)

    def local(c, carry):
        rows = chunk_rows(c)
        ii = gi_ref[0, rows, :] + bi_ref[...]
        ff = gf_ref[0, rows, :] + bf_ref[...]
        lf = jnp.minimum(ff, 0.0) - jnp.log1p(jnp.exp(-jnp.abs(ff)))
        cs = lf
        rs = lf
        for k in steps:
            cs = cs + jnp.where(t_in >= k, pltpu.roll(cs, k, 0), 0.0)
            rs = rs + jnp.where(t_in + k < lc, pltpu.roll(rs, lc - k, 0), 0.0)
        b = jnp.where(fwd, cs, rs)
        btot = cs + rs - lf
        r = ii - b
        pm = r
        sm = r
        for k in steps:
            pm = jnp.where(t_in >= k, jnp.maximum(pm, pltpu.roll(pm, k, 0)), pm)
            sm = jnp.where(t_in + k < lc, jnp.maximum(sm, pltpu.roll(sm, lc - k, 0)), sm)
        b_s[rows, :] = b
        r_s[rows, :] = r
        dmax_s[rows, :] = b + jnp.where(fwd, pm, sm)
        btot_s[pl.ds(c, 1), :] = btot[0:1]
        gmax_s[pl.ds(c, 1), :] = (btot + jnp.maximum(pm, sm))[0:1]
        return carry

    lax.fori_loop(0, nc, local, 0)

    def scan(k, carry):
        mf, mb = carry
        mpf_s[pl.ds(k, 1), :] = mf
        mf = jnp.maximum(btot_s[pl.ds(k, 1), :] + mf, gmax_s[pl.ds(k, 1), :])
        kb = nc - 1 - k
        mpb_s[pl.ds(kb, 1), :] = mb
        mb = jnp.maximum(btot_s[pl.ds(kb, 1), :] + mb, gmax_s[pl.ds(kb, 1), :])
        return mf, mb

    zero = jnp.zeros((1, LANES), F32)
    lax.fori_loop(0, nc, scan, (zero, zero))

    def emit(c, carry):
        rows = chunk_rows(c)
        b = b_s[rows, :]
        r = r_s[rows, :]
        btot = btot_s[pl.ds(c, 1), :]
        mprev = jnp.where(fwd_row, mpf_s[pl.ds(c, 1), :], mpb_s[pl.ds(c, 1), :])
        mnew = jnp.maximum(btot + mprev, gmax_s[pl.ds(c, 1), :])
        dec_ref[0, pl.ds(c, 1), :] = jnp.exp(btot + mprev - mnew)
        m_inter = b + mprev
        m_t = jnp.maximum(m_inter, dmax_s[rows, :])
        cb = b - m_t
        inter = jnp.exp(m_inter - m_t)
        e = jnp.exp(-m_t)
        for p in range(C_PAIRS):
            cols_ref[0, p, rows, :] = jnp.where(
                lane < 4, pltpu.roll(cb, (LANES - 4 * p) % LANES, 1),
                jnp.where(lane < 8, pltpu.roll(inter, (LANES + 4 - 4 * p) % LANES, 1),
                          pltpu.roll(e, (LANES + 8 - 4 * p) % LANES, 1)))
        wk = jnp.exp(btot + r - mnew)
        rt_ref[0, c] = r.T[:C_STREAMS, :]
        wkt_ref[0, c] = wk.T[:C_STREAMS, :]
        return carry

    lax.fori_loop(0, nc, emit, 0)


def mlstm_gates(y3, bias_i, bias_f):
    b, s, _ = y3.shape
    nc = s // MLSTM_CHUNK
    row = pl.BlockSpec((1, nc, C_STREAMS, MLSTM_CHUNK), lambda bi: (bi, 0, 0, 0))
    vec = pl.BlockSpec((1, LANES), lambda bi: (0, 0))
    return pl.pallas_call(
        functools.partial(_gate_kernel, nc=nc),
        grid=(b,),
        in_specs=[pl.BlockSpec((1, s, LANES), lambda bi: (bi, 0, Y_GI // LANES)),
                  pl.BlockSpec((1, s, LANES), lambda bi: (bi, 0, Y_GF // LANES)), vec, vec],
        out_specs=[pl.BlockSpec((1, C_PAIRS, s, LANES), lambda bi: (bi, 0, 0, 0)), row, row,
                   pl.BlockSpec((1, nc, LANES), lambda bi: (bi, 0, 0))],
        out_shape=[jax.ShapeDtypeStruct((b, C_PAIRS, s, LANES), F32)]
        + [jax.ShapeDtypeStruct((b, nc, C_STREAMS, MLSTM_CHUNK), F32)] * 2
        + [jax.ShapeDtypeStruct((b, nc, LANES), F32)],
        scratch_shapes=[pltpu.VMEM((s, LANES), F32)] * 3 + [pltpu.VMEM((nc, LANES), F32)] * 4,
        compiler_params=_cparams(("parallel",)),
        name="mlstm_gates",
    )(y3, y3, bias_i.reshape(1, LANES), bias_f.reshape(1, LANES))


def _mlstm_kernel(dec_ref, q_ref, kt_ref, v_ref, o_ref, g_ref, cols_ref, rt_ref, wkt_ref,
                  y_ref, akw, hsum, ct, *, seq, nc):
    lc = MLSTM_CHUNK
    bi = pl.program_id(0)
    pair = pl.program_id(1)
    lane = lax.broadcasted_iota(jnp.int32, (lc, LANES), 1)
    trow = lax.broadcasted_iota(jnp.int32, (lc, lc), 0)
    scol = lax.broadcasted_iota(jnp.int32, (lc, lc), 1)
    ones = jnp.ones((lc, C_V_DIM), BF16)

    def head_q(q2, hh):
        return jnp.where((lane >= hh * C_QK_DIM) & (lane < (hh + 1) * C_QK_DIM), q2, jnp.zeros_like(q2))

    def intra(c, carry):
        rows = pl.ds(pl.multiple_of(c * lc, lc), lc)
        q2 = q_ref[0, rows, :]
        kt2 = kt_ref[0, c]
        for hh in range(2):
            sc = jnp.dot(head_q(q2, hh), kt2, preferred_element_type=F32)
            kth = kt2[hh * C_QK_DIM:(hh + 1) * C_QK_DIM, :].astype(F32)
            for direction in range(2):
                sl = hh * 2 + direction
                stream = pair * 4 + sl
                mask = (scol <= trow) if direction == 0 else (scol >= trow)
                arg = cols_ref[0, 0, rows, sl:sl + 1] + rt_ref[0, c, pl.ds(stream, 1), :]
                akw[sl, c, :lc, :] = (jnp.exp(jnp.where(mask, arg, NEG_INF)) * sc).astype(BF16)
                akw[sl, c, lc:, :] = (kth * wkt_ref[0, c, pl.ds(stream, 1), :]).astype(BF16)
        return carry

    lax.fori_loop(0, nc, intra, 0, unroll=2)

    ct[...] = jnp.zeros_like(ct)
    hsum[...] = jnp.zeros_like(hsum)

    def step(it, carry):
        work = []
        for direction in range(2):
            c = it if direction == 0 else nc - 1 - it
            rows = pl.ds(pl.multiple_of(c * lc, lc), lc)
            q2 = q_ref[0, rows, :]
            ct_pair = ct[direction].astype(BF16)
            for hh in range(2):
                sl = hh * 2 + direction
                vaug = jnp.concatenate([v_ref[0, rows, hh * C_V_DIM:(hh + 1) * C_V_DIM], ones], axis=1)
                num_a = jnp.dot(akw[sl, c, :lc, :], vaug, preferred_element_type=F32)
                num_c = jnp.dot(head_q(q2, hh), ct_pair, preferred_element_type=F32)
                u = jnp.dot(akw[sl, c, lc:, :], vaug, preferred_element_type=F32)
                work.append((direction, hh, sl, c, rows, num_a, num_c, u))
        for direction, hh, sl, c, rows, num_a, num_c, u in work:
            dec = dec_ref[(bi * nc + c) * C_STREAMS + pair * 4 + sl]
            num = num_a + cols_ref[0, 0, rows, 4 + sl:5 + sl] * num_c
            den = jnp.maximum(jnp.abs(num[:, C_V_DIM:]), cols_ref[0, 0, rows, 8 + sl:9 + sl])
            hsum[rows, hh * C_V_DIM:(hh + 1) * C_V_DIM] += num[:, :C_V_DIM] / den
            d_rows = slice(hh * C_QK_DIM, (hh + 1) * C_QK_DIM)
            ct[direction, d_rows, :] = dec * ct[direction, d_rows, :] + u
        return carry

    lax.fori_loop(0, nc, step, 0, unroll=2)

    tile = 512

    def finish(i, carry):
        rows = pl.ds(pl.multiple_of(i * tile, tile), tile)
        for hh in range(2):
            cols = slice(hh * C_V_DIM, (hh + 1) * C_V_DIM)
            h = hsum[rows, cols]
            h = h * lax.rsqrt(jnp.mean(h * h, axis=-1, keepdims=True) + EPS)
            h = h * g_ref[:, cols]
            y_ref[0, rows, cols] = (h * jax.nn.sigmoid(o_ref[0, rows, cols])).astype(y_ref.dtype)
        return carry

    lax.fori_loop(0, seq // tile, finish, 0)


def mlstm_scan(dec, qk, kt, v3, y3, out_g, cols, rt, wkt):
    b, s, _ = qk.shape
    nc = s // MLSTM_CHUNK
    pw = 2 * C_V_DIM
    col = pl.BlockSpec((1, 1, s, LANES), lambda bi, p, d: (bi, p, 0, 0))
    row = pl.BlockSpec((1, nc, C_STREAMS, MLSTM_CHUNK), lambda bi, p, d: (bi, 0, 0, 0))
    grid_spec = pltpu.PrefetchScalarGridSpec(
        num_scalar_prefetch=1,
        grid=(b, C_PAIRS),
        in_specs=[pl.BlockSpec((1, s, LANES), lambda bi, p, d: (bi, 0, p)),
                  pl.BlockSpec((1, nc, LANES, MLSTM_CHUNK), lambda bi, p, d: (bi, 0, p, 0)),
                  pl.BlockSpec((1, s, pw), lambda bi, p, d: (bi, 0, V_CV // pw + p)),
                  pl.BlockSpec((1, s, pw), lambda bi, p, d: (bi, 0, Y_CO // pw + p)),
                  pl.BlockSpec((1, pw), lambda bi, p, d: (0, p)),
                  col, row, row],
        out_specs=pl.BlockSpec((1, s, pw), lambda bi, p, d: (bi, 0, p)),
        scratch_shapes=[pltpu.VMEM((4, nc, MLSTM_CHUNK + C_QK_DIM, MLSTM_CHUNK), BF16),
                        pltpu.VMEM((s, pw), F32),
                        pltpu.VMEM((2, LANES, 2 * C_V_DIM), F32)],
    )
    return pl.pallas_call(
        functools.partial(_mlstm_kernel, seq=s, nc=nc),
        grid_spec=grid_spec,
        out_shape=jax.ShapeDtypeStruct((b, s, C_HEADS * C_V_DIM), BF16),
        compiler_params=_cparams(("parallel", "arbitrary")),
        name="mlstm_scan",
    )(dec, qk, kt, v3, y3, out_g.reshape(1, C_V_WIDTH), cols, rt, wkt)


def _merge_kernel(ya_ref, yb_ref, yc_ref, ga_ref, gb_ref, gc_ref, wa_ref, wb_ref, wc_ref, o_ref):
    acc = jax.nn.sigmoid(ga_ref[...]) * jnp.dot(ya_ref[...], wa_ref[...], preferred_element_type=F32)
    acc = acc + jax.nn.sigmoid(gb_ref[...]) * jnp.dot(yb_ref[...], wb_ref[...], preferred_element_type=F32)
    acc = acc + jax.nn.sigmoid(gc_ref[...]) * jnp.dot(yc_ref[...], wc_ref[...], preferred_element_type=F32)
    o_ref[...] = acc.astype(o_ref.dtype)


def branch_merge(ya, yb, yc, y2, wa, wb, wc, tm=512, tn=1024):
    m = ya.shape[0]
    n = wa.shape[1]
    nj = n // tn

    def act(width):
        return pl.BlockSpec((tm, width), lambda j, i: (i, 0))

    def gate(branch):
        return pl.BlockSpec((tm, tn), lambda j, i: (i, Y_G // tn + branch * nj + j))

    def weight(width):
        return pl.BlockSpec((width, tn), lambda j, i: (0, j))

    return pl.pallas_call(
        _merge_kernel,
        grid=(nj, m // tm),
        in_specs=[act(A_WIDTH), act(B_Q_WIDTH), act(C_V_WIDTH), gate(0), gate(1), gate(2),
                  weight(A_WIDTH), weight(B_Q_WIDTH), weight(C_V_WIDTH)],
        out_specs=pl.BlockSpec((tm, tn), lambda j, i: (i, j)),
        out_shape=jax.ShapeDtypeStruct((m, n), BF16),
        compiler_params=_cparams(("parallel", "parallel")),
        name="branch_merge",
    )(ya, yb, yc, y2, y2, y2, wa, wb, wc)


def _rope_tables(pos, dim):
    inv_freq = ROPE_THETA ** (-jnp.arange(0, dim, 2, dtype=F32) / dim)
    ang = pos.astype(F32)[:, None] * inv_freq[None, :]
    return jnp.cos(ang), jnp.sin(ang)


def _stream_gate_columns(w_gate, b_gate):
    lanes = jnp.arange(C_STREAMS)
    head = (lanes // 4) * 2 + (lanes % 4) // 2
    direction = lanes % 2
    i_cols = direction * 2 * C_HEADS + head
    f_cols = i_cols + C_HEADS
    pad = LANES - C_STREAMS
    wi = jnp.pad(w_gate[:, i_cols], ((0, 0), (0, pad)))
    wf = jnp.pad(w_gate[:, f_cols], ((0, 0), (0, pad)))
    return wi, wf, jnp.pad(b_gate[i_cols], (0, pad)), jnp.pad(b_gate[f_cols], (0, pad))


def _layer(x2, shape, tables, norm1_g, w_in, a_q_gain, a_k_gain, b_q_gain, b_k_gain, c_conv_w, c_conv_b,
           c_gate_b, c_out_g, w_branch_a, w_branch_b, w_branch_c, w_out, norm2_g, w_up, w_down):
    bsz, seq = shape
    cos_a, sin_a, cos_b, sin_b = tables
    sizes = (A_WIDTH, A_WIDTH, A_WIDTH, B_Q_WIDTH, B_KV_WIDTH, B_KV_WIDTH, 2 * C_QK_WIDTH, C_V_WIDTH, C_V_WIDTH,
             4 * C_HEADS, N_BRANCHES * x2.shape[1])
    offs = [0]
    for sz in sizes:
        offs.append(offs[-1] + sz)
    seg = lambda i: w_in[:, offs[i]:offs[i + 1]]
    wi, wf, bias_i, bias_f = _stream_gate_columns(seg(9), c_gate_b)
    w_y = jnp.concatenate([seg(10), seg(6), seg(8), seg(3), seg(0), seg(1), seg(4), wi, wf], axis=1).astype(BF16)
    w_v = jnp.concatenate([seg(2), seg(5), seg(7)], axis=1).astype(BF16)

    xn = rmsnorm(x2, norm1_g)
    y2 = matmul(xn, w_y, out_dtype=F32, tm=2048)
    v2 = matmul(xn, w_v, out_dtype=BF16, tm=2048)
    y3 = y2.reshape(bsz, seq, Y_COLS)
    v3 = v2.reshape(bsz, seq, V_COLS)

    ya = dilated_mixer(y3, v3, cos_a, sin_a, a_q_gain, a_k_gain).reshape(bsz * seq, A_WIDTH)
    yb = gqa_attention(y3, v3, cos_b, sin_b, b_q_gain, b_k_gain).reshape(bsz * seq, B_Q_WIDTH)

    qk_c, kt_c = mlstm_conv(y3, c_conv_w, c_conv_b)
    cols, rt, wkt, dec = mlstm_gates(y3, bias_i, bias_f)
    dec_flat = dec[:, :, :C_STREAMS].reshape(-1)
    yc = mlstm_scan(dec_flat, qk_c, kt_c, v3, y3, c_out_g, cols, rt, wkt).reshape(bsz * seq, C_V_WIDTH)

    merged = branch_merge(ya, yb, yc, y2, w_branch_a.astype(BF16), w_branch_b.astype(BF16), w_branch_c.astype(BF16))
    h, hn = matmul_residual_norm(merged, w_out.astype(BF16), x2, norm2_g)
    u2 = matmul(hn, w_up.astype(BF16), out_dtype=BF16, relu2=True, tm=2048)
    return matmul(u2, w_down.astype(BF16), out_dtype=F32, res=h, tm=1024, tn=1024, tk=2048)


def kernel(x, norm1_g, w_in, a_q_gain, a_k_gain, b_q_gain, b_k_gain, c_conv_w, c_conv_b, c_gate_b, c_out_g, w_branch_a, w_branch_b, w_branch_c, w_out, norm2_g, w_up, w_down):
    bsz, seq, d_model = x.shape
    pos = jnp.arange(seq)
    cos1, sin1 = _rope_tables(pos, HEAD_DIM)
    cos_r, sin_r = _rope_tables(pos // GRID_W, HEAD_DIM // 2)
    cos_c, sin_c = _rope_tables(pos % GRID_W, HEAD_DIM // 2)
    tables = (jnp.concatenate([cos1, cos1], axis=1), jnp.concatenate([-sin1, sin1], axis=1),
              jnp.concatenate([cos_r, cos_r, cos_c, cos_c], axis=1),
              jnp.concatenate([-sin_r, sin_r, -sin_c, sin_c], axis=1))
    x2 = x.reshape(bsz * seq, d_model)
    for l in range(norm1_g.shape[0]):
        x2 = _layer(x2, (bsz, seq), tables, norm1_g[l], w_in[l], a_q_gain[l], a_k_gain[l], b_q_gain[l], b_k_gain[l],
                    c_conv_w[l], c_conv_b[l], c_gate_b[l], c_out_g[l], w_branch_a[l], w_branch_b[l], w_branch_c[l],
                    w_out[l], norm2_g[l], w_up[l], w_down[l])
    return x2.reshape(bsz, seq, d_model)
```

```python
import functools
import math

import jax
import jax.numpy as jnp
from jax import lax
from jax.experimental import pallas as pl
from jax.experimental.pallas import tpu as pltpu

F32 = jnp.float32
BF16 = jnp.bfloat16

HEAD_DIM = 128
GRID_W = 64
ROPE_THETA = 10000.0
EPS = 1e-6
NEG_INF = -1e30

A_HEADS = 6
A_PATTERNS = ((128, 1), (512, 4), (2048, 16))
A_REACH_BLOCK = 64
A_QUERY_BLOCK = 128
A_GROUP = 4
B_Q_HEADS = 8
B_KV_HEADS = 2
B_ROW_SLAB = 256
C_HEADS = 8
C_QK_DIM = 64
C_V_DIM = 128
C_CONV = 5
N_BRANCHES = 3

A_WIDTH = A_HEADS * HEAD_DIM
B_Q_WIDTH = B_Q_HEADS * HEAD_DIM
B_KV_WIDTH = B_KV_HEADS * HEAD_DIM
C_QK_WIDTH = C_HEADS * C_QK_DIM
C_V_WIDTH = C_HEADS * C_V_DIM

LANES = 128
MLSTM_CHUNK = 128
C_PAIRS = C_HEADS // 2
C_STREAMS = 2 * C_HEADS
VMEM_LIMIT = 60 * 1024 * 1024

Y_G = 0
Y_CQK = 6144
Y_CO = 7168
Y_BQ = 8192
Y_AQ = 9216
Y_AK = 9984
Y_BK = 10752
Y_GI = 11008
Y_GF = 11136
Y_COLS = 11264
V_AV = 0
V_BV = 768
V_CV = 1024
V_COLS = 2048


def _cparams(sem):
    return pltpu.CompilerParams(dimension_semantics=sem, vmem_limit_bytes=VMEM_LIMIT)


def _rmsnorm_kernel(x_ref, g_ref, o_ref):
    x = x_ref[...]
    r = lax.rsqrt(jnp.mean(x * x, axis=-1, keepdims=True) + EPS)
    o_ref[...] = ((x * r) * g_ref[...]).astype(o_ref.dtype)


def rmsnorm(x, g, tm=512):
    m, d = x.shape
    return pl.pallas_call(
        _rmsnorm_kernel,
        grid=(m // tm,),
        in_specs=[pl.BlockSpec((tm, d), lambda i: (i, 0)), pl.BlockSpec((1, d), lambda i: (0, 0))],
        out_specs=pl.BlockSpec((tm, d), lambda i: (i, 0)),
        out_shape=jax.ShapeDtypeStruct((m, d), BF16),
        compiler_params=_cparams(("parallel",)),
        name="rmsnorm",
    )(x, g.reshape(1, d))


def _matmul_kernel(*refs, nk, relu2, has_res):
    if has_res:
        a_ref, w_ref, r_ref, o_ref = refs
    else:
        a_ref, w_ref, o_ref = refs
        r_ref = None

    prod = jnp.dot(a_ref[...], w_ref[...], preferred_element_type=F32)
    if relu2:
        u = jnp.maximum(prod, 0.0)
        prod = u * u
    if nk == 1:
        o_ref[...] = (prod if r_ref is None else r_ref[...] + prod).astype(o_ref.dtype)
    else:
        k = pl.program_id(2)

        @pl.when(k == 0)
        def _():
            o_ref[...] = prod if r_ref is None else r_ref[...] + prod

        @pl.when(k > 0)
        def _():
            o_ref[...] += prod


def matmul(a, w, *, out_dtype, res=None, relu2=False, tm=1024, tn=1024, tk=2048):
    m, kdim = a.shape
    n = w.shape[1]
    tn = min(tn, n)
    tk = min(tk, kdim)
    nk = kdim // tk
    assert nk == 1 or (out_dtype == F32 and not relu2)
    in_specs = [pl.BlockSpec((tm, tk), lambda i, j, k: (i, k)),
                pl.BlockSpec((tk, tn), lambda i, j, k: (k, j))]
    args = [a, w]
    if res is not None:
        in_specs.append(pl.BlockSpec((tm, tn), lambda i, j, k: (i, j)))
        args.append(res)
    return pl.pallas_call(
        functools.partial(_matmul_kernel, nk=nk, relu2=relu2, has_res=res is not None),
        grid=(m // tm, n // tn, nk),
        in_specs=in_specs,
        out_specs=pl.BlockSpec((tm, tn), lambda i, j, k: (i, j)),
        out_shape=jax.ShapeDtypeStruct((m, n), out_dtype),
        compiler_params=_cparams(("parallel", "parallel", "arbitrary")),
        name="matmul",
    )(*args)


def _out_norm_kernel(a_ref, w_ref, r_ref, g_ref, h_ref, hn_ref):
    h = r_ref[...] + jnp.dot(a_ref[...], w_ref[...], preferred_element_type=F32)
    h_ref[...] = h
    r = lax.rsqrt(jnp.mean(h * h, axis=-1, keepdims=True) + EPS)
    hn_ref[...] = ((h * r) * g_ref[...]).astype(hn_ref.dtype)


def matmul_residual_norm(a, w, res, g, tm=512):
    m, kdim = a.shape
    n = w.shape[1]
    row = lambda width: pl.BlockSpec((tm, width), lambda i: (i, 0))
    return pl.pallas_call(
        _out_norm_kernel,
        grid=(m // tm,),
        in_specs=[row(kdim), pl.BlockSpec((kdim, n), lambda i: (0, 0)), row(n), pl.BlockSpec((1, n), lambda i: (0, 0))],
        out_specs=[row(n), row(n)],
        out_shape=[jax.ShapeDtypeStruct((m, n), F32), jax.ShapeDtypeStruct((m, n), BF16)],
        compiler_params=_cparams(("parallel",)),
        name="matmul_residual_norm",
    )(a, w, res, g.reshape(1, n))


def _norm_rope(x, gain, cos_full, sin_signed, half):
    y = (x * lax.rsqrt(jnp.mean(x * x, axis=-1, keepdims=True) + EPS)) * gain
    if 2 * half == LANES:
        partner = pltpu.roll(y, half, 1)
    else:
        lane = lax.broadcasted_iota(jnp.int32, y.shape, 1)
        partner = jnp.where(lane % (2 * half) < half, pltpu.roll(y, LANES - half, 1), pltpu.roll(y, half, 1))
    return y * cos_full + partner * sin_signed


def _dilated_kernel(q_ref, k_ref, v_ref, cos_ref, sin_ref, qg_ref, kg_ref, y_ref,
                    qn, kn, vn, kd, vd, acc, mrun, zrun, sbuf, pbuf, mblk, bias, *, seq):
    blk = A_REACH_BLOCK
    qb = A_QUERY_BLOCK
    kw = qb + 2 * blk
    grp = A_GROUP
    tile = 512
    scale = HEAD_DIM ** -0.5

    def prep(i, carry):
        rows = pl.ds(pl.multiple_of(i * tile, tile), tile)
        cos = cos_ref[rows, :]
        sin = sin_ref[rows, :]
        qn[rows, :] = _norm_rope(q_ref[0, rows, :], qg_ref[...], cos, sin, HEAD_DIM // 2)
        kn[rows, :] = _norm_rope(k_ref[0, rows, :], kg_ref[...], cos, sin, HEAD_DIM // 2)
        vn[rows, :] = v_ref[0, rows, :].astype(F32)
        return carry

    lax.fori_loop(0, seq // tile, prep, 0)
    zpad = jnp.zeros((blk, HEAD_DIM), BF16)
    kd[0:blk, :] = zpad
    kd[blk + seq:, :] = zpad
    vd[0:blk, :] = zpad
    vd[blk + seq:, :] = zpad

    t_io = lax.broadcasted_iota(jnp.int32, (qb, kw), 0)
    j_io = lax.broadcasted_iota(jnp.int32, (qb, kw), 1)
    in_reach = jnp.abs(j_io - blk - t_io) <= blk
    bias[0] = jnp.where(in_reach, 0.0, NEG_INF)
    bias[1] = jnp.where(in_reach & (j_io >= blk), 0.0, NEG_INF)
    bias[2] = jnp.where(in_reach & (j_io < qb + blk), 0.0, NEG_INF)
    nblk = seq // qb
    ones = jnp.ones((kw, HEAD_DIM), BF16)

    for pi, (_, d) in enumerate(sorted(A_PATTERNS, key=lambda wd: -wd[1])):
        sub_len = seq // d
        chunk = 256
        per_res = sub_len // chunk
        blk_per_res = sub_len // qb
        first = pi == 0

        def gather(idx, carry, d=d, sub_len=sub_len, per_res=per_res):
            r = idx // per_res
            j = idx % per_res
            if d == 1:
                src = pl.ds(pl.multiple_of(j * chunk, chunk), chunk)
            else:
                src = pl.ds(r + j * chunk * d, chunk, stride=d)
            dst = pl.ds(pl.multiple_of(blk + r * sub_len + j * chunk, blk), chunk)
            kd[dst, :] = kn[src, :].astype(BF16)
            vd[dst, :] = vn[src, :].astype(BF16)
            return carry

        lax.fori_loop(0, seq // chunk, gather, 0)

        def natural_rows(n, d=d, blk_per_res=blk_per_res):
            if d == 1:
                return pl.ds(pl.multiple_of(n * qb, qb), qb)
            return pl.ds(n // blk_per_res + (n % blk_per_res) * qb * d, qb, stride=d)

        def packed_rows(n, size=qb):
            return pl.ds(pl.multiple_of(n * qb, qb), size)

        def scores(t, blk_per_res=blk_per_res):
            for g in range(grp):
                n = t * grp + g
                q = qn[natural_rows(n), :].astype(BF16)
                s = lax.dot_general(q, kd[packed_rows(n, kw), :], (((1,), (1,)), ((), ())),
                                    preferred_element_type=F32) * scale
                i = n % blk_per_res
                kind = jnp.where(i == 0, 1, jnp.where(i == blk_per_res - 1, 2, 0))
                sbuf[packed_rows(n), :] = s + bias[kind]

        def softmax(t):
            for g in range(grp):
                rows = packed_rows(t * grp + g)
                s = sbuf[rows, :]
                m = jnp.max(s, axis=-1, keepdims=True)
                pbuf[rows, :] = jnp.exp(s - m).astype(BF16)
                mblk[rows, :] = jnp.broadcast_to(m, (qb, HEAD_DIM))

        def values(t, first=first):
            for g in range(grp):
                n = t * grp + g
                vaug = jnp.concatenate([vd[packed_rows(n, kw), :], ones], axis=1)
                oz = jnp.dot(pbuf[packed_rows(n), :], vaug, preferred_element_type=F32)
                o = oz[:, :HEAD_DIM]
                z = oz[:, HEAD_DIM:]
                if first:
                    rw = natural_rows(n)
                    acc[rw, :] = o
                    zrun[rw, :] = z
                    mrun[rw, :] = mblk[packed_rows(n), :]
                else:
                    rw = natural_rows(n)
                    m_old = mrun[rw, :]
                    mb = mblk[packed_rows(n), :]
                    m_new = jnp.maximum(m_old, mb)
                    w_old = jnp.exp(m_old - m_new)
                    w_blk = jnp.exp(mb - m_new)
                    acc[rw, :] = w_old * acc[rw, :] + w_blk * o
                    zrun[rw, :] = w_old * zrun[rw, :] + w_blk * z
                    mrun[rw, :] = m_new

        ngroups = nblk // grp
        scores(0)
        softmax(0)
        scores(1)

        def trip(t, carry):
            values(t - 2)
            softmax(t - 1)
            scores(t)
            return carry

        lax.fori_loop(2, ngroups, trip, 0)
        values(ngroups - 2)
        softmax(ngroups - 1)
        values(ngroups - 1)

    def finish(i, carry):
        rows = pl.ds(pl.multiple_of(i * tile, tile), tile)
        y_ref[0, rows, :] = (acc[rows, :] / zrun[rows, :]).astype(y_ref.dtype)
        return carry

    lax.fori_loop(0, seq // tile, finish, 0)


def dilated_mixer(y3, v3, cos_full, sin_signed, q_gain, k_gain):
    b, s, _ = y3.shape
    assert all(w // (2 * d) == A_REACH_BLOCK for w, d in A_PATTERNS)
    assert all((s // d) % 256 == 0 for _, d in A_PATTERNS) and (s // A_QUERY_BLOCK) % A_GROUP == 0 and s // A_QUERY_BLOCK >= 3 * A_GROUP
    kw = A_QUERY_BLOCK + 2 * A_REACH_BLOCK
    head = lambda off: pl.BlockSpec((1, s, HEAD_DIM), lambda bi, h: (bi, 0, off // HEAD_DIM + h))
    table = pl.BlockSpec((s, HEAD_DIM), lambda bi, h: (0, 0))
    vec = pl.BlockSpec((1, HEAD_DIM), lambda bi, h: (0, 0))
    pad_rows = s + 2 * A_REACH_BLOCK
    return pl.pallas_call(
        functools.partial(_dilated_kernel, seq=s),
        grid=(b, A_HEADS),
        in_specs=[head(Y_AQ), head(Y_AK), head(V_AV), table, table, vec, vec],
        out_specs=pl.BlockSpec((1, s, HEAD_DIM), lambda bi, h: (bi, 0, h)),
        out_shape=jax.ShapeDtypeStruct((b, s, A_WIDTH), BF16),
        scratch_shapes=[pltpu.VMEM((s, HEAD_DIM), F32)] * 3 + [pltpu.VMEM((pad_rows, HEAD_DIM), BF16)] * 2
        + [pltpu.VMEM((s, HEAD_DIM), F32)] * 3
        + [pltpu.VMEM((s, kw), F32), pltpu.VMEM((s, kw), BF16), pltpu.VMEM((s, HEAD_DIM), F32)]
        + [pltpu.VMEM((3, A_QUERY_BLOCK, kw), F32)],
        compiler_params=_cparams(("parallel", "parallel")),
        name="dilated_mixer",
    )(y3, y3, v3, cos_full, sin_signed, q_gain.reshape(1, HEAD_DIM), k_gain.reshape(1, HEAD_DIM))


def _gqa_kernel(q_ref, k_ref, v_ref, cosq_ref, sinq_ref, cosk_ref, sink_ref, qg_ref, kg_ref, o_ref, kn, vaug, *, group, seq):
    half = HEAD_DIM // 4
    tile = 512

    @pl.when(pl.program_id(2) == 0)
    def _():
        def prep(i, carry):
            rows = pl.ds(pl.multiple_of(i * tile, tile), tile)
            kn[rows, :] = _norm_rope(k_ref[0, rows, :], kg_ref[...], cosk_ref[rows, :], sink_ref[rows, :],
                                     half).astype(kn.dtype)
            return carry

        lax.fori_loop(0, seq // tile, prep, 0)

        vaug[:, :HEAD_DIM] = v_ref[0]
        vaug[:, HEAD_DIM:] = jnp.ones((seq, HEAD_DIM), vaug.dtype)

    k = kn[...]
    v = vaug[...]
    c = (HEAD_DIM ** -0.5) * math.log2(math.e)
    slab = B_ROW_SLAB
    units = [(g, r) for g in range(group) for r in range(q_ref.shape[1] // slab)]

    def scores(unit):
        g, r = unit
        rows = slice(r * slab, (r + 1) * slab)
        q = _norm_rope(q_ref[0, rows, g * HEAD_DIM:(g + 1) * HEAD_DIM], qg_ref[...], cosq_ref[rows, :],
                       sinq_ref[rows, :], half)
        return lax.dot_general(q.astype(BF16), k, (((1,), (1,)), ((), ())), preferred_element_type=F32)

    def probs(s):
        return jnp.exp2((s - jnp.max(s, axis=-1, keepdims=True)) * c).astype(BF16)

    def values(unit, p):
        g, r = unit
        o = jnp.dot(p, v, preferred_element_type=F32)
        o_ref[0, r * slab:(r + 1) * slab, g * HEAD_DIM:(g + 1) * HEAD_DIM] = (
            o[:, :HEAD_DIM] / o[:, HEAD_DIM:]).astype(o_ref.dtype)

    s_next = scores(units[0])
    p_prev = None
    for u, unit in enumerate(units):
        s_cur = s_next
        if u + 1 < len(units):
            s_next = scores(units[u + 1])
        if p_prev is not None:
            values(units[u - 1], p_prev)
        p_prev = probs(s_cur)
    values(units[-1], p_prev)


def gqa_attention(y3, v3, cos_full, sin_signed, q_gain, k_gain, tq=512):
    b, s, _ = y3.shape
    group = B_Q_HEADS // B_KV_HEADS
    gw = group * HEAD_DIM
    vec = pl.BlockSpec((1, HEAD_DIM), lambda bi, h, t: (0, 0))
    qtab = pl.BlockSpec((tq, HEAD_DIM), lambda bi, h, t: (t, 0))
    ktab = pl.BlockSpec((s, HEAD_DIM), lambda bi, h, t: (0, 0))
    return pl.pallas_call(
        functools.partial(_gqa_kernel, group=group, seq=s),
        grid=(b, B_KV_HEADS, s // tq),
        in_specs=[pl.BlockSpec((1, tq, gw), lambda bi, h, t: (bi, t, Y_BQ // gw + h)),
                  pl.BlockSpec((1, s, HEAD_DIM), lambda bi, h, t: (bi, 0, Y_BK // HEAD_DIM + h)),
                  pl.BlockSpec((1, s, HEAD_DIM), lambda bi, h, t: (bi, 0, V_BV // HEAD_DIM + h)),
                  qtab, qtab, ktab, ktab, vec, vec],
        out_specs=pl.BlockSpec((1, tq, gw), lambda bi, h, t: (bi, t, h)),
        out_shape=jax.ShapeDtypeStruct((b, s, B_Q_WIDTH), BF16),
        scratch_shapes=[pltpu.VMEM((s, HEAD_DIM), BF16), pltpu.VMEM((s, 2 * HEAD_DIM), BF16)],
        compiler_params=_cparams(("parallel", "parallel", "arbitrary")),
        name="gqa_attention",
    )(y3, y3, v3, cos_full, sin_signed, cos_full, sin_signed, q_gain.reshape(1, HEAD_DIM), k_gain.reshape(1, HEAD_DIM))


def _conv_kernel(xp_ref, xm_ref, xn_ref, w_ref, b_ref, qk_ref, kt_ref, buf, *, tile, ntiles):
    ti = pl.program_id(1)
    halo = 8
    buf[0:halo] = jnp.where(ti > 0, xp_ref[0], 0.0)
    buf[halo:halo + tile] = xm_ref[0]
    buf[halo + tile:] = jnp.where(ti < ntiles - 1, xn_ref[0], 0.0)
    pad = C_CONV // 2
    acc = w_ref[0:1, :] * buf[halo - pad:halo - pad + tile, :]
    for j in range(1, C_CONV):
        acc = acc + w_ref[j:j + 1, :] * buf[halo - pad + j:halo - pad + j + tile, :]
    acc = acc + b_ref[...]
    act = acc * jax.nn.sigmoid(acc)
    q = act[:, :C_QK_WIDTH]
    k = act[:, C_QK_WIDTH:] * (C_QK_DIM ** -0.5)
    qk_ref[0, :, :C_QK_WIDTH] = q.astype(qk_ref.dtype)
    qk_ref[0, :, C_QK_WIDTH:] = k.astype(qk_ref.dtype)
    for c in range(tile // MLSTM_CHUNK):
        kt_ref[0, c] = k[c * MLSTM_CHUNK:(c + 1) * MLSTM_CHUNK, :].T.astype(kt_ref.dtype)


def mlstm_conv(y3, conv_w, conv_b, tile=512):
    b, s, _ = y3.shape
    ntiles = s // tile
    w2 = 2 * C_QK_WIDTH
    cb = Y_CQK // w2
    per = tile // 8
    return pl.pallas_call(
        functools.partial(_conv_kernel, tile=tile, ntiles=ntiles),
        grid=(b, ntiles),
        in_specs=[pl.BlockSpec((1, 8, w2), lambda bi, t: (bi, jnp.maximum(t * per - 1, 0), cb)),
                  pl.BlockSpec((1, tile, w2), lambda bi, t: (bi, t, cb)),
                  pl.BlockSpec((1, 8, w2), lambda bi, t: (bi, jnp.minimum((t + 1) * per, s // 8 - 1), cb)),
                  pl.BlockSpec((C_CONV, w2), lambda bi, t: (0, 0)),
                  pl.BlockSpec((1, w2), lambda bi, t: (0, 0))],
        out_specs=[pl.BlockSpec((1, tile, w2), lambda bi, t: (bi, t, 0)),
                   pl.BlockSpec((1, tile // MLSTM_CHUNK, C_QK_WIDTH, MLSTM_CHUNK), lambda bi, t: (bi, t, 0, 0))],
        out_shape=[jax.ShapeDtypeStruct((b, s, w2), BF16),
                   jax.ShapeDtypeStruct((b, s // MLSTM_CHUNK, C_QK_WIDTH, MLSTM_CHUNK), BF16)],
        scratch_shapes=[pltpu.VMEM((tile + 16, w2), F32)],
        compiler_params=_cparams(("parallel", "parallel")),
        name="mlstm_conv",
    )(y3, y3, y3, conv_w, conv_b.reshape(1, w2))


def _gate_kernel(gi_ref, gf_ref, bi_ref, bf_ref, cols_ref, rt_ref, wkt_ref, dec_ref,
                 b_s, r_s, dmax_s, btot_s, gmax_s, mpf_s, mpb_s, *, nc):
    lc = MLSTM_CHUNK
    t_in = lax.broadcasted_iota(jnp.int32, (lc, LANES), 0)
    lane = lax.broadcasted_iota(jnp.int32, (lc, LANES), 1)
    fwd = lane % 2 == 0
    fwd_row = fwd[0:1]
    steps = [1 << i for i in range(lc.bit_length() - 1)]

    def chunk_rows(c):
        return pl.ds(pl.multiple_of(c * lc, lc), lc)

    def local(c, carry):
        rows = chunk_rows(c)
        ii = gi_ref[0, rows, :] + bi_ref[...]
        ff = gf_ref[0, rows, :] + bf_ref[...]
        lf = jnp.minimum(ff, 0.0) - jnp.log1p(jnp.exp(-jnp.abs(ff)))
        cs = lf
        rs = lf
        for k in steps:
            cs = cs + jnp.where(t_in >= k, pltpu.roll(cs, k, 0), 0.0)
            rs = rs + jnp.where(t_in + k < lc, pltpu.roll(rs, lc - k, 0), 0.0)
        b = jnp.where(fwd, cs, rs)
        btot = cs + rs - lf
        r = ii - b
        pm = r
        sm = r
        for k in steps:
            pm = jnp.where(t_in >= k, jnp.maximum(pm, pltpu.roll(pm, k, 0)), pm)
            sm = jnp.where(t_in + k < lc, jnp.maximum(sm, pltpu.roll(sm, lc - k, 0)), sm)
        b_s[rows, :] = b
        r_s[rows, :] = r
        dmax_s[rows, :] = b + jnp.where(fwd, pm, sm)
        btot_s[pl.ds(c, 1), :] = btot[0:1]
        gmax_s[pl.ds(c, 1), :] = (btot + jnp.maximum(pm, sm))[0:1]
        return carry

    lax.fori_loop(0, nc, local, 0)

    def scan(k, carry):
        mf, mb = carry
        mpf_s[pl.ds(k, 1), :] = mf
        mf = jnp.maximum(btot_s[pl.ds(k, 1), :] + mf, gmax_s[pl.ds(k, 1), :])
        kb = nc - 1 - k
        mpb_s[pl.ds(kb, 1), :] = mb
        mb = jnp.maximum(btot_s[pl.ds(kb, 1), :] + mb, gmax_s[pl.ds(kb, 1), :])
        return mf, mb

    zero = jnp.zeros((1, LANES), F32)
    lax.fori_loop(0, nc, scan, (zero, zero))

    def emit(c, carry):
        rows = chunk_rows(c)
        b = b_s[rows, :]
        r = r_s[rows, :]
        btot = btot_s[pl.ds(c, 1), :]
        mprev = jnp.where(fwd_row, mpf_s[pl.ds(c, 1), :], mpb_s[pl.ds(c, 1), :])
        mnew = jnp.maximum(btot + mprev, gmax_s[pl.ds(c, 1), :])
        dec_ref[0, pl.ds(c, 1), :] = jnp.exp(btot + mprev - mnew)
        m_inter = b + mprev
        m_t = jnp.maximum(m_inter, dmax_s[rows, :])
        cb = b - m_t
        inter = jnp.exp(m_inter - m_t)
        e = jnp.exp(-m_t)
        for p in range(C_PAIRS):
            cols_ref[0, p, rows, :] = jnp.where(
                lane < 4, pltpu.roll(cb, (LANES - 4 * p) % LANES, 1),
                jnp.where(lane < 8, pltpu.roll(inter, (LANES + 4 - 4 * p) % LANES, 1),
                          pltpu.roll(e, (LANES + 8 - 4 * p) % LANES, 1)))
        wk = jnp.exp(btot + r - mnew)
        rt_ref[0, c] = r.T[:C_STREAMS, :]
        wkt_ref[0, c] = wk.T[:C_STREAMS, :]
        return carry

    lax.fori_loop(0, nc, emit, 0)


def mlstm_gates(y3, bias_i, bias_f):
    b, s, _ = y3.shape
    nc = s // MLSTM_CHUNK
    row = pl.BlockSpec((1, nc, C_STREAMS, MLSTM_CHUNK), lambda bi: (bi, 0, 0, 0))
    vec = pl.BlockSpec((1, LANES), lambda bi: (0, 0))
    return pl.pallas_call(
        functools.partial(_gate_kernel, nc=nc),
        grid=(b,),
        in_specs=[pl.BlockSpec((1, s, LANES), lambda bi: (bi, 0, Y_GI // LANES)),
                  pl.BlockSpec((1, s, LANES), lambda bi: (bi, 0, Y_GF // LANES)), vec, vec],
        out_specs=[pl.BlockSpec((1, C_PAIRS, s, LANES), lambda bi: (bi, 0, 0, 0)), row, row,
                   pl.BlockSpec((1, nc, LANES), lambda bi: (bi, 0, 0))],
        out_shape=[jax.ShapeDtypeStruct((b, C_PAIRS, s, LANES), F32)]
        + [jax.ShapeDtypeStruct((b, nc, C_STREAMS, MLSTM_CHUNK), F32)] * 2
        + [jax.ShapeDtypeStruct((b, nc, LANES), F32)],
        scratch_shapes=[pltpu.VMEM((s, LANES), F32)] * 3 + [pltpu.VMEM((nc, LANES), F32)] * 4,
        compiler_params=_cparams(("parallel",)),
        name="mlstm_gates",
    )(y3, y3, bias_i.reshape(1, LANES), bias_f.reshape(1, LANES))


def _mlstm_kernel(dec_ref, q_ref, kt_ref, v_ref, o_ref, g_ref, cols_ref, rt_ref, wkt_ref,
                  y_ref, akw, hsum, ct, *, seq, nc):
    lc = MLSTM_CHUNK
    bi = pl.program_id(0)
    pair = pl.program_id(1)
    lane = lax.broadcasted_iota(jnp.int32, (lc, LANES), 1)
    trow = lax.broadcasted_iota(jnp.int32, (lc, lc), 0)
    scol = lax.broadcasted_iota(jnp.int32, (lc, lc), 1)
    ones = jnp.ones((lc, C_V_DIM), BF16)

    def head_q(q2, hh):
        return jnp.where((lane >= hh * C_QK_DIM) & (lane < (hh + 1) * C_QK_DIM), q2, jnp.zeros_like(q2))

    def intra(c, carry):
        rows = pl.ds(pl.multiple_of(c * lc, lc), lc)
        q2 = q_ref[0, rows, :]
        kt2 = kt_ref[0, c]
        for hh in range(2):
            sc = jnp.dot(head_q(q2, hh), kt2, preferred_element_type=F32)
            kth = kt2[hh * C_QK_DIM:(hh + 1) * C_QK_DIM, :].astype(F32)
            for direction in range(2):
                sl = hh * 2 + direction
                stream = pair * 4 + sl
                mask = (scol <= trow) if direction == 0 else (scol >= trow)
                arg = cols_ref[0, 0, rows, sl:sl + 1] + rt_ref[0, c, pl.ds(stream, 1), :]
                akw[sl, c, :lc, :] = (jnp.exp(jnp.where(mask, arg, NEG_INF)) * sc).astype(BF16)
                akw[sl, c, lc:, :] = (kth * wkt_ref[0, c, pl.ds(stream, 1), :]).astype(BF16)
        return carry

    ct[...] = jnp.zeros_like(ct)
    hsum[...] = jnp.zeros_like(hsum)

    def step(it, carry):
        work = []
        for direction in range(2):
            c = it if direction == 0 else nc - 1 - it
            rows = pl.ds(pl.multiple_of(c * lc, lc), lc)
            q2 = q_ref[0, rows, :]
            ct_pair = ct[direction].astype(BF16)
            for hh in range(2):
                sl = hh * 2 + direction
                vaug = jnp.concatenate([v_ref[0, rows, hh * C_V_DIM:(hh + 1) * C_V_DIM], ones], axis=1)
                num_a = jnp.dot(akw[sl, c, :lc, :], vaug, preferred_element_type=F32)
                num_c = jnp.dot(head_q(q2, hh), ct_pair, preferred_element_type=F32)
                u = jnp.dot(akw[sl, c, lc:, :], vaug, preferred_element_type=F32)
                work.append((direction, hh, sl, c, rows, num_a, num_c, u))
        for direction, hh, sl, c, rows, num_a, num_c, u in work:
            dec = dec_ref[(bi * nc + c) * C_STREAMS + pair * 4 + sl]
            num = num_a + cols_ref[0, 0, rows, 4 + sl:5 + sl] * num_c
            den = jnp.maximum(jnp.abs(num[:, C_V_DIM:]), cols_ref[0, 0, rows, 8 + sl:9 + sl])
            hsum[rows, hh * C_V_DIM:(hh + 1) * C_V_DIM] += num[:, :C_V_DIM] / den
            d_rows = slice(hh * C_QK_DIM, (hh + 1) * C_QK_DIM)
            ct[direction, d_rows, :] = dec * ct[direction, d_rows, :] + u
        return carry

    intra(0, 0)
    intra(nc - 1, 0)
    half = nc // 2

    def fused(it, carry):
        step(it, carry)
        intra(it + 1, carry)
        intra(nc - 2 - it, carry)
        return carry

    lax.fori_loop(0, half - 1, fused, 0)
    lax.fori_loop(half - 1, nc, step, 0, unroll=2)

    tile = 512

    def finish(i, carry):
        rows = pl.ds(pl.multiple_of(i * tile, tile), tile)
        for hh in range(2):
            cols = slice(hh * C_V_DIM, (hh + 1) * C_V_DIM)
            h = hsum[rows, cols]
            h = h * lax.rsqrt(jnp.mean(h * h, axis=-1, keepdims=True) + EPS)
            h = h * g_ref[:, cols]
            y_ref[0, rows, cols] = (h * jax.nn.sigmoid(o_ref[0, rows, cols])).astype(y_ref.dtype)
        return carry

    lax.fori_loop(0, seq // tile, finish, 0)


def mlstm_scan(dec, qk, kt, v3, y3, out_g, cols, rt, wkt):
    b, s, _ = qk.shape
    nc = s // MLSTM_CHUNK
    pw = 2 * C_V_DIM
    col = pl.BlockSpec((1, 1, s, LANES), lambda bi, p, d: (bi, p, 0, 0))
    row = pl.BlockSpec((1, nc, C_STREAMS, MLSTM_CHUNK), lambda bi, p, d: (bi, 0, 0, 0))
    grid_spec = pltpu.PrefetchScalarGridSpec(
        num_scalar_prefetch=1,
        grid=(b, C_PAIRS),
        in_specs=[pl.BlockSpec((1, s, LANES), lambda bi, p, d: (bi, 0, p)),
                  pl.BlockSpec((1, nc, LANES, MLSTM_CHUNK), lambda bi, p, d: (bi, 0, p, 0)),
                  pl.BlockSpec((1, s, pw), lambda bi, p, d: (bi, 0, V_CV // pw + p)),
                  pl.BlockSpec((1, s, pw), lambda bi, p, d: (bi, 0, Y_CO // pw + p)),
                  pl.BlockSpec((1, pw), lambda bi, p, d: (0, p)),
                  col, row, row],
        out_specs=pl.BlockSpec((1, s, pw), lambda bi, p, d: (bi, 0, p)),
        scratch_shapes=[pltpu.VMEM((4, nc, MLSTM_CHUNK + C_QK_DIM, MLSTM_CHUNK), BF16),
                        pltpu.VMEM((s, pw), F32),
                        pltpu.VMEM((2, LANES, 2 * C_V_DIM), F32)],
    )
    return pl.pallas_call(
        functools.partial(_mlstm_kernel, seq=s, nc=nc),
        grid_spec=grid_spec,
        out_shape=jax.ShapeDtypeStruct((b, s, C_HEADS * C_V_DIM), BF16),
        compiler_params=_cparams(("parallel", "arbitrary")),
        name="mlstm_scan",
    )(dec, qk, kt, v3, y3, out_g.reshape(1, C_V_WIDTH), cols, rt, wkt)


def _merge_kernel(ya_ref, yb_ref, yc_ref, ga_ref, gb_ref, gc_ref, wa_ref, wb_ref, wc_ref, o_ref):
    acc = jax.nn.sigmoid(ga_ref[...]) * jnp.dot(ya_ref[...], wa_ref[...], preferred_element_type=F32)
    acc = acc + jax.nn.sigmoid(gb_ref[...]) * jnp.dot(yb_ref[...], wb_ref[...], preferred_element_type=F32)
    acc = acc + jax.nn.sigmoid(gc_ref[...]) * jnp.dot(yc_ref[...], wc_ref[...], preferred_element_type=F32)
    o_ref[...] = acc.astype(o_ref.dtype)


def branch_merge(ya, yb, yc, y2, wa, wb, wc, tm=512, tn=1024):
    m = ya.shape[0]
    n = wa.shape[1]
    nj = n // tn

    def act(width):
        return pl.BlockSpec((tm, width), lambda j, i: (i, 0))

    def gate(branch):
        return pl.BlockSpec((tm, tn), lambda j, i: (i, Y_G // tn + branch * nj + j))

    def weight(width):
        return pl.BlockSpec((width, tn), lambda j, i: (0, j))

    return pl.pallas_call(
        _merge_kernel,
        grid=(nj, m // tm),
        in_specs=[act(A_WIDTH), act(B_Q_WIDTH), act(C_V_WIDTH), gate(0), gate(1), gate(2),
                  weight(A_WIDTH), weight(B_Q_WIDTH), weight(C_V_WIDTH)],
        out_specs=pl.BlockSpec((tm, tn), lambda j, i: (i, j)),
        out_shape=jax.ShapeDtypeStruct((m, n), BF16),
        compiler_params=_cparams(("parallel", "parallel")),
        name="branch_merge",
    )(ya, yb, yc, y2, y2, y2, wa, wb, wc)


def _rope_tables(pos, dim):
    inv_freq = ROPE_THETA ** (-jnp.arange(0, dim, 2, dtype=F32) / dim)
    ang = pos.astype(F32)[:, None] * inv_freq[None, :]
    return jnp.cos(ang), jnp.sin(ang)


def _stream_gate_columns(w_gate, b_gate):
    lanes = jnp.arange(C_STREAMS)
    head = (lanes // 4) * 2 + (lanes % 4) // 2
    direction = lanes % 2
    i_cols = direction * 2 * C_HEADS + head
    f_cols = i_cols + C_HEADS
    pad = LANES - C_STREAMS
    wi = jnp.pad(w_gate[:, i_cols], ((0, 0), (0, pad)))
    wf = jnp.pad(w_gate[:, f_cols], ((0, 0), (0, pad)))
    return wi, wf, jnp.pad(b_gate[i_cols], (0, pad)), jnp.pad(b_gate[f_cols], (0, pad))


def _layer(x2, shape, tables, norm1_g, w_in, a_q_gain, a_k_gain, b_q_gain, b_k_gain, c_conv_w, c_conv_b,
           c_gate_b, c_out_g, w_branch_a, w_branch_b, w_branch_c, w_out, norm2_g, w_up, w_down):
    bsz, seq = shape
    cos_a, sin_a, cos_b, sin_b = tables
    sizes = (A_WIDTH, A_WIDTH, A_WIDTH, B_Q_WIDTH, B_KV_WIDTH, B_KV_WIDTH, 2 * C_QK_WIDTH, C_V_WIDTH, C_V_WIDTH,
             4 * C_HEADS, N_BRANCHES * x2.shape[1])
    offs = [0]
    for sz in sizes:
        offs.append(offs[-1] + sz)
    seg = lambda i: w_in[:, offs[i]:offs[i + 1]]
    wi, wf, bias_i, bias_f = _stream_gate_columns(seg(9), c_gate_b)
    w_y = jnp.concatenate([seg(10), seg(6), seg(8), seg(3), seg(0), seg(1), seg(4), wi, wf], axis=1).astype(BF16)
    w_v = jnp.concatenate([seg(2), seg(5), seg(7)], axis=1).astype(BF16)

    xn = rmsnorm(x2, norm1_g)
    y2 = matmul(xn, w_y, out_dtype=F32, tm=2048)
    v2 = matmul(xn, w_v, out_dtype=BF16, tm=2048)
    y3 = y2.reshape(bsz, seq, Y_COLS)
    v3 = v2.reshape(bsz, seq, V_COLS)

    ya = dilated_mixer(y3, v3, cos_a, sin_a, a_q_gain, a_k_gain).reshape(bsz * seq, A_WIDTH)
    yb = gqa_attention(y3, v3, cos_b, sin_b, b_q_gain, b_k_gain).reshape(bsz * seq, B_Q_WIDTH)

    qk_c, kt_c = mlstm_conv(y3, c_conv_w, c_conv_b)
    cols, rt, wkt, dec = mlstm_gates(y3, bias_i, bias_f)
    dec_flat = dec[:, :, :C_STREAMS].reshape(-1)
    yc = mlstm_scan(dec_flat, qk_c, kt_c, v3, y3, c_out_g, cols, rt, wkt).reshape(bsz * seq, C_V_WIDTH)

    merged = branch_merge(ya, yb, yc, y2, w_branch_a.astype(BF16), w_branch_b.astype(BF16), w_branch_c.astype(BF16))
    h, hn = matmul_residual_norm(merged, w_out.astype(BF16), x2, norm2_g)
    u2 = matmul(hn, w_up.astype(BF16), out_dtype=BF16, relu2=True, tm=2048)
    return matmul(u2, w_down.astype(BF16), out_dtype=F32, res=h, tm=1024, tn=1024, tk=2048)


def kernel(x, norm1_g, w_in, a_q_gain, a_k_gain, b_q_gain, b_k_gain, c_conv_w, c_conv_b, c_gate_b, c_out_g, w_branch_a, w_branch_b, w_branch_c, w_out, norm2_g, w_up, w_down):
    bsz, seq, d_model = x.shape
    pos = jnp.arange(seq)
    cos1, sin1 = _rope_tables(pos, HEAD_DIM)
    cos_r, sin_r = _rope_tables(pos // GRID_W, HEAD_DIM // 2)
    cos_c, sin_c = _rope_tables(pos % GRID_W, HEAD_DIM // 2)
    tables = (jnp.concatenate([cos1, cos1], axis=1), jnp.concatenate([-sin1, sin1], axis=1),
              jnp.concatenate([cos_r, cos_r, cos_c, cos_c], axis=1),
              jnp.concatenate([-sin_r, sin_r, -sin_c, sin_c], axis=1))
    x2 = x.reshape(bsz * seq, d_model)
    for l in range(norm1_g.shape[0]):
        x2 = _layer(x2, (bsz, seq), tables, norm1_g[l], w_in[l], a_q_gain[l], a_k_gain[l], b_q_gain[l], b_k_gain[l],
                    c_conv_w[l], c_conv_b[l], c_gate_b[l], c_out_g[l], w_branch_a[l], w_branch_b[l], w_branch_c[l],
                    w_out[l], norm2_g[l], w_up[l], w_down[l])
    return x2.reshape(bsz, seq, d_model)
```

```python
import functools
import math

import jax
import jax.numpy as jnp
from jax import lax
from jax.experimental import pallas as pl
from jax.experimental.pallas import tpu as pltpu

F32 = jnp.float32
BF16 = jnp.bfloat16

HEAD_DIM = 128
GRID_W = 64
ROPE_THETA = 10000.0
EPS = 1e-6
NEG_INF = -1e30

A_HEADS = 6
A_PATTERNS = ((128, 1), (512, 4), (2048, 16))
A_REACH_BLOCK = 64
A_QUERY_BLOCK = 128
A_GROUP = 4
B_Q_HEADS = 8
B_KV_HEADS = 2
B_ROW_SLAB = 256
C_HEADS = 8
C_QK_DIM = 64
C_V_DIM = 128
C_CONV = 5
N_BRANCHES = 3

A_WIDTH = A_HEADS * HEAD_DIM
B_Q_WIDTH = B_Q_HEADS * HEAD_DIM
B_KV_WIDTH = B_KV_HEADS * HEAD_DIM
C_QK_WIDTH = C_HEADS * C_QK_DIM
C_V_WIDTH = C_HEADS * C_V_DIM

LANES = 128
MLSTM_CHUNK = 128
C_PAIRS = C_HEADS // 2
C_STREAMS = 2 * C_HEADS
VMEM_LIMIT = 60 * 1024 * 1024

Y_G = 0
Y_CQK = 6144
Y_CO = 7168
Y_BQ = 8192
Y_AQ = 9216
Y_AK = 9984
Y_BK = 10752
Y_GI = 11008
Y_GF = 11136
Y_COLS = 11264
V_AV = 0
V_BV = 768
V_CV = 1024
V_COLS = 2048


def _cparams(sem):
    return pltpu.CompilerParams(dimension_semantics=sem, vmem_limit_bytes=VMEM_LIMIT)


def _rmsnorm_kernel(x_ref, g_ref, o_ref):
    x = x_ref[...]
    r = lax.rsqrt(jnp.mean(x * x, axis=-1, keepdims=True) + EPS)
    o_ref[...] = ((x * r) * g_ref[...]).astype(o_ref.dtype)


def rmsnorm(x, g, tm=512):
    m, d = x.shape
    return pl.pallas_call(
        _rmsnorm_kernel,
        grid=(m // tm,),
        in_specs=[pl.BlockSpec((tm, d), lambda i: (i, 0)), pl.BlockSpec((1, d), lambda i: (0, 0))],
        out_specs=pl.BlockSpec((tm, d), lambda i: (i, 0)),
        out_shape=jax.ShapeDtypeStruct((m, d), BF16),
        compiler_params=_cparams(("parallel",)),
        name="rmsnorm",
    )(x, g.reshape(1, d))


def _matmul_kernel(*refs, nk, relu2, has_res):
    if has_res:
        a_ref, w_ref, r_ref, o_ref = refs
    else:
        a_ref, w_ref, o_ref = refs
        r_ref = None

    prod = jnp.dot(a_ref[...], w_ref[...], preferred_element_type=F32)
    if relu2:
        u = jnp.maximum(prod, 0.0)
        prod = u * u
    if nk == 1:
        o_ref[...] = (prod if r_ref is None else r_ref[...] + prod).astype(o_ref.dtype)
    else:
        k = pl.program_id(2)

        @pl.when(k == 0)
        def _():
            o_ref[...] = prod if r_ref is None else r_ref[...] + prod

        @pl.when(k > 0)
        def _():
            o_ref[...] += prod


def matmul(a, w, *, out_dtype, res=None, relu2=False, tm=1024, tn=1024, tk=2048):
    m, kdim = a.shape
    n = w.shape[1]
    tn = min(tn, n)
    tk = min(tk, kdim)
    nk = kdim // tk
    assert nk == 1 or (out_dtype == F32 and not relu2)
    in_specs = [pl.BlockSpec((tm, tk), lambda i, j, k: (i, k)),
                pl.BlockSpec((tk, tn), lambda i, j, k: (k, j))]
    args = [a, w]
    if res is not None:
        in_specs.append(pl.BlockSpec((tm, tn), lambda i, j, k: (i, j)))
        args.append(res)
    return pl.pallas_call(
        functools.partial(_matmul_kernel, nk=nk, relu2=relu2, has_res=res is not None),
        grid=(m // tm, n // tn, nk),
        in_specs=in_specs,
        out_specs=pl.BlockSpec((tm, tn), lambda i, j, k: (i, j)),
        out_shape=jax.ShapeDtypeStruct((m, n), out_dtype),
        compiler_params=_cparams(("parallel", "parallel", "arbitrary")),
        name="matmul",
    )(*args)


def _out_norm_kernel(a_ref, w_ref, r_ref, g_ref, h_ref, hn_ref):
    h = r_ref[...] + jnp.dot(a_ref[...], w_ref[...], preferred_element_type=F32)
    h_ref[...] = h
    r = lax.rsqrt(jnp.mean(h * h, axis=-1, keepdims=True) + EPS)
    hn_ref[...] = ((h * r) * g_ref[...]).astype(hn_ref.dtype)


def matmul_residual_norm(a, w, res, g, tm=512):
    m, kdim = a.shape
    n = w.shape[1]
    row = lambda width: pl.BlockSpec((tm, width), lambda i: (i, 0))
    return pl.pallas_call(
        _out_norm_kernel,
        grid=(m // tm,),
        in_specs=[row(kdim), pl.BlockSpec((kdim, n), lambda i: (0, 0)), row(n), pl.BlockSpec((1, n), lambda i: (0, 0))],
        out_specs=[row(n), row(n)],
        out_shape=[jax.ShapeDtypeStruct((m, n), F32), jax.ShapeDtypeStruct((m, n), BF16)],
        compiler_params=_cparams(("parallel",)),
        name="matmul_residual_norm",
    )(a, w, res, g.reshape(1, n))


def _norm_rope(x, gain, cos_full, sin_signed, half):
    y = (x * lax.rsqrt(jnp.mean(x * x, axis=-1, keepdims=True) + EPS)) * gain
    if 2 * half == LANES:
        partner = pltpu.roll(y, half, 1)
    else:
        lane = lax.broadcasted_iota(jnp.int32, y.shape, 1)
        partner = jnp.where(lane % (2 * half) < half, pltpu.roll(y, LANES - half, 1), pltpu.roll(y, half, 1))
    return y * cos_full + partner * sin_signed


def _dilated_kernel(q_ref, k_ref, v_ref, cos_ref, sin_ref, qg_ref, kg_ref, y_ref,
                    qn, kn, vn, kd, vd, acc, mrun, zrun, sbuf, pbuf, mblk, bias, *, seq):
    blk = A_REACH_BLOCK
    qb = A_QUERY_BLOCK
    kw = qb + 2 * blk
    grp = A_GROUP
    tile = 512
    scale = HEAD_DIM ** -0.5

    def prep(i, carry):
        rows = pl.ds(pl.multiple_of(i * tile, tile), tile)
        cos = cos_ref[rows, :]
        sin = sin_ref[rows, :]
        qn[rows, :] = _norm_rope(q_ref[0, rows, :], qg_ref[...], cos, sin, HEAD_DIM // 2)
        kn[rows, :] = _norm_rope(k_ref[0, rows, :], kg_ref[...], cos, sin, HEAD_DIM // 2)
        vn[rows, :] = v_ref[0, rows, :].astype(F32)
        return carry

    lax.fori_loop(0, seq // tile, prep, 0)
    zpad = jnp.zeros((blk, HEAD_DIM), BF16)
    kd[0:blk, :] = zpad
    kd[blk + seq:, :] = zpad
    vd[0:blk, :] = zpad
    vd[blk + seq:, :] = zpad

    t_io = lax.broadcasted_iota(jnp.int32, (qb, kw), 0)
    j_io = lax.broadcasted_iota(jnp.int32, (qb, kw), 1)
    in_reach = jnp.abs(j_io - blk - t_io) <= blk
    bias[0] = jnp.where(in_reach, 0.0, NEG_INF)
    bias[1] = jnp.where(in_reach & (j_io >= blk), 0.0, NEG_INF)
    bias[2] = jnp.where(in_reach & (j_io < qb + blk), 0.0, NEG_INF)
    nblk = seq // qb
    ones = jnp.ones((kw, HEAD_DIM), BF16)

    for pi, (_, d) in enumerate(sorted(A_PATTERNS, key=lambda wd: -wd[1])):
        sub_len = seq // d
        chunk = 256
        per_res = sub_len // chunk
        blk_per_res = sub_len // qb
        first = pi == 0

        def gather(idx, carry, d=d, sub_len=sub_len, per_res=per_res):
            r = idx // per_res
            j = idx % per_res
            if d == 1:
                src = pl.ds(pl.multiple_of(j * chunk, chunk), chunk)
            else:
                src = pl.ds(r + j * chunk * d, chunk, stride=d)
            dst = pl.ds(pl.multiple_of(blk + r * sub_len + j * chunk, blk), chunk)
            kd[dst, :] = kn[src, :].astype(BF16)
            vd[dst, :] = vn[src, :].astype(BF16)
            return carry

        lax.fori_loop(0, seq // chunk, gather, 0)

        def natural_rows(n, d=d, blk_per_res=blk_per_res):
            if d == 1:
                return pl.ds(pl.multiple_of(n * qb, qb), qb)
            return pl.ds(n // blk_per_res + (n % blk_per_res) * qb * d, qb, stride=d)

        def packed_rows(n, size=qb):
            return pl.ds(pl.multiple_of(n * qb, qb), size)

        def scores(t, blk_per_res=blk_per_res):
            for g in range(grp):
                n = t * grp + g
                q = qn[natural_rows(n), :].astype(BF16)
                s = lax.dot_general(q, kd[packed_rows(n, kw), :], (((1,), (1,)), ((), ())),
                                    preferred_element_type=F32) * scale
                i = n % blk_per_res
                kind = jnp.where(i == 0, 1, jnp.where(i == blk_per_res - 1, 2, 0))
                sbuf[packed_rows(n), :] = s + bias[kind]

        def softmax(t):
            for g in range(grp):
                rows = packed_rows(t * grp + g)
                s = sbuf[rows, :]
                m = jnp.max(s, axis=-1, keepdims=True)
                pbuf[rows, :] = jnp.exp(s - m).astype(BF16)
                mblk[rows, :] = jnp.broadcast_to(m, (qb, HEAD_DIM))

        def values(t, first=first):
            for g in range(grp):
                n = t * grp + g
                vaug = jnp.concatenate([vd[packed_rows(n, kw), :], ones], axis=1)
                oz = jnp.dot(pbuf[packed_rows(n), :], vaug, preferred_element_type=F32)
                o = oz[:, :HEAD_DIM]
                z = oz[:, HEAD_DIM:]
                if first:
                    rw = natural_rows(n)
                    acc[rw, :] = o
                    zrun[rw, :] = z
                    mrun[rw, :] = mblk[packed_rows(n), :]
                else:
                    rw = natural_rows(n)
                    m_old = mrun[rw, :]
                    mb = mblk[packed_rows(n), :]
                    m_new = jnp.maximum(m_old, mb)
                    w_old = jnp.exp(m_old - m_new)
                    w_blk = jnp.exp(mb - m_new)
                    acc[rw, :] = w_old * acc[rw, :] + w_blk * o
                    zrun[rw, :] = w_old * zrun[rw, :] + w_blk * z
                    mrun[rw, :] = m_new

        ngroups = nblk // grp
        scores(0)
        softmax(0)
        scores(1)

        def trip(t, carry):
            values(t - 2)
            softmax(t - 1)
            scores(t)
            return carry

        lax.fori_loop(2, ngroups, trip, 0)
        values(ngroups - 2)
        softmax(ngroups - 1)
        values(ngroups - 1)

    def finish(i, carry):
        rows = pl.ds(pl.multiple_of(i * tile, tile), tile)
        y_ref[0, rows, :] = (acc[rows, :] / zrun[rows, :]).astype(y_ref.dtype)
        return carry

    lax.fori_loop(0, seq // tile, finish, 0)


def dilated_mixer(y3, v3, cos_full, sin_signed, q_gain, k_gain):
    b, s, _ = y3.shape
    assert all(w // (2 * d) == A_REACH_BLOCK for w, d in A_PATTERNS)
    assert all((s // d) % 256 == 0 for _, d in A_PATTERNS) and (s // A_QUERY_BLOCK) % A_GROUP == 0 and s // A_QUERY_BLOCK >= 3 * A_GROUP
    kw = A_QUERY_BLOCK + 2 * A_REACH_BLOCK
    head = lambda off: pl.BlockSpec((1, s, HEAD_DIM), lambda bi, h: (bi, 0, off // HEAD_DIM + h))
    table = pl.BlockSpec((s, HEAD_DIM), lambda bi, h: (0, 0))
    vec = pl.BlockSpec((1, HEAD_DIM), lambda bi, h: (0, 0))
    pad_rows = s + 2 * A_REACH_BLOCK
    return pl.pallas_call(
        functools.partial(_dilated_kernel, seq=s),
        grid=(b, A_HEADS),
        in_specs=[head(Y_AQ), head(Y_AK), head(V_AV), table, table, vec, vec],
        out_specs=pl.BlockSpec((1, s, HEAD_DIM), lambda bi, h: (bi, 0, h)),
        out_shape=jax.ShapeDtypeStruct((b, s, A_WIDTH), BF16),
        scratch_shapes=[pltpu.VMEM((s, HEAD_DIM), F32)] * 3 + [pltpu.VMEM((pad_rows, HEAD_DIM), BF16)] * 2
        + [pltpu.VMEM((s, HEAD_DIM), F32)] * 3
        + [pltpu.VMEM((s, kw), F32), pltpu.VMEM((s, kw), BF16), pltpu.VMEM((s, HEAD_DIM), F32)]
        + [pltpu.VMEM((3, A_QUERY_BLOCK, kw), F32)],
        compiler_params=_cparams(("parallel", "parallel")),
        name="dilated_mixer",
    )(y3, y3, v3, cos_full, sin_signed, q_gain.reshape(1, HEAD_DIM), k_gain.reshape(1, HEAD_DIM))


def _gqa_kernel(q_ref, k_ref, v_ref, cosq_ref, sinq_ref, cosk_ref, sink_ref, qg_ref, kg_ref, o_ref, kn, vaug, *, group, seq):
    half = HEAD_DIM // 4
    tile = 512

    @pl.when(pl.program_id(2) == 0)
    def _():
        def prep(i, carry):
            rows = pl.ds(pl.multiple_of(i * tile, tile), tile)
            kn[rows, :] = _norm_rope(k_ref[0, rows, :], kg_ref[...], cosk_ref[rows, :], sink_ref[rows, :],
                                     half).astype(kn.dtype)
            return carry

        lax.fori_loop(0, seq // tile, prep, 0)

        vaug[:, :HEAD_DIM] = v_ref[0]
        vaug[:, HEAD_DIM:] = jnp.ones((seq, HEAD_DIM), vaug.dtype)

    k = kn[...]
    v = vaug[...]
    c = (HEAD_DIM ** -0.5) * math.log2(math.e)
    slab = B_ROW_SLAB
    units = [(g, r) for g in range(group) for r in range(q_ref.shape[1] // slab)]

    def scores(unit):
        g, r = unit
        rows = slice(r * slab, (r + 1) * slab)
        q = _norm_rope(q_ref[0, rows, g * HEAD_DIM:(g + 1) * HEAD_DIM], qg_ref[...], cosq_ref[rows, :],
                       sinq_ref[rows, :], half)
        return lax.dot_general(q.astype(BF16), k, (((1,), (1,)), ((), ())), preferred_element_type=F32)

    def probs(s):
        return jnp.exp2((s - jnp.max(s, axis=-1, keepdims=True)) * c).astype(BF16)

    def values(unit, p):
        g, r = unit
        o = jnp.dot(p, v, preferred_element_type=F32)
        o_ref[0, r * slab:(r + 1) * slab, g * HEAD_DIM:(g + 1) * HEAD_DIM] = (
            o[:, :HEAD_DIM] / o[:, HEAD_DIM:]).astype(o_ref.dtype)

    s_next = scores(units[0])
    p_prev = None
    for u, unit in enumerate(units):
        s_cur = s_next
        if u + 1 < len(units):
            s_next = scores(units[u + 1])
        if p_prev is not None:
            values(units[u - 1], p_prev)
        p_prev = probs(s_cur)
    values(units[-1], p_prev)


def gqa_attention(y3, v3, cos_full, sin_signed, q_gain, k_gain, tq=1024):
    b, s, _ = y3.shape
    group = B_Q_HEADS // B_KV_HEADS
    gw = group * HEAD_DIM
    vec = pl.BlockSpec((1, HEAD_DIM), lambda bi, h, t: (0, 0))
    qtab = pl.BlockSpec((tq, HEAD_DIM), lambda bi, h, t: (t, 0))
    ktab = pl.BlockSpec((s, HEAD_DIM), lambda bi, h, t: (0, 0))
    return pl.pallas_call(
        functools.partial(_gqa_kernel, group=group, seq=s),
        grid=(b, B_KV_HEADS, s // tq),
        in_specs=[pl.BlockSpec((1, tq, gw), lambda bi, h, t: (bi, t, Y_BQ // gw + h)),
                  pl.BlockSpec((1, s, HEAD_DIM), lambda bi, h, t: (bi, 0, Y_BK // HEAD_DIM + h)),
                  pl.BlockSpec((1, s, HEAD_DIM), lambda bi, h, t: (bi, 0, V_BV // HEAD_DIM + h)),
                  qtab, qtab, ktab, ktab, vec, vec],
        out_specs=pl.BlockSpec((1, tq, gw), lambda bi, h, t: (bi, t, h)),
        out_shape=jax.ShapeDtypeStruct((b, s, B_Q_WIDTH), BF16),
        scratch_shapes=[pltpu.VMEM((s, HEAD_DIM), BF16), pltpu.VMEM((s, 2 * HEAD_DIM), BF16)],
        compiler_params=_cparams(("parallel", "parallel", "arbitrary")),
        name="gqa_attention",
    )(y3, y3, v3, cos_full, sin_signed, cos_full, sin_signed, q_gain.reshape(1, HEAD_DIM), k_gain.reshape(1, HEAD_DIM))


def _conv_kernel(xp_ref, xm_ref, xn_ref, w_ref, b_ref, qk_ref, kt_ref, buf, *, tile, ntiles):
    ti = pl.program_id(1)
    halo = 8
    buf[0:halo] = jnp.where(ti > 0, xp_ref[0], 0.0)
    buf[halo:halo + tile] = xm_ref[0]
    buf[halo + tile:] = jnp.where(ti < ntiles - 1, xn_ref[0], 0.0)
    pad = C_CONV // 2
    acc = w_ref[0:1, :] * buf[halo - pad:halo - pad + tile, :]
    for j in range(1, C_CONV):
        acc = acc + w_ref[j:j + 1, :] * buf[halo - pad + j:halo - pad + j + tile, :]
    acc = acc + b_ref[...]
    act = acc * jax.nn.sigmoid(acc)
    q = act[:, :C_QK_WIDTH]
    k = act[:, C_QK_WIDTH:] * (C_QK_DIM ** -0.5)
    qk_ref[0, :, :C_QK_WIDTH] = q.astype(qk_ref.dtype)
    qk_ref[0, :, C_QK_WIDTH:] = k.astype(qk_ref.dtype)
    for c in range(tile // MLSTM_CHUNK):
        kt_ref[0, c] = k[c * MLSTM_CHUNK:(c + 1) * MLSTM_CHUNK, :].T.astype(kt_ref.dtype)


def mlstm_conv(y3, conv_w, conv_b, tile=512):
    b, s, _ = y3.shape
    ntiles = s // tile
    w2 = 2 * C_QK_WIDTH
    cb = Y_CQK // w2
    per = tile // 8
    return pl.pallas_call(
        functools.partial(_conv_kernel, tile=tile, ntiles=ntiles),
        grid=(b, ntiles),
        in_specs=[pl.BlockSpec((1, 8, w2), lambda bi, t: (bi, jnp.maximum(t * per - 1, 0), cb)),
                  pl.BlockSpec((1, tile, w2), lambda bi, t: (bi, t, cb)),
                  pl.BlockSpec((1, 8, w2), lambda bi, t: (bi, jnp.minimum((t + 1) * per, s // 8 - 1), cb)),
                  pl.BlockSpec((C_CONV, w2), lambda bi, t: (0, 0)),
                  pl.BlockSpec((1, w2), lambda bi, t: (0, 0))],
        out_specs=[pl.BlockSpec((1, tile, w2), lambda bi, t: (bi, t, 0)),
                   pl.BlockSpec((1, tile // MLSTM_CHUNK, C_QK_WIDTH, MLSTM_CHUNK), lambda bi, t: (bi, t, 0, 0))],
        out_shape=[jax.ShapeDtypeStruct((b, s, w2), BF16),
                   jax.ShapeDtypeStruct((b, s // MLSTM_CHUNK, C_QK_WIDTH, MLSTM_CHUNK), BF16)],
        scratch_shapes=[pltpu.VMEM((tile + 16, w2), F32)],
        compiler_params=_cparams(("parallel", "parallel")),
        name="mlstm_conv",
    )(y3, y3, y3, conv_w, conv_b.reshape(1, w2))


def _gate_kernel(gi_ref, gf_ref, bi_ref, bf_ref, cols_ref, rt_ref, wkt_ref, dec_ref,
                 b_s, r_s, dmax_s, btot_s, gmax_s, mpf_s, mpb_s, *, nc):
    lc = MLSTM_CHUNK
    t_in = lax.broadcasted_iota(jnp.int32, (lc, LANES), 0)
    lane = lax.broadcasted_iota(jnp.int32, (lc, LANES), 1)
    fwd = lane % 2 == 0
    fwd_row = fwd[0:1]
    steps = [1 << i for i in range(lc.bit_length() - 1)]

    def chunk_rows(c):
        return pl.ds(pl.multiple_of(c * lc, lc), lc)

    def local(c, carry):
        rows = chunk_rows(c)
        ii = gi_ref[0, rows, :] + bi_ref[...]
        ff = gf_ref[0, rows, :] + bf_ref[...]
        lf = jnp.minimum(ff, 0.0) - jnp.log1p(jnp.exp(-jnp.abs(ff)))
        cs = lf
        rs = lf
        for k in steps:
            cs = cs + jnp.where(t_in >= k, pltpu.roll(cs, k, 0), 0.0)
            rs = rs + jnp.where(t_in + k < lc, pltpu.roll(rs, lc - k, 0), 0.0)
        b = jnp.where(fwd, cs, rs)
        btot = cs + rs - lf
        r = ii - b
        pm = r
        sm = r
        for k in steps:
            pm = jnp.where(t_in >= k, jnp.maximum(pm, pltpu.roll(pm, k, 0)), pm)
            sm = jnp.where(t_in + k < lc, jnp.maximum(sm, pltpu.roll(sm, lc - k, 0)), sm)
        b_s[rows, :] = b
        r_s[rows, :] = r
        dmax_s[rows, :] = b + jnp.where(fwd, pm, sm)
        btot_s[pl.ds(c, 1), :] = btot[0:1]
        gmax_s[pl.ds(c, 1), :] = (btot + jnp.maximum(pm, sm))[0:1]
        return carry

    lax.fori_loop(0, nc, local, 0)

    def scan(k, carry):
        mf, mb = carry
        mpf_s[pl.ds(k, 1), :] = mf
        mf = jnp.maximum(btot_s[pl.ds(k, 1), :] + mf, gmax_s[pl.ds(k, 1), :])
        kb = nc - 1 - k
        mpb_s[pl.ds(kb, 1), :] = mb
        mb = jnp.maximum(btot_s[pl.ds(kb, 1), :] + mb, gmax_s[pl.ds(kb, 1), :])
        return mf, mb

    zero = jnp.zeros((1, LANES), F32)
    lax.fori_loop(0, nc, scan, (zero, zero))

    def emit(c, carry):
        rows = chunk_rows(c)
        b = b_s[rows, :]
        r = r_s[rows, :]
        btot = btot_s[pl.ds(c, 1), :]
        mprev = jnp.where(fwd_row, mpf_s[pl.ds(c, 1), :], mpb_s[pl.ds(c, 1), :])
        mnew = jnp.maximum(btot + mprev, gmax_s[pl.ds(c, 1), :])
        dec_ref[0, pl.ds(c, 1), :] = jnp.exp(btot + mprev - mnew)
        m_inter = b + mprev
        m_t = jnp.maximum(m_inter, dmax_s[rows, :])
        cb = b - m_t
        inter = jnp.exp(m_inter - m_t)
        e = jnp.exp(-m_t)
        for p in range(C_PAIRS):
            cols_ref[0, p, rows, :] = jnp.where(
                lane < 4, pltpu.roll(cb, (LANES - 4 * p) % LANES, 1),
                jnp.where(lane < 8, pltpu.roll(inter, (LANES + 4 - 4 * p) % LANES, 1),
                          pltpu.roll(e, (LANES + 8 - 4 * p) % LANES, 1)))
        wk = jnp.exp(btot + r - mnew)
        rt_ref[0, c] = r.T[:C_STREAMS, :]
        wkt_ref[0, c] = wk.T[:C_STREAMS, :]
        return carry

    lax.fori_loop(0, nc, emit, 0)


def mlstm_gates(y3, bias_i, bias_f):
    b, s, _ = y3.shape
    nc = s // MLSTM_CHUNK
    row = pl.BlockSpec((1, nc, C_STREAMS, MLSTM_CHUNK), lambda bi: (bi, 0, 0, 0))
    vec = pl.BlockSpec((1, LANES), lambda bi: (0, 0))
    return pl.pallas_call(
        functools.partial(_gate_kernel, nc=nc),
        grid=(b,),
        in_specs=[pl.BlockSpec((1, s, LANES), lambda bi: (bi, 0, Y_GI // LANES)),
                  pl.BlockSpec((1, s, LANES), lambda bi: (bi, 0, Y_GF // LANES)), vec, vec],
        out_specs=[pl.BlockSpec((1, C_PAIRS, s, LANES), lambda bi: (bi, 0, 0, 0)), row, row,
                   pl.BlockSpec((1, nc, LANES), lambda bi: (bi, 0, 0))],
        out_shape=[jax.ShapeDtypeStruct((b, C_PAIRS, s, LANES), F32)]
        + [jax.ShapeDtypeStruct((b, nc, C_STREAMS, MLSTM_CHUNK), F32)] * 2
        + [jax.ShapeDtypeStruct((b, nc, LANES), F32)],
        scratch_shapes=[pltpu.VMEM((s, LANES), F32)] * 3 + [pltpu.VMEM((nc, LANES), F32)] * 4,
        compiler_params=_cparams(("parallel",)),
        name="mlstm_gates",
    )(y3, y3, bias_i.reshape(1, LANES), bias_f.reshape(1, LANES))


def _mlstm_kernel(dec_ref, q_ref, kt_ref, v_ref, o_ref, g_ref, cols_ref, rt_ref, wkt_ref,
                  y_ref, akw, hsum, ct, *, seq, nc):
    lc = MLSTM_CHUNK
    bi = pl.program_id(0)
    pair = pl.program_id(1)
    lane = lax.broadcasted_iota(jnp.int32, (lc, LANES), 1)
    trow = lax.broadcasted_iota(jnp.int32, (lc, lc), 0)
    scol = lax.broadcasted_iota(jnp.int32, (lc, lc), 1)
    ones = jnp.ones((lc, C_V_DIM), BF16)

    def head_q(q2, hh):
        return jnp.where((lane >= hh * C_QK_DIM) & (lane < (hh + 1) * C_QK_DIM), q2, jnp.zeros_like(q2))

    def intra(c, carry):
        rows = pl.ds(pl.multiple_of(c * lc, lc), lc)
        q2 = q_ref[0, rows, :]
        kt2 = kt_ref[0, c]
        for hh in range(2):
            sc = jnp.dot(head_q(q2, hh), kt2, preferred_element_type=F32)
            kth = kt2[hh * C_QK_DIM:(hh + 1) * C_QK_DIM, :].astype(F32)
            for direction in range(2):
                sl = hh * 2 + direction
                stream = pair * 4 + sl
                mask = (scol <= trow) if direction == 0 else (scol >= trow)
                arg = cols_ref[0, 0, rows, sl:sl + 1] + rt_ref[0, c, pl.ds(stream, 1), :]
                akw[sl, c, :lc, :] = (jnp.exp(jnp.where(mask, arg, NEG_INF)) * sc).astype(BF16)
                akw[sl, c, lc:, :] = (kth * wkt_ref[0, c, pl.ds(stream, 1), :]).astype(BF16)
        return carry

    ct[...] = jnp.zeros_like(ct)
    hsum[...] = jnp.zeros_like(hsum)

    def step(it, carry):
        work = []
        for direction in range(2):
            c = it if direction == 0 else nc - 1 - it
            rows = pl.ds(pl.multiple_of(c * lc, lc), lc)
            q2 = q_ref[0, rows, :]
            ct_pair = ct[direction].astype(BF16)
            for hh in range(2):
                sl = hh * 2 + direction
                vaug = jnp.concatenate([v_ref[0, rows, hh * C_V_DIM:(hh + 1) * C_V_DIM], ones], axis=1)
                num_a = jnp.dot(akw[sl, c, :lc, :], vaug, preferred_element_type=F32)
                num_c = jnp.dot(head_q(q2, hh), ct_pair, preferred_element_type=F32)
                u = jnp.dot(akw[sl, c, lc:, :], vaug, preferred_element_type=F32)
                work.append((direction, hh, sl, c, rows, num_a, num_c, u))
        for direction, hh, sl, c, rows, num_a, num_c, u in work:
            dec = dec_ref[(bi * nc + c) * C_STREAMS + pair * 4 + sl]
            num = num_a + cols_ref[0, 0, rows, 4 + sl:5 + sl] * num_c
            den = jnp.maximum(jnp.abs(num[:, C_V_DIM:]), cols_ref[0, 0, rows, 8 + sl:9 + sl])
            hsum[rows, hh * C_V_DIM:(hh + 1) * C_V_DIM] += num[:, :C_V_DIM] / den
            d_rows = slice(hh * C_QK_DIM, (hh + 1) * C_QK_DIM)
            ct[direction, d_rows, :] = dec * ct[direction, d_rows, :] + u
        return carry

    intra(0, 0)
    intra(nc - 1, 0)
    half = nc // 2

    def fused(it, carry):
        step(it, carry)
        intra(it + 1, carry)
        intra(nc - 2 - it, carry)
        return carry

    lax.fori_loop(0, half - 1, fused, 0)
    lax.fori_loop(half - 1, nc, step, 0, unroll=2)

    tile = 512

    def finish(i, carry):
        rows = pl.ds(pl.multiple_of(i * tile, tile), tile)
        for hh in range(2):
            cols = slice(hh * C_V_DIM, (hh + 1) * C_V_DIM)
            h = hsum[rows, cols]
            h = h * lax.rsqrt(jnp.mean(h * h, axis=-1, keepdims=True) + EPS)
            h = h * g_ref[:, cols]
            y_ref[0, rows, cols] = (h * jax.nn.sigmoid(o_ref[0, rows, cols])).astype(y_ref.dtype)
        return carry

    lax.fori_loop(0, seq // tile, finish, 0)


def mlstm_scan(dec, qk, kt, v3, y3, out_g, cols, rt, wkt):
    b, s, _ = qk.shape
    nc = s // MLSTM_CHUNK
    pw = 2 * C_V_DIM
    col = pl.BlockSpec((1, 1, s, LANES), lambda bi, p, d: (bi, p, 0, 0))
    row = pl.BlockSpec((1, nc, C_STREAMS, MLSTM_CHUNK), lambda bi, p, d: (bi, 0, 0, 0))
    grid_spec = pltpu.PrefetchScalarGridSpec(
        num_scalar_prefetch=1,
        grid=(b, C_PAIRS),
        in_specs=[pl.BlockSpec((1, s, LANES), lambda bi, p, d: (bi, 0, p)),
                  pl.BlockSpec((1, nc, LANES, MLSTM_CHUNK), lambda bi, p, d: (bi, 0, p, 0)),
                  pl.BlockSpec((1, s, pw), lambda bi, p, d: (bi, 0, V_CV // pw + p)),
                  pl.BlockSpec((1, s, pw), lambda bi, p, d: (bi, 0, Y_CO // pw + p)),
                  pl.BlockSpec((1, pw), lambda bi, p, d: (0, p)),
                  col, row, row],
        out_specs=pl.BlockSpec((1, s, pw), lambda bi, p, d: (bi, 0, p)),
        scratch_shapes=[pltpu.VMEM((4, nc, MLSTM_CHUNK + C_QK_DIM, MLSTM_CHUNK), BF16),
                        pltpu.VMEM((s, pw), F32),
                        pltpu.VMEM((2, LANES, 2 * C_V_DIM), F32)],
    )
    return pl.pallas_call(
        functools.partial(_mlstm_kernel, seq=s, nc=nc),
        grid_spec=grid_spec,
        out_shape=jax.ShapeDtypeStruct((b, s, C_HEADS * C_V_DIM), BF16),
        compiler_params=_cparams(("parallel", "arbitrary")),
        name="mlstm_scan",
    )(dec, qk, kt, v3, y3, out_g.reshape(1, C_V_WIDTH), cols, rt, wkt)


def _merge_kernel(ya_ref, yb_ref, yc_ref, ga_ref, gb_ref, gc_ref, wa_ref, wb_ref, wc_ref, o_ref):
    acc = jax.nn.sigmoid(ga_ref[...]) * jnp.dot(ya_ref[...], wa_ref[...], preferred_element_type=F32)
    acc = acc + jax.nn.sigmoid(gb_ref[...]) * jnp.dot(yb_ref[...], wb_ref[...], preferred_element_type=F32)
    acc = acc + jax.nn.sigmoid(gc_ref[...]) * jnp.dot(yc_ref[...], wc_ref[...], preferred_element_type=F32)
    o_ref[...] = acc.astype(o_ref.dtype)


def branch_merge(ya, yb, yc, y2, wa, wb, wc, tm=512, tn=1024):
    m = ya.shape[0]
    n = wa.shape[1]
    nj = n // tn

    def act(width):
        return pl.BlockSpec((tm, width), lambda j, i: (i, 0))

    def gate(branch):
        return pl.BlockSpec((tm, tn), lambda j, i: (i, Y_G // tn + branch * nj + j))

    def weight(width):
        return pl.BlockSpec((width, tn), lambda j, i: (0, j))

    return pl.pallas_call(
        _merge_kernel,
        grid=(nj, m // tm),
        in_specs=[act(A_WIDTH), act(B_Q_WIDTH), act(C_V_WIDTH), gate(0), gate(1), gate(2),
                  weight(A_WIDTH), weight(B_Q_WIDTH), weight(C_V_WIDTH)],
        out_specs=pl.BlockSpec((tm, tn), lambda j, i: (i, j)),
        out_shape=jax.ShapeDtypeStruct((m, n), BF16),
        compiler_params=_cparams(("parallel", "parallel")),
        name="branch_merge",
    )(ya, yb, yc, y2, y2, y2, wa, wb, wc)


def _rope_tables(pos, dim):
    inv_freq = ROPE_THETA ** (-jnp.arange(0, dim, 2, dtype=F32) / dim)
    ang = pos.astype(F32)[:, None] * inv_freq[None, :]
    return jnp.cos(ang), jnp.sin(ang)


def _stream_gate_columns(w_gate, b_gate):
    lanes = jnp.arange(C_STREAMS)
    head = (lanes // 4) * 2 + (lanes % 4) // 2
    direction = lanes % 2
    i_cols = direction * 2 * C_HEADS + head
    f_cols = i_cols + C_HEADS
    pad = LANES - C_STREAMS
    wi = jnp.pad(w_gate[:, i_cols], ((0, 0), (0, pad)))
    wf = jnp.pad(w_gate[:, f_cols], ((0, 0), (0, pad)))
    return wi, wf, jnp.pad(b_gate[i_cols], (0, pad)), jnp.pad(b_gate[f_cols], (0, pad))


def _layer(x2, shape, tables, norm1_g, w_in, a_q_gain, a_k_gain, b_q_gain, b_k_gain, c_conv_w, c_conv_b,
           c_gate_b, c_out_g, w_branch_a, w_branch_b, w_branch_c, w_out, norm2_g, w_up, w_down):
    bsz, seq = shape
    cos_a, sin_a, cos_b, sin_b = tables
    sizes = (A_WIDTH, A_WIDTH, A_WIDTH, B_Q_WIDTH, B_KV_WIDTH, B_KV_WIDTH, 2 * C_QK_WIDTH, C_V_WIDTH, C_V_WIDTH,
             4 * C_HEADS, N_BRANCHES * x2.shape[1])
    offs = [0]
    for sz in sizes:
        offs.append(offs[-1] + sz)
    seg = lambda i: w_in[:, offs[i]:offs[i + 1]]
    wi, wf, bias_i, bias_f = _stream_gate_columns(seg(9), c_gate_b)
    w_y = jnp.concatenate([seg(10), seg(6), seg(8), seg(3), seg(0), seg(1), seg(4), wi, wf], axis=1).astype(BF16)
    w_v = jnp.concatenate([seg(2), seg(5), seg(7)], axis=1).astype(BF16)

    xn = rmsnorm(x2, norm1_g)
    y2 = matmul(xn, w_y, out_dtype=F32, tm=2048)
    v2 = matmul(xn, w_v, out_dtype=BF16, tm=2048)
    y3 = y2.reshape(bsz, seq, Y_COLS)
    v3 = v2.reshape(bsz, seq, V_COLS)

    ya = dilated_mixer(y3, v3, cos_a, sin_a, a_q_gain, a_k_gain).reshape(bsz * seq, A_WIDTH)
    yb = gqa_attention(y3, v3, cos_b, sin_b, b_q_gain, b_k_gain).reshape(bsz * seq, B_Q_WIDTH)

    qk_c, kt_c = mlstm_conv(y3, c_conv_w, c_conv_b)
    cols, rt, wkt, dec = mlstm_gates(y3, bias_i, bias_f)
    dec_flat = dec[:, :, :C_STREAMS].reshape(-1)
    yc = mlstm_scan(dec_flat, qk_c, kt_c, v3, y3, c_out_g, cols, rt, wkt).reshape(bsz * seq, C_V_WIDTH)

    merged = branch_merge(ya, yb, yc, y2, w_branch_a.astype(BF16), w_branch_b.astype(BF16), w_branch_c.astype(BF16))
    h, hn = matmul_residual_norm(merged, w_out.astype(BF16), x2, norm2_g)
    u2 = matmul(hn, w_up.astype(BF16), out_dtype=BF16, relu2=True, tm=2048)
    return matmul(u2, w_down.astype(BF16), out_dtype=F32, res=h, tm=1024, tn=1024, tk=2048)


def kernel(x, norm1_g, w_in, a_q_gain, a_k_gain, b_q_gain, b_k_gain, c_conv_w, c_conv_b, c_gate_b, c_out_g, w_branch_a, w_branch_b, w_branch_c, w_out, norm2_g, w_up, w_down):
    bsz, seq, d_model = x.shape
    pos = jnp.arange(seq)
    cos1, sin1 = _rope_tables(pos, HEAD_DIM)
    cos_r, sin_r = _rope_tables(pos // GRID_W, HEAD_DIM // 2)
    cos_c, sin_c = _rope_tables(pos % GRID_W, HEAD_DIM // 2)
    tables = (jnp.concatenate([cos1, cos1], axis=1), jnp.concatenate([-sin1, sin1], axis=1),
              jnp.concatenate([cos_r, cos_r, cos_c, cos_c], axis=1),
              jnp.concatenate([-sin_r, sin_r, -sin_c, sin_c], axis=1))
    x2 = x.reshape(bsz * seq, d_model)
    for l in range(norm1_g.shape[0]):
        x2 = _layer(x2, (bsz, seq), tables, norm1_g[l], w_in[l], a_q_gain[l], a_k_gain[l], b_q_gain[l], b_k_gain[l],
                    c_conv_w[l], c_conv_b[l], c_gate_b[l], c_out_g[l], w_branch_a[l], w_branch_b[l], w_branch_c[l],
                    w_out[l], norm2_g[l], w_up[l], w_down[l])
    return x2.reshape(bsz, seq, d_model)
```

```python
import functools
import math

import jax
import jax.numpy as jnp
from jax import lax
from jax.experimental import pallas as pl
from jax.experimental.pallas import tpu as pltpu

F32 = jnp.float32
BF16 = jnp.bfloat16

HEAD_DIM = 128
GRID_W = 64
ROPE_THETA = 10000.0
EPS = 1e-6
NEG_INF = -1e30

A_HEADS = 6
A_PATTERNS = ((128, 1), (512, 4), (2048, 16))
A_REACH_BLOCK = 64
A_QUERY_BLOCK = 128
A_GROUP = 4
B_Q_HEADS = 8
B_KV_HEADS = 2
B_ROW_SLAB = 256
C_HEADS = 8
C_QK_DIM = 64
C_V_DIM = 128
C_CONV = 5
N_BRANCHES = 3

A_WIDTH = A_HEADS * HEAD_DIM
B_Q_WIDTH = B_Q_HEADS * HEAD_DIM
B_KV_WIDTH = B_KV_HEADS * HEAD_DIM
C_QK_WIDTH = C_HEADS * C_QK_DIM
C_V_WIDTH = C_HEADS * C_V_DIM

LANES = 128
MLSTM_CHUNK = 128
C_PAIRS = C_HEADS // 2
C_STREAMS = 2 * C_HEADS
VMEM_LIMIT = 60 * 1024 * 1024

Y_G = 0
Y_CQK = 6144
Y_CO = 7168
Y_BQ = 8192
Y_AQ = 9216
Y_AK = 9984
Y_BK = 10752
Y_GI = 11008
Y_GF = 11136
Y_COLS = 11264
V_AV = 0
V_BV = 768
V_CV = 1024
V_COLS = 2048


def _cparams(sem):
    return pltpu.CompilerParams(dimension_semantics=sem, vmem_limit_bytes=VMEM_LIMIT)


def _rmsnorm_kernel(x_ref, g_ref, o_ref):
    x = x_ref[...]
    r = lax.rsqrt(jnp.mean(x * x, axis=-1, keepdims=True) + EPS)
    o_ref[...] = ((x * r) * g_ref[...]).astype(o_ref.dtype)


def rmsnorm(x, g, tm=512):
    m, d = x.shape
    return pl.pallas_call(
        _rmsnorm_kernel,
        grid=(m // tm,),
        in_specs=[pl.BlockSpec((tm, d), lambda i: (i, 0)), pl.BlockSpec((1, d), lambda i: (0, 0))],
        out_specs=pl.BlockSpec((tm, d), lambda i: (i, 0)),
        out_shape=jax.ShapeDtypeStruct((m, d), BF16),
        compiler_params=_cparams(("parallel",)),
        name="rmsnorm",
    )(x, g.reshape(1, d))


def _matmul_kernel(*refs, nk, relu2, has_res):
    if has_res:
        a_ref, w_ref, r_ref, o_ref = refs
    else:
        a_ref, w_ref, o_ref = refs
        r_ref = None

    prod = jnp.dot(a_ref[...], w_ref[...], preferred_element_type=F32)
    if relu2:
        u = jnp.maximum(prod, 0.0)
        prod = u * u
    if nk == 1:
        o_ref[...] = (prod if r_ref is None else r_ref[...] + prod).astype(o_ref.dtype)
    else:
        k = pl.program_id(2)

        @pl.when(k == 0)
        def _():
            o_ref[...] = prod if r_ref is None else r_ref[...] + prod

        @pl.when(k > 0)
        def _():
            o_ref[...] += prod


def matmul(a, w, *, out_dtype, res=None, relu2=False, tm=1024, tn=1024, tk=2048):
    m, kdim = a.shape
    n = w.shape[1]
    tn = min(tn, n)
    tk = min(tk, kdim)
    nk = kdim // tk
    assert nk == 1 or (out_dtype == F32 and not relu2)
    in_specs = [pl.BlockSpec((tm, tk), lambda i, j, k: (i, k)),
                pl.BlockSpec((tk, tn), lambda i, j, k: (k, j))]
    args = [a, w]
    if res is not None:
        in_specs.append(pl.BlockSpec((tm, tn), lambda i, j, k: (i, j)))
        args.append(res)
    return pl.pallas_call(
        functools.partial(_matmul_kernel, nk=nk, relu2=relu2, has_res=res is not None),
        grid=(m // tm, n // tn, nk),
        in_specs=in_specs,
        out_specs=pl.BlockSpec((tm, tn), lambda i, j, k: (i, j)),
        out_shape=jax.ShapeDtypeStruct((m, n), out_dtype),
        compiler_params=_cparams(("parallel", "parallel", "arbitrary")),
        name="matmul",
    )(*args)


def _out_norm_kernel(a_ref, w_ref, r_ref, g_ref, h_ref, hn_ref):
    h = r_ref[...] + jnp.dot(a_ref[...], w_ref[...], preferred_element_type=F32)
    h_ref[...] = h
    r = lax.rsqrt(jnp.mean(h * h, axis=-1, keepdims=True) + EPS)
    hn_ref[...] = ((h * r) * g_ref[...]).astype(hn_ref.dtype)


def matmul_residual_norm(a, w, res, g, tm=512):
    m, kdim = a.shape
    n = w.shape[1]
    row = lambda width: pl.BlockSpec((tm, width), lambda i: (i, 0))
    return pl.pallas_call(
        _out_norm_kernel,
        grid=(m // tm,),
        in_specs=[row(kdim), pl.BlockSpec((kdim, n), lambda i: (0, 0)), row(n), pl.BlockSpec((1, n), lambda i: (0, 0))],
        out_specs=[row(n), row(n)],
        out_shape=[jax.ShapeDtypeStruct((m, n), F32), jax.ShapeDtypeStruct((m, n), BF16)],
        compiler_params=_cparams(("parallel",)),
        name="matmul_residual_norm",
    )(a, w, res, g.reshape(1, n))


def _norm_rope(x, gain, cos_full, sin_signed, half):
    y = (x * lax.rsqrt(jnp.mean(x * x, axis=-1, keepdims=True) + EPS)) * gain
    if 2 * half == LANES:
        partner = pltpu.roll(y, half, 1)
    else:
        lane = lax.broadcasted_iota(jnp.int32, y.shape, 1)
        partner = jnp.where(lane % (2 * half) < half, pltpu.roll(y, LANES - half, 1), pltpu.roll(y, half, 1))
    return y * cos_full + partner * sin_signed


def _dilated_kernel(q_ref, k_ref, v_ref, cos_ref, sin_ref, qg_ref, kg_ref, y_ref,
                    qn, kn, vn, kd, vd, acc, mrun, zrun, sbuf, pbuf, mblk, bias, *, seq):
    blk = A_REACH_BLOCK
    qb = A_QUERY_BLOCK
    kw = qb + 2 * blk
    grp = A_GROUP
    tile = 512
    scale = HEAD_DIM ** -0.5

    def prep(i, carry):
        rows = pl.ds(pl.multiple_of(i * tile, tile), tile)
        cos = cos_ref[rows, :]
        sin = sin_ref[rows, :]
        qn[rows, :] = _norm_rope(q_ref[0, rows, :], qg_ref[...], cos, sin, HEAD_DIM // 2)
        kn[rows, :] = _norm_rope(k_ref[0, rows, :], kg_ref[...], cos, sin, HEAD_DIM // 2)
        vn[rows, :] = v_ref[0, rows, :].astype(F32)
        return carry

    lax.fori_loop(0, seq // tile, prep, 0)
    zpad = jnp.zeros((blk, HEAD_DIM), BF16)
    kd[0:blk, :] = zpad
    kd[blk + seq:, :] = zpad
    vd[0:blk, :] = zpad
    vd[blk + seq:, :] = zpad

    t_io = lax.broadcasted_iota(jnp.int32, (qb, kw), 0)
    j_io = lax.broadcasted_iota(jnp.int32, (qb, kw), 1)
    in_reach = jnp.abs(j_io - blk - t_io) <= blk
    bias[0] = jnp.where(in_reach, 0.0, NEG_INF)
    bias[1] = jnp.where(in_reach & (j_io >= blk), 0.0, NEG_INF)
    bias[2] = jnp.where(in_reach & (j_io < qb + blk), 0.0, NEG_INF)
    nblk = seq // qb
    ones = jnp.ones((kw, HEAD_DIM), BF16)

    for pi, (_, d) in enumerate(sorted(A_PATTERNS, key=lambda wd: -wd[1])):
        sub_len = seq // d
        chunk = 256
        per_res = sub_len // chunk
        blk_per_res = sub_len // qb
        first = pi == 0

        def gather(idx, carry, d=d, sub_len=sub_len, per_res=per_res):
            r = idx // per_res
            j = idx % per_res
            if d == 1:
                src = pl.ds(pl.multiple_of(j * chunk, chunk), chunk)
            else:
                src = pl.ds(r + j * chunk * d, chunk, stride=d)
            dst = pl.ds(pl.multiple_of(blk + r * sub_len + j * chunk, blk), chunk)
            kd[dst, :] = kn[src, :].astype(BF16)
            vd[dst, :] = vn[src, :].astype(BF16)
            return carry

        lax.fori_loop(0, seq // chunk, gather, 0)

        def natural_rows(n, d=d, blk_per_res=blk_per_res):
            if d == 1:
                return pl.ds(pl.multiple_of(n * qb, qb), qb)
            return pl.ds(n // blk_per_res + (n % blk_per_res) * qb * d, qb, stride=d)

        def packed_rows(n, size=qb):
            return pl.ds(pl.multiple_of(n * qb, qb), size)

        def scores(t, blk_per_res=blk_per_res):
            for g in range(grp):
                n = t * grp + g
                q = qn[natural_rows(n), :].astype(BF16)
                s = lax.dot_general(q, kd[packed_rows(n, kw), :], (((1,), (1,)), ((), ())),
                                    preferred_element_type=F32) * scale
                i = n % blk_per_res
                kind = jnp.where(i == 0, 1, jnp.where(i == blk_per_res - 1, 2, 0))
                sbuf[packed_rows(n), :] = s + bias[kind]

        def softmax(t):
            for g in range(grp):
                rows = packed_rows(t * grp + g)
                s = sbuf[rows, :]
                m = jnp.max(s, axis=-1, keepdims=True)
                pbuf[rows, :] = jnp.exp(s - m).astype(BF16)
                mblk[rows, :] = jnp.broadcast_to(m, (qb, HEAD_DIM))

        def values(t, first=first):
            for g in range(grp):
                n = t * grp + g
                vaug = jnp.concatenate([vd[packed_rows(n, kw), :], ones], axis=1)
                oz = jnp.dot(pbuf[packed_rows(n), :], vaug, preferred_element_type=F32)
                o = oz[:, :HEAD_DIM]
                z = oz[:, HEAD_DIM:]
                if first:
                    rw = natural_rows(n)
                    acc[rw, :] = o
                    zrun[rw, :] = z
                    mrun[rw, :] = mblk[packed_rows(n), :]
                else:
                    rw = natural_rows(n)
                    m_old = mrun[rw, :]
                    mb = mblk[packed_rows(n), :]
                    m_new = jnp.maximum(m_old, mb)
                    w_old = jnp.exp(m_old - m_new)
                    w_blk = jnp.exp(mb - m_new)
                    acc[rw, :] = w_old * acc[rw, :] + w_blk * o
                    zrun[rw, :] = w_old * zrun[rw, :] + w_blk * z
                    mrun[rw, :] = m_new

        ngroups = nblk // grp
        scores(0)
        softmax(0)
        scores(1)

        def trip(t, carry):
            values(t - 2)
            softmax(t - 1)
            scores(t)
            return carry

        lax.fori_loop(2, ngroups, trip, 0)
        values(ngroups - 2)
        softmax(ngroups - 1)
        values(ngroups - 1)

    def finish(i, carry):
        rows = pl.ds(pl.multiple_of(i * tile, tile), tile)
        y_ref[0, rows, :] = (acc[rows, :] / zrun[rows, :]).astype(y_ref.dtype)
        return carry

    lax.fori_loop(0, seq // tile, finish, 0)


def dilated_mixer(y3, v3, cos_full, sin_signed, q_gain, k_gain):
    b, s, _ = y3.shape
    assert all(w // (2 * d) == A_REACH_BLOCK for w, d in A_PATTERNS)
    assert all((s // d) % 256 == 0 for _, d in A_PATTERNS) and (s // A_QUERY_BLOCK) % A_GROUP == 0 and s // A_QUERY_BLOCK >= 3 * A_GROUP
    kw = A_QUERY_BLOCK + 2 * A_REACH_BLOCK
    head = lambda off: pl.BlockSpec((1, s, HEAD_DIM), lambda bi, h: (bi, 0, off // HEAD_DIM + h))
    table = pl.BlockSpec((s, HEAD_DIM), lambda bi, h: (0, 0))
    vec = pl.BlockSpec((1, HEAD_DIM), lambda bi, h: (0, 0))
    pad_rows = s + 2 * A_REACH_BLOCK
    return pl.pallas_call(
        functools.partial(_dilated_kernel, seq=s),
        grid=(b, A_HEADS),
        in_specs=[head(Y_AQ), head(Y_AK), head(V_AV), table, table, vec, vec],
        out_specs=pl.BlockSpec((1, s, HEAD_DIM), lambda bi, h: (bi, 0, h)),
        out_shape=jax.ShapeDtypeStruct((b, s, A_WIDTH), BF16),
        scratch_shapes=[pltpu.VMEM((s, HEAD_DIM), F32)] * 3 + [pltpu.VMEM((pad_rows, HEAD_DIM), BF16)] * 2
        + [pltpu.VMEM((s, HEAD_DIM), F32)] * 3
        + [pltpu.VMEM((s, kw), F32), pltpu.VMEM((s, kw), BF16), pltpu.VMEM((s, HEAD_DIM), F32)]
        + [pltpu.VMEM((3, A_QUERY_BLOCK, kw), F32)],
        compiler_params=_cparams(("parallel", "parallel")),
        name="dilated_mixer",
    )(y3, y3, v3, cos_full, sin_signed, q_gain.reshape(1, HEAD_DIM), k_gain.reshape(1, HEAD_DIM))


def _gqa_kernel(q_ref, k_ref, v_ref, cosq_ref, sinq_ref, cosk_ref, sink_ref, qg_ref, kg_ref, o_ref, kn, vaug, *, group, seq):
    half = HEAD_DIM // 4
    tile = 512

    @pl.when(pl.program_id(2) == 0)
    def _():
        def prep(i, carry):
            rows = pl.ds(pl.multiple_of(i * tile, tile), tile)
            kn[rows, :] = _norm_rope(k_ref[0, rows, :], kg_ref[...], cosk_ref[rows, :], sink_ref[rows, :],
                                     half).astype(kn.dtype)
            return carry

        lax.fori_loop(0, seq // tile, prep, 0)

        vaug[:, :HEAD_DIM] = v_ref[0]
        vaug[:, HEAD_DIM:] = jnp.ones((seq, HEAD_DIM), vaug.dtype)

    k = kn[...]
    v = vaug[...]
    c = (HEAD_DIM ** -0.5) * math.log2(math.e)
    slab = B_ROW_SLAB
    units = [(g, r) for g in range(group) for r in range(q_ref.shape[1] // slab)]

    def scores(unit):
        g, r = unit
        rows = slice(r * slab, (r + 1) * slab)
        q = _norm_rope(q_ref[0, rows, g * HEAD_DIM:(g + 1) * HEAD_DIM], qg_ref[...], cosq_ref[rows, :],
                       sinq_ref[rows, :], half)
        return lax.dot_general(q.astype(BF16), k, (((1,), (1,)), ((), ())), preferred_element_type=F32)

    def probs(s):
        return jnp.exp2((s - jnp.max(s, axis=-1, keepdims=True)) * c).astype(BF16)

    def values(unit, p):
        g, r = unit
        o = jnp.dot(p, v, preferred_element_type=F32)
        o_ref[0, r * slab:(r + 1) * slab, g * HEAD_DIM:(g + 1) * HEAD_DIM] = (
            o[:, :HEAD_DIM] / o[:, HEAD_DIM:]).astype(o_ref.dtype)

    s_next = scores(units[0])
    p_prev = None
    for u, unit in enumerate(units):
        s_cur = s_next
        if u + 1 < len(units):
            s_next = scores(units[u + 1])
        if p_prev is not None:
            values(units[u - 1], p_prev)
        p_prev = probs(s_cur)
    values(units[-1], p_prev)


def gqa_attention(y3, v3, cos_full, sin_signed, q_gain, k_gain, tq=1024):
    b, s, _ = y3.shape
    group = B_Q_HEADS // B_KV_HEADS
    gw = group * HEAD_DIM
    vec = pl.BlockSpec((1, HEAD_DIM), lambda bi, h, t: (0, 0))
    qtab = pl.BlockSpec((tq, HEAD_DIM), lambda bi, h, t: (t, 0))
    ktab = pl.BlockSpec((s, HEAD_DIM), lambda bi, h, t: (0, 0))
    return pl.pallas_call(
        functools.partial(_gqa_kernel, group=group, seq=s),
        grid=(b, B_KV_HEADS, s // tq),
        in_specs=[pl.BlockSpec((1, tq, gw), lambda bi, h, t: (bi, t, Y_BQ // gw + h)),
                  pl.BlockSpec((1, s, HEAD_DIM), lambda bi, h, t: (bi, 0, Y_BK // HEAD_DIM + h)),
                  pl.BlockSpec((1, s, HEAD_DIM), lambda bi, h, t: (bi, 0, V_BV // HEAD_DIM + h)),
                  qtab, qtab, ktab, ktab, vec, vec],
        out_specs=pl.BlockSpec((1, tq, gw), lambda bi, h, t: (bi, t, h)),
        out_shape=jax.ShapeDtypeStruct((b, s, B_Q_WIDTH), BF16),
        scratch_shapes=[pltpu.VMEM((s, HEAD_DIM), BF16), pltpu.VMEM((s, 2 * HEAD_DIM), BF16)],
        compiler_params=_cparams(("parallel", "parallel", "arbitrary")),
        name="gqa_attention",
    )(y3, y3, v3, cos_full, sin_signed, cos_full, sin_signed, q_gain.reshape(1, HEAD_DIM), k_gain.reshape(1, HEAD_DIM))


def _conv_kernel(xp_ref, xm_ref, xn_ref, w_ref, b_ref, qk_ref, kt_ref, buf, *, tile, ntiles):
    ti = pl.program_id(1)
    halo = 8
    buf[0:halo] = jnp.where(ti > 0, xp_ref[0], 0.0)
    buf[halo:halo + tile] = xm_ref[0]
    buf[halo + tile:] = jnp.where(ti < ntiles - 1, xn_ref[0], 0.0)
    pad = C_CONV // 2
    acc = w_ref[0:1, :] * buf[halo - pad:halo - pad + tile, :]
    for j in range(1, C_CONV):
        acc = acc + w_ref[j:j + 1, :] * buf[halo - pad + j:halo - pad + j + tile, :]
    acc = acc + b_ref[...]
    act = acc * jax.nn.sigmoid(acc)
    q = act[:, :C_QK_WIDTH]
    k = act[:, C_QK_WIDTH:] * (C_QK_DIM ** -0.5)
    qk_ref[0, :, :C_QK_WIDTH] = q.astype(qk_ref.dtype)
    qk_ref[0, :, C_QK_WIDTH:] = k.astype(qk_ref.dtype)
    for c in range(tile // MLSTM_CHUNK):
        kt_ref[0, c] = k[c * MLSTM_CHUNK:(c + 1) * MLSTM_CHUNK, :].T.astype(kt_ref.dtype)


def mlstm_conv(y3, conv_w, conv_b, tile=512):
    b, s, _ = y3.shape
    ntiles = s // tile
    w2 = 2 * C_QK_WIDTH
    cb = Y_CQK // w2
    per = tile // 8
    return pl.pallas_call(
        functools.partial(_conv_kernel, tile=tile, ntiles=ntiles),
        grid=(b, ntiles),
        in_specs=[pl.BlockSpec((1, 8, w2), lambda bi, t: (bi, jnp.maximum(t * per - 1, 0), cb)),
                  pl.BlockSpec((1, tile, w2), lambda bi, t: (bi, t, cb)),
                  pl.BlockSpec((1, 8, w2), lambda bi, t: (bi, jnp.minimum((t + 1) * per, s // 8 - 1), cb)),
                  pl.BlockSpec((C_CONV, w2), lambda bi, t: (0, 0)),
                  pl.BlockSpec((1, w2), lambda bi, t: (0, 0))],
        out_specs=[pl.BlockSpec((1, tile, w2), lambda bi, t: (bi, t, 0)),
                   pl.BlockSpec((1, tile // MLSTM_CHUNK, C_QK_WIDTH, MLSTM_CHUNK), lambda bi, t: (bi, t, 0, 0))],
        out_shape=[jax.ShapeDtypeStruct((b, s, w2), BF16),
                   jax.ShapeDtypeStruct((b, s // MLSTM_CHUNK, C_QK_WIDTH, MLSTM_CHUNK), BF16)],
        scratch_shapes=[pltpu.VMEM((tile + 16, w2), F32)],
        compiler_params=_cparams(("parallel", "parallel")),
        name="mlstm_conv",
    )(y3, y3, y3, conv_w, conv_b.reshape(1, w2))


def _gate_kernel(gi_ref, gf_ref, bi_ref, bf_ref, cols_ref, rt_ref, wkt_ref, dec_ref,
                 b_s, r_s, dmax_s, btot_s, gmax_s, mpf_s, mpb_s, *, nc):
    lc = MLSTM_CHUNK
    t_in = lax.broadcasted_iota(jnp.int32, (lc, LANES), 0)
    lane = lax.broadcasted_iota(jnp.int32, (lc, LANES), 1)
    fwd = lane % 2 == 0
    fwd_row = fwd[0:1]
    steps = [1 << i for i in range(lc.bit_length() - 1)]

    def chunk_rows(c):
        return pl.ds(pl.multiple_of(c * lc, lc), lc)

    def local(c, carry):
        rows = chunk_rows(c)
        ii = gi_ref[0, rows, :] + bi_ref[...]
        ff = gf_ref[0, rows, :] + bf_ref[...]
        lf = jnp.minimum(ff, 0.0) - jnp.log1p(jnp.exp(-jnp.abs(ff)))
        cs = lf
        rs = lf
        for k in steps:
            cs = cs + jnp.where(t_in >= k, pltpu.roll(cs, k, 0), 0.0)
            rs = rs + jnp.where(t_in + k < lc, pltpu.roll(rs, lc - k, 0), 0.0)
        b = jnp.where(fwd, cs, rs)
        btot = cs + rs - lf
        r = ii - b
        pm = r
        sm = r
        for k in steps:
            pm = jnp.where(t_in >= k, jnp.maximum(pm, pltpu.roll(pm, k, 0)), pm)
            sm = jnp.where(t_in + k < lc, jnp.maximum(sm, pltpu.roll(sm, lc - k, 0)), sm)
        b_s[rows, :] = b
        r_s[rows, :] = r
        dmax_s[rows, :] = b + jnp.where(fwd, pm, sm)
        btot_s[pl.ds(c, 1), :] = btot[0:1]
        gmax_s[pl.ds(c, 1), :] = (btot + jnp.maximum(pm, sm))[0:1]
        return carry

    lax.fori_loop(0, nc, local, 0)

    def scan(k, carry):
        mf, mb = carry
        mpf_s[pl.ds(k, 1), :] = mf
        mf = jnp.maximum(btot_s[pl.ds(k, 1), :] + mf, gmax_s[pl.ds(k, 1), :])
        kb = nc - 1 - k
        mpb_s[pl.ds(kb, 1), :] = mb
        mb = jnp.maximum(btot_s[pl.ds(kb, 1), :] + mb, gmax_s[pl.ds(kb, 1), :])
        return mf, mb

    zero = jnp.zeros((1, LANES), F32)
    lax.fori_loop(0, nc, scan, (zero, zero))

    def emit(c, carry):
        rows = chunk_rows(c)
        b = b_s[rows, :]
        r = r_s[rows, :]
        btot = btot_s[pl.ds(c, 1), :]
        mprev = jnp.where(fwd_row, mpf_s[pl.ds(c, 1), :], mpb_s[pl.ds(c, 1), :])
        mnew = jnp.maximum(btot + mprev, gmax_s[pl.ds(c, 1), :])
        dec_ref[0, pl.ds(c, 1), :] = jnp.exp(btot + mprev - mnew)
        m_inter = b + mprev
        m_t = jnp.maximum(m_inter, dmax_s[rows, :])
        cb = b - m_t
        inter = jnp.exp(m_inter - m_t)
        e = jnp.exp(-m_t)
        for p in range(C_PAIRS):
            cols_ref[0, p, rows, :] = jnp.where(
                lane < 4, pltpu.roll(cb, (LANES - 4 * p) % LANES, 1),
                jnp.where(lane < 8, pltpu.roll(inter, (LANES + 4 - 4 * p) % LANES, 1),
                          pltpu.roll(e, (LANES + 8 - 4 * p) % LANES, 1)))
        wk = jnp.exp(btot + r - mnew)
        rt_ref[0, c] = r.T[:C_STREAMS, :]
        wkt_ref[0, c] = wk.T[:C_STREAMS, :]
        return carry

    lax.fori_loop(0, nc, emit, 0)


def mlstm_gates(y3, bias_i, bias_f):
    b, s, _ = y3.shape
    nc = s // MLSTM_CHUNK
    row = pl.BlockSpec((1, nc, C_STREAMS, MLSTM_CHUNK), lambda bi: (bi, 0, 0, 0))
    vec = pl.BlockSpec((1, LANES), lambda bi: (0, 0))
    return pl.pallas_call(
        functools.partial(_gate_kernel, nc=nc),
        grid=(b,),
        in_specs=[pl.BlockSpec((1, s, LANES), lambda bi: (bi, 0, Y_GI // LANES)),
                  pl.BlockSpec((1, s, LANES), lambda bi: (bi, 0, Y_GF // LANES)), vec, vec],
        out_specs=[pl.BlockSpec((1, C_PAIRS, s, LANES), lambda bi: (bi, 0, 0, 0)), row, row,
                   pl.BlockSpec((1, nc, LANES), lambda bi: (bi, 0, 0))],
        out_shape=[jax.ShapeDtypeStruct((b, C_PAIRS, s, LANES), F32)]
        + [jax.ShapeDtypeStruct((b, nc, C_STREAMS, MLSTM_CHUNK), F32)] * 2
        + [jax.ShapeDtypeStruct((b, nc, LANES), F32)],
        scratch_shapes=[pltpu.VMEM((s, LANES), F32)] * 3 + [pltpu.VMEM((nc, LANES), F32)] * 4,
        compiler_params=_cparams(("parallel",)),
        name="mlstm_gates",
    )(y3, y3, bias_i.reshape(1, LANES), bias_f.reshape(1, LANES))


def _mlstm_kernel(dec_ref, q_ref, kt_ref, v_ref, o_ref, g_ref, cols_ref, rt_ref, wkt_ref,
                  y_ref, akw, hsum, ct, *, seq, nc):
    lc = MLSTM_CHUNK
    bi = pl.program_id(0)
    pair = pl.program_id(1)
    lane = lax.broadcasted_iota(jnp.int32, (lc, LANES), 1)
    trow = lax.broadcasted_iota(jnp.int32, (lc, lc), 0)
    scol = lax.broadcasted_iota(jnp.int32, (lc, lc), 1)
    ones = jnp.ones((lc, C_V_DIM), BF16)

    def head_q(q2, hh):
        return jnp.where((lane >= hh * C_QK_DIM) & (lane < (hh + 1) * C_QK_DIM), q2, jnp.zeros_like(q2))

    def intra(c, carry):
        rows = pl.ds(pl.multiple_of(c * lc, lc), lc)
        q2 = q_ref[0, rows, :]
        kt2 = kt_ref[0, c]
        for hh in range(2):
            sc = jnp.dot(head_q(q2, hh), kt2, preferred_element_type=F32)
            kth = kt2[hh * C_QK_DIM:(hh + 1) * C_QK_DIM, :].astype(F32)
            for direction in range(2):
                sl = hh * 2 + direction
                stream = pair * 4 + sl
                mask = (scol <= trow) if direction == 0 else (scol >= trow)
                arg = cols_ref[0, 0, rows, sl:sl + 1] + rt_ref[0, c, pl.ds(stream, 1), :]
                akw[sl, c, :lc, :] = (jnp.exp(jnp.where(mask, arg, NEG_INF)) * sc).astype(BF16)
                akw[sl, c, lc:, :] = (kth * wkt_ref[0, c, pl.ds(stream, 1), :]).astype(BF16)
        return carry

    ct[...] = jnp.zeros_like(ct)
    hsum[...] = jnp.zeros_like(hsum)

    def step(it, carry):
        work = []
        for direction in range(2):
            c = it if direction == 0 else nc - 1 - it
            rows = pl.ds(pl.multiple_of(c * lc, lc), lc)
            q2 = q_ref[0, rows, :]
            ct_pair = ct[direction].astype(BF16)
            for hh in range(2):
                sl = hh * 2 + direction
                vaug = jnp.concatenate([v_ref[0, rows, hh * C_V_DIM:(hh + 1) * C_V_DIM], ones], axis=1)
                num_a = jnp.dot(akw[sl, c, :lc, :], vaug, preferred_element_type=F32)
                num_c = jnp.dot(head_q(q2, hh), ct_pair, preferred_element_type=F32)
                u = jnp.dot(akw[sl, c, lc:, :], vaug, preferred_element_type=F32)
                work.append((direction, hh, sl, c, rows, num_a, num_c, u))
        for direction, hh, sl, c, rows, num_a, num_c, u in work:
            dec = dec_ref[(bi * nc + c) * C_STREAMS + pair * 4 + sl]
            num = num_a + cols_ref[0, 0, rows, 4 + sl:5 + sl] * num_c
            den = jnp.maximum(jnp.abs(num[:, C_V_DIM:]), cols_ref[0, 0, rows, 8 + sl:9 + sl])
            hsum[rows, hh * C_V_DIM:(hh + 1) * C_V_DIM] += num[:, :C_V_DIM] / den
            d_rows = slice(hh * C_QK_DIM, (hh + 1) * C_QK_DIM)
            ct[direction, d_rows, :] = dec * ct[direction, d_rows, :] + u
        return carry

    intra(0, 0)
    intra(nc - 1, 0)
    half = nc // 2

    def fused(it, carry):
        step(it, carry)
        intra(it + 1, carry)
        intra(nc - 2 - it, carry)
        return carry

    lax.fori_loop(0, half - 1, fused, 0)
    lax.fori_loop(half - 1, nc, step, 0, unroll=2)

    tile = 512

    def finish(i, carry):
        rows = pl.ds(pl.multiple_of(i * tile, tile), tile)
        for hh in range(2):
            cols = slice(hh * C_V_DIM, (hh + 1) * C_V_DIM)
            h = hsum[rows, cols]
            h = h * lax.rsqrt(jnp.mean(h * h, axis=-1, keepdims=True) + EPS)
            h = h * g_ref[:, cols]
            y_ref[0, rows, cols] = (h * jax.nn.sigmoid(o_ref[0, rows, cols])).astype(y_ref.dtype)
        return carry

    lax.fori_loop(0, seq // tile, finish, 0)


def mlstm_scan(dec, qk, kt, v3, y3, out_g, cols, rt, wkt):
    b, s, _ = qk.shape
    nc = s // MLSTM_CHUNK
    pw = 2 * C_V_DIM
    col = pl.BlockSpec((1, 1, s, LANES), lambda bi, p, d: (bi, p, 0, 0))
    row = pl.BlockSpec((1, nc, C_STREAMS, MLSTM_CHUNK), lambda bi, p, d: (bi, 0, 0, 0))
    grid_spec = pltpu.PrefetchScalarGridSpec(
        num_scalar_prefetch=1,
        grid=(b, C_PAIRS),
        in_specs=[pl.BlockSpec((1, s, LANES), lambda bi, p, d: (bi, 0, p)),
                  pl.BlockSpec((1, nc, LANES, MLSTM_CHUNK), lambda bi, p, d: (bi, 0, p, 0)),
                  pl.BlockSpec((1, s, pw), lambda bi, p, d: (bi, 0, V_CV // pw + p)),
                  pl.BlockSpec((1, s, pw), lambda bi, p, d: (bi, 0, Y_CO // pw + p)),
                  pl.BlockSpec((1, pw), lambda bi, p, d: (0, p)),
                  col, row, row],
        out_specs=pl.BlockSpec((1, s, pw), lambda bi, p, d: (bi, 0, p)),
        scratch_shapes=[pltpu.VMEM((4, nc, MLSTM_CHUNK + C_QK_DIM, MLSTM_CHUNK), BF16),
                        pltpu.VMEM((s, pw), F32),
                        pltpu.VMEM((2, LANES, 2 * C_V_DIM), F32)],
    )
    return pl.pallas_call(
        functools.partial(_mlstm_kernel, seq=s, nc=nc),
        grid_spec=grid_spec,
        out_shape=jax.ShapeDtypeStruct((b, s, C_HEADS * C_V_DIM), BF16),
        compiler_params=_cparams(("parallel", "arbitrary")),
        name="mlstm_scan",
    )(dec, qk, kt, v3, y3, out_g.reshape(1, C_V_WIDTH), cols, rt, wkt)


def _merge_kernel(ya_ref, yb_ref, yc_ref, ga_ref, gb_ref, gc_ref, wa_ref, wb_ref, wc_ref, o_ref):
    acc = jax.nn.sigmoid(ga_ref[...]) * jnp.dot(ya_ref[...], wa_ref[...], preferred_element_type=F32)
    acc = acc + jax.nn.sigmoid(gb_ref[...]) * jnp.dot(yb_ref[...], wb_ref[...], preferred_element_type=F32)
    acc = acc + jax.nn.sigmoid(gc_ref[...]) * jnp.dot(yc_ref[...], wc_ref[...], preferred_element_type=F32)
    o_ref[...] = acc.astype(o_ref.dtype)


def branch_merge(ya, yb, yc, y2, wa, wb, wc, tm=512, tn=1024):
    m = ya.shape[0]
    n = wa.shape[1]
    nj = n // tn

    def act(width):
        return pl.BlockSpec((tm, width), lambda j, i: (i, 0))

    def gate(branch):
        return pl.BlockSpec((tm, tn), lambda j, i: (i, Y_G // tn + branch * nj + j))

    def weight(width):
        return pl.BlockSpec((width, tn), lambda j, i: (0, j))

    return pl.pallas_call(
        _merge_kernel,
        grid=(nj, m // tm),
        in_specs=[act(A_WIDTH), act(B_Q_WIDTH), act(C_V_WIDTH), gate(0), gate(1), gate(2),
                  weight(A_WIDTH), weight(B_Q_WIDTH), weight(C_V_WIDTH)],
        out_specs=pl.BlockSpec((tm, tn), lambda j, i: (i, j)),
        out_shape=jax.ShapeDtypeStruct((m, n), BF16),
        compiler_params=_cparams(("parallel", "parallel")),
        name="branch_merge",
    )(ya, yb, yc, y2, y2, y2, wa, wb, wc)


def _rope_tables(pos, dim):
    inv_freq = ROPE_THETA ** (-jnp.arange(0, dim, 2, dtype=F32) / dim)
    ang = pos.astype(F32)[:, None] * inv_freq[None, :]
    return jnp.cos(ang), jnp.sin(ang)


def _stream_gate_columns(w_gate, b_gate):
    lanes = jnp.arange(C_STREAMS)
    head = (lanes // 4) * 2 + (lanes % 4) // 2
    direction = lanes % 2
    i_cols = direction * 2 * C_HEADS + head
    f_cols = i_cols + C_HEADS
    pad = LANES - C_STREAMS
    wi = jnp.pad(w_gate[:, i_cols], ((0, 0), (0, pad)))
    wf = jnp.pad(w_gate[:, f_cols], ((0, 0), (0, pad)))
    return wi, wf, jnp.pad(b_gate[i_cols], (0, pad)), jnp.pad(b_gate[f_cols], (0, pad))


def _layer(x2, shape, tables, norm1_g, w_in, a_q_gain, a_k_gain, b_q_gain, b_k_gain, c_conv_w, c_conv_b,
           c_gate_b, c_out_g, w_branch_a, w_branch_b, w_branch_c, w_out, norm2_g, w_up, w_down):
    bsz, seq = shape
    cos_a, sin_a, cos_b, sin_b = tables
    sizes = (A_WIDTH, A_WIDTH, A_WIDTH, B_Q_WIDTH, B_KV_WIDTH, B_KV_WIDTH, 2 * C_QK_WIDTH, C_V_WIDTH, C_V_WIDTH,
             4 * C_HEADS, N_BRANCHES * x2.shape[1])
    offs = [0]
    for sz in sizes:
        offs.append(offs[-1] + sz)
    w_in_b = w_in.astype(BF16)
    seg = lambda i: w_in_b[:, offs[i]:offs[i + 1]]
    wi, wf, bias_i, bias_f = _stream_gate_columns(seg(9), c_gate_b)
    w_y = jnp.concatenate([seg(10), seg(6), seg(8), seg(3), seg(0), seg(1), seg(4), wi, wf], axis=1)
    w_v = jnp.concatenate([seg(2), seg(5), seg(7)], axis=1)

    xn = rmsnorm(x2, norm1_g)
    y2 = matmul(xn, w_y, out_dtype=F32, tm=2048)
    v2 = matmul(xn, w_v, out_dtype=BF16, tm=2048)
    y3 = y2.reshape(bsz, seq, Y_COLS)
    v3 = v2.reshape(bsz, seq, V_COLS)

    ya = dilated_mixer(y3, v3, cos_a, sin_a, a_q_gain, a_k_gain).reshape(bsz * seq, A_WIDTH)
    yb = gqa_attention(y3, v3, cos_b, sin_b, b_q_gain, b_k_gain).reshape(bsz * seq, B_Q_WIDTH)

    qk_c, kt_c = mlstm_conv(y3, c_conv_w, c_conv_b)
    cols, rt, wkt, dec = mlstm_gates(y3, bias_i, bias_f)
    dec_flat = dec[:, :, :C_STREAMS].reshape(-1)
    yc = mlstm_scan(dec_flat, qk_c, kt_c, v3, y3, c_out_g, cols, rt, wkt).reshape(bsz * seq, C_V_WIDTH)

    merged = branch_merge(ya, yb, yc, y2, w_branch_a.astype(BF16), w_branch_b.astype(BF16), w_branch_c.astype(BF16))
    h, hn = matmul_residual_norm(merged, w_out.astype(BF16), x2, norm2_g)
    u2 = matmul(hn, w_up.astype(BF16), out_dtype=BF16, relu2=True, tm=2048)
    return matmul(u2, w_down.astype(BF16), out_dtype=F32, res=h, tm=1024, tn=1024, tk=2048)


def kernel(x, norm1_g, w_in, a_q_gain, a_k_gain, b_q_gain, b_k_gain, c_conv_w, c_conv_b, c_gate_b, c_out_g, w_branch_a, w_branch_b, w_branch_c, w_out, norm2_g, w_up, w_down):
    bsz, seq, d_model = x.shape
    pos = jnp.arange(seq)
    cos1, sin1 = _rope_tables(pos, HEAD_DIM)
    cos_r, sin_r = _rope_tables(pos // GRID_W, HEAD_DIM // 2)
    cos_c, sin_c = _rope_tables(pos % GRID_W, HEAD_DIM // 2)
    tables = (jnp.concatenate([cos1, cos1], axis=1), jnp.concatenate([-sin1, sin1], axis=1),
              jnp.concatenate([cos_r, cos_r, cos_c, cos_c], axis=1),
              jnp.concatenate([-sin_r, sin_r, -sin_c, sin_c], axis=1))
    x2 = x.reshape(bsz * seq, d_model)
    for l in range(norm1_g.shape[0]):
        x2 = _layer(x2, (bsz, seq), tables, norm1_g[l], w_in[l], a_q_gain[l], a_k_gain[l], b_q_gain[l], b_k_gain[l],
                    c_conv_w[l], c_conv_b[l], c_gate_b[l], c_out_g[l], w_branch_a[l], w_branch_b[l], w_branch_c[l],
                    w_out[l], norm2_g[l], w_up[l], w_down[l])
    return x2.reshape(bsz, seq, d_model)
```
